```python
import math
import jax, jax.numpy as jnp
from jax import lax
import numpy as np

D_MODEL = 1024
BATCH = 32
SEQ = 2048
DEPTH = 2

HEAD_DIM = 64
A_HEADS = 6
IDX_HEADS = 8
IDX_DIM = 64
A_TOPK_MAX = 256
A_Q_BLOCK = 128
B_SLOTS = 4
B_PATTERNS = ((128, 1), (512, 4), (2048, 16))
B_GROUPS = 3
B_PAD = 2048
B_Q_BLOCK = 64
C_HEADS = 6
C_BLOCK = 256
C_TOPK = 3
C_Q_CHUNK = 32
N_BUCKETS = 32
MAX_DISTANCE = 2048
N_BIAS_HEADS = A_HEADS + B_GROUPS * B_SLOTS + C_HEADS
D_FF = 4 * D_MODEL
PLE_DIM = 256
N_BRANCH = 3
ALPHA = (2 * DEPTH) ** 0.25
BETA = (8 * DEPTH) ** -0.25
LN_EPS = 1e-5
NEG = -1e30

IN_WIDTHS = (
    A_HEADS * HEAD_DIM, HEAD_DIM, HEAD_DIM,
    IDX_HEADS * IDX_DIM, IDX_DIM, IDX_HEADS,
    B_GROUPS * B_SLOTS * HEAD_DIM, B_SLOTS * HEAD_DIM, B_SLOTS * HEAD_DIM,
    C_HEADS * HEAD_DIM, C_HEADS * HEAD_DIM, C_HEADS * HEAD_DIM,
)
D_IN = sum(IN_WIDTHS)
SPLIT_POINTS = tuple(int(c) for c in np.cumsum(IN_WIDTHS)[:-1])

kernel_name = "hybrid_dsa_dilated_moba_deepnorm"


def layer_norm(x, g, b):
    xf = x.astype(jnp.float32)
    mu = jnp.mean(xf, axis=-1, keepdims=True)
    var = jnp.mean(jnp.square(xf - mu), axis=-1, keepdims=True)
    return ((xf - mu) * lax.rsqrt(var + LN_EPS) * g + b).astype(x.dtype)


def rel_bucket(dist):
    n = jnp.maximum(dist, 0)
    max_exact = N_BUCKETS // 2
    nf = jnp.maximum(n, 1).astype(jnp.float32)
    large = max_exact + (jnp.log(nf / max_exact) / math.log(MAX_DISTANCE / max_exact)
                         * (N_BUCKETS - max_exact)).astype(jnp.int32)
    large = jnp.minimum(large, N_BUCKETS - 1)
    return jnp.where(n < max_exact, n, large)


def masked_softmax(logits, mask):
    return jax.nn.softmax(jnp.where(mask, logits.astype(jnp.float32), NEG), axis=-1)


def dsa_mixer(q, k, v, iq, ik, iw, bias_tab):
    bsz, s_len = q.shape[0], q.shape[1]
    topk = min(A_TOPK_MAX, s_len // 4)
    scale = HEAD_DIM ** -0.5
    iscale = IDX_DIM ** -0.5
    kpos = jnp.arange(s_len)
    qloc = jnp.arange(A_Q_BLOCK)

    def block(t0):
        qb = lax.dynamic_slice_in_dim(q, t0, A_Q_BLOCK, axis=1)
        iqb = lax.dynamic_slice_in_dim(iq, t0, A_Q_BLOCK, axis=1)
        iwb = lax.dynamic_slice_in_dim(iw, t0, A_Q_BLOCK, axis=1)
        tpos = t0 + qloc
        sc = jnp.einsum('bqhd,bsd->bqhs', iqb, ik) * iscale
        index = jnp.einsum('bqh,bqhs->bqs', iwb, jax.nn.relu(sc)).astype(jnp.float32)
        causal = kpos[None, :] <= tpos[:, None]
        index = jnp.where(causal[None], index, -jnp.inf)
        _, sel = lax.top_k(index, topk)
        ksel = jax.vmap(lambda kb, ib: kb[ib])(k, sel)
        vsel = jax.vmap(lambda vb, ib: vb[ib])(v, sel)
        dist = tpos[None, :, None] - sel
        bias = jnp.moveaxis(bias_tab[rel_bucket(dist)], -1, 2)
        logits = jnp.einsum('bqhd,bqkd->bqhk', qb, ksel).astype(jnp.float32) * scale + bias
        probs = masked_softmax(logits, (dist >= 0)[:, :, None, :])
        return jnp.einsum('bqhk,bqkd->bqhd', probs.astype(v.dtype), vsel)

    starts = jnp.arange(s_len // A_Q_BLOCK) * A_Q_BLOCK
    out = lax.map(block, starts)
    return jnp.moveaxis(out, 0, 1).reshape(bsz, s_len, A_HEADS * HEAD_DIM)


def dilated_mixer(q, k, v, bias_tab):
    bsz, s_len = q.shape[0], q.shape[1]
    scale = HEAD_DIM ** -0.5
    kp = jnp.pad(k, ((0, 0), (B_PAD, 0), (0, 0), (0, 0)))
    vp = jnp.pad(v, ((0, 0), (B_PAD, 0), (0, 0), (0, 0)))
    qloc = jnp.arange(B_Q_BLOCK)

    def block(t0):
        qb = lax.dynamic_slice_in_dim(q, t0, B_Q_BLOCK, axis=1)
        tpos = t0 + qloc
        maxes, denoms, outs = [], [], []
        for g, (win, dil) in enumerate(B_PATTERNS):
            offs = jnp.arange(win // dil + 1) * dil
            kc = lax.dynamic_slice_in_dim(kp, t0 + B_PAD - win, B_Q_BLOCK + win, axis=1)
            vc = lax.dynamic_slice_in_dim(vp, t0 + B_PAD - win, B_Q_BLOCK + win, axis=1)
            lidx = qloc[:, None] + win - offs[None, :]
            kg = kc[:, lidx]
            vg = vc[:, lidx]
            bias = bias_tab[rel_bucket(offs), g * B_SLOTS:(g + 1) * B_SLOTS].T
            logits = jnp.einsum('bqhd,bqjhd->bqhj', qb[:, :, g], kg).astype(jnp.float32) * scale + bias
            valid = (tpos[:, None] - offs[None, :]) >= 0
            logits = jnp.where(valid[None, :, None, :], logits, NEG)
            m = jnp.max(logits, axis=-1, keepdims=True)
            e = jnp.exp(logits - m)
            den = jnp.sum(e, axis=-1, keepdims=True)
            outs.append(jnp.einsum('bqhj,bqjhd->bqhd', (e / den).astype(v.dtype), vg))
            maxes.append(m)
            denoms.append(den)
        m_all = jnp.stack(maxes)
        wts = jnp.stack(denoms) * jnp.exp(m_all - jnp.max(m_all, axis=0, keepdims=True))
        wts = wts / jnp.sum(wts, axis=0, keepdims=True)
        return jnp.sum(wts.astype(v.dtype) * jnp.stack(outs), axis=0)

    starts = jnp.arange(s_len // B_Q_BLOCK) * B_Q_BLOCK
    out = lax.map(block, starts)
    return jnp.moveaxis(out, 0, 1).reshape(bsz, s_len, B_SLOTS * HEAD_DIM)


def moba_mixer(q, k, v, bias_tab):
    bsz, s_len, n_h, dh = q.shape
    scale = dh ** -0.5
    nblk = -(-s_len // C_BLOCK)
    pad = nblk * C_BLOCK - s_len
    kp = jnp.pad(k, ((0, 0), (0, pad), (0, 0), (0, 0)))
    vp = jnp.pad(v, ((0, 0), (0, pad), (0, 0), (0, 0)))
    kb = kp.reshape(bsz, nblk, C_BLOCK, n_h, dh)
    k_mean = jnp.mean(kb, axis=2)
    kbh = jnp.moveaxis(kb, 3, 1)
    vbh = jnp.moveaxis(vp.reshape(bsz, nblk, C_BLOCK, n_h, dh), 3, 1)
    ntop = min(C_TOPK, nblk)
    bi = jnp.arange(bsz)[:, None, None, None]
    hi = jnp.arange(n_h)[None, None, :, None]
    tab_h = bias_tab.T
    kin = jnp.arange(C_BLOCK)
    qloc = jnp.arange(C_Q_CHUNK)

    def chunk(t0):
        qc = lax.dynamic_slice_in_dim(q, t0, C_Q_CHUNK, axis=1)
        tpos = t0 + qloc
        cur = t0 // C_BLOCK
        gate = jnp.einsum('bqhd,bnhd->bqhn', qc, k_mean).astype(jnp.float32)
        gate = jnp.where(jnp.arange(nblk) < cur, gate, -jnp.inf)
        _, sel = lax.top_k(gate, ntop)
        ksel = kbh[bi, hi, sel]
        vsel = vbh[bi, hi, sel]
        dist_sel = tpos[None, :, None, None, None] - (sel[..., None] * C_BLOCK + kin)
        bias_sel = tab_h[hi[..., None], rel_bucket(dist_sel)]
        l_sel = jnp.einsum('bqhd,bqhnkd->bqhnk', qc, ksel).astype(jnp.float32) * scale + bias_sel
        valid_sel = jnp.broadcast_to((sel < cur)[..., None], l_sel.shape)
        kown = lax.dynamic_slice_in_dim(kp, cur * C_BLOCK, C_BLOCK, axis=1)
        vown = lax.dynamic_slice_in_dim(vp, cur * C_BLOCK, C_BLOCK, axis=1)
        dist_own = tpos[:, None] - (cur * C_BLOCK + kin)[None, :]
        bias_own = jnp.moveaxis(bias_tab[rel_bucket(dist_own)], -1, 1)[None]
        l_own = jnp.einsum('bqhd,bkhd->bqhk', qc, kown).astype(jnp.float32) * scale + bias_own
        valid_own = jnp.broadcast_to((dist_own >= 0)[None, :, None, :], l_own.shape)
        n_sel = ntop * C_BLOCK
        logits = jnp.concatenate([l_sel.reshape(bsz, C_Q_CHUNK, n_h, n_sel), l_own], axis=-1)
        mask = jnp.concatenate([valid_sel.reshape(bsz, C_Q_CHUNK, n_h, n_sel), valid_own], axis=-1)
        probs = masked_softmax(logits, mask).astype(v.dtype)
        p_sel = probs[..., :n_sel].reshape(bsz, C_Q_CHUNK, n_h, ntop, C_BLOCK)
        p_own = probs[..., n_sel:]
        return (jnp.einsum('bqhnk,bqhnkd->bqhd', p_sel, vsel)
                + jnp.einsum('bqhk,bkhd->bqhd', p_own, vown))

    starts = jnp.arange(s_len // C_Q_CHUNK) * C_Q_CHUNK
    out = lax.map(chunk, starts)
    return jnp.moveaxis(out, 0, 1).reshape(bsz, s_len, n_h * dh)


def setup_inputs(seed: int = 0) -> dict:
    key = jax.random.key(seed)
    ks = jax.random.split(key, 20)

    def nrm(k, shape, fan_in, gain=1.0):
        return jax.random.normal(k, shape, jnp.float32) * (gain * fan_in ** -0.5)

    def small(k, shape, s=0.02):
        return jax.random.normal(k, shape, jnp.float32) * s

    a_w = A_HEADS * HEAD_DIM
    b_w = B_SLOTS * HEAD_DIM
    c_w = C_HEADS * HEAD_DIM
    return {
        "x": jax.random.normal(ks[0], (BATCH, SEQ, D_MODEL), jnp.float32),
        "p": jax.random.normal(ks[1], (DEPTH, BATCH, SEQ, PLE_DIM), jnp.float32),
        "w_in": nrm(ks[2], (DEPTH, D_MODEL, D_IN), D_MODEL),
        "w_gate": nrm(ks[3], (DEPTH, D_MODEL, N_BRANCH * D_MODEL), D_MODEL),
        "w_br_a": nrm(ks[4], (DEPTH, a_w, D_MODEL), a_w, BETA),
        "w_br_b": nrm(ks[5], (DEPTH, b_w, D_MODEL), b_w, BETA),
        "w_br_c": nrm(ks[6], (DEPTH, c_w, D_MODEL), c_w, BETA),
        "w_out": nrm(ks[7], (DEPTH, D_MODEL, D_MODEL), D_MODEL, BETA),
        "ln1_g": 1.0 + small(ks[8], (DEPTH, D_MODEL)),
        "ln1_b": small(ks[9], (DEPTH, D_MODEL)),
        "w_up": nrm(ks[10], (DEPTH, D_MODEL, D_FF), D_MODEL, BETA),
        "w_down": nrm(ks[11], (DEPTH, D_FF, D_MODEL), D_FF, BETA),
        "w_ple_gate": nrm(ks[12], (DEPTH, D_MODEL, D_MODEL), D_MODEL),
        "w_ple": nrm(ks[13], (DEPTH, PLE_DIM, D_MODEL), PLE_DIM, BETA),
        "ln2_g": 1.0 + small(ks[14], (DEPTH, D_MODEL)),
        "ln2_b": small(ks[15], (DEPTH, D_MODEL)),
        "rel_bias": small(ks[16], (N_BUCKETS, N_BIAS_HEADS), 0.1),
    }


def reference(x, p, w_in, w_gate, w_br_a, w_br_b, w_br_c, w_out, ln1_g, ln1_b,
              w_up, w_down, w_ple_gate, w_ple, ln2_g, ln2_b, rel_bias):
    bsz, s_len, _ = x.shape
    bias_a = rel_bias[:, :A_HEADS]
    bias_b = rel_bias[:, A_HEADS:A_HEADS + B_GROUPS * B_SLOTS]
    bias_c = rel_bias[:, A_HEADS + B_GROUPS * B_SLOTS:]
    for i in range(DEPTH):
        proj = x @ w_in[i]
        aq, ak, av, iq, ik, iw, bq, bk, bv, cq, ck, cv = jnp.split(proj, SPLIT_POINTS, axis=-1)
        o_a = dsa_mixer(aq.reshape(bsz, s_len, A_HEADS, HEAD_DIM), ak, av,
                        iq.reshape(bsz, s_len, IDX_HEADS, IDX_DIM), ik, iw, bias_a)
        o_b = dilated_mixer(bq.reshape(bsz, s_len, B_GROUPS, B_SLOTS, HEAD_DIM),
                            bk.reshape(bsz, s_len, B_SLOTS, HEAD_DIM),
                            bv.reshape(bsz, s_len, B_SLOTS, HEAD_DIM), bias_b)
        o_c = moba_mixer(cq.reshape(bsz, s_len, C_HEADS, HEAD_DIM),
                         ck.reshape(bsz, s_len, C_HEADS, HEAD_DIM),
                         cv.reshape(bsz, s_len, C_HEADS, HEAD_DIM), bias_c)
        gates = jax.nn.sigmoid(x @ w_gate[i]).reshape(bsz, s_len, N_BRANCH, D_MODEL)
        merged = (gates[:, :, 0] * (o_a @ w_br_a[i])
                  + gates[:, :, 1] * (o_b @ w_br_b[i])
                  + gates[:, :, 2] * (o_c @ w_br_c[i]))
        x = layer_norm(ALPHA * x + merged @ w_out[i], ln1_g[i], ln1_b[i])
        h = jnp.square(jax.nn.relu(x @ w_up[i])) @ w_down[i]
        ple = jax.nn.sigmoid(x @ w_ple_gate[i]) * (p[i] @ w_ple[i])
        x = layer_norm(ALPHA * x + h + ple, ln2_g[i], ln2_b[i])
    return x
```

```python
import functools
import math

import numpy as np
import jax
import jax.numpy as jnp
from jax import lax
from jax.experimental import pallas as pl
from jax.experimental.pallas import tpu as pltpu

D_MODEL = 1024
HEAD_DIM = 64
A_HEADS = 6
IDX_HEADS = 8
A_TOPK_MAX = 256
B_SLOTS = 4
B_PATTERNS = ((128, 1), (512, 4), (2048, 16))
B_GROUPS = 3
C_HEADS = 6
C_BLOCK = 256
C_TOPK = 3
N_BUCKETS = 32
MAX_DISTANCE = 2048
D_FF = 4 * D_MODEL
PLE_DIM = 256
DEPTH = 2
ALPHA = (2 * DEPTH) ** 0.25
LN_EPS = 1e-5
NEG = -1e30
QK_SCALE = HEAD_DIM ** -0.5

LANES = 128
VMEM_LIMIT = 56 * 1024 * 1024

TQ = 128
BAND = 128
BF = jnp.bfloat16
F32 = jnp.float32
INT_MIN = -2 ** 31

_NT = (((1,), (1,)), ((), ()))


def _dot(a, b):
    return jnp.dot(a, b, preferred_element_type=F32)


def _dot_nt(a, b):
    return lax.dot_general(a, b, _NT, preferred_element_type=F32)


def _cparams(*sem):
    return pltpu.CompilerParams(dimension_semantics=sem, vmem_limit_bytes=VMEM_LIMIT)


def _bucket_starts():
    d = np.arange(0, MAX_DISTANCE + 1)
    max_exact = N_BUCKETS // 2
    nf = np.maximum(d, 1).astype(np.float32)
    large = max_exact + (np.log(nf / np.float32(max_exact)) / np.float32(math.log(MAX_DISTANCE / max_exact))
                         * np.float32(N_BUCKETS - max_exact)).astype(np.int32)
    bucket = np.where(d < max_exact, d, np.minimum(large, N_BUCKETS - 1))
    return [int(np.argmax(bucket >= b)) if np.any(bucket >= b) else None for b in range(N_BUCKETS)]


_BUCKET_START = _bucket_starts()


def _bias_from_distance(dist, tab_ref, col):
    val = jnp.full(dist.shape, tab_ref[0, col], F32)
    for b in range(1, N_BUCKETS):
        if _BUCKET_START[b] is not None:
            val = jnp.where(dist >= _BUCKET_START[b], tab_ref[b, col], val)
    return jnp.where(dist < 0, NEG, val)


def _causal_bias_kernel(tab_ref, o_ref, *, head0):
    i = pl.program_id(0)
    h = pl.program_id(1)
    tq, s_len = o_ref.shape[2], o_ref.shape[3]
    t = i * tq + lax.broadcasted_iota(jnp.int32, (tq, s_len), 0)
    s = lax.broadcasted_iota(jnp.int32, (tq, s_len), 1)
    o_ref[0, 0] = _bias_from_distance(t - s, tab_ref, head0 + h)


def _causal_bias_tiles(rel_bias, head0, n_heads, s_len):
    n_t = s_len // TQ
    return pl.pallas_call(
        functools.partial(_causal_bias_kernel, head0=head0),
        grid=(n_t, n_heads),
        in_specs=[pl.BlockSpec(memory_space=pltpu.SMEM)],
        out_specs=pl.BlockSpec((1, 1, TQ, s_len), lambda i, h: (i, h, 0, 0)),
        out_shape=jax.ShapeDtypeStruct((n_t, n_heads, TQ, s_len), F32),
        compiler_params=_cparams("arbitrary", "arbitrary"),
        name="causal_bias_tiles",
    )(rel_bias)


def _band_bias_kernel(tab_ref, o_ref, *, head0):
    g = pl.program_id(0)
    h = pl.program_id(1)
    row = lax.broadcasted_iota(jnp.int32, (BAND, 2 * BAND), 0)
    col = lax.broadcasted_iota(jnp.int32, (BAND, 2 * BAND), 1)
    j = row + BAND - col
    for gi, (_, dil) in enumerate(B_PATTERNS):
        @pl.when(g == gi)
        def _(dil=dil):
            bias = _bias_from_distance(j * dil, tab_ref, head0 + g * B_SLOTS + h)
            o_ref[0, 0] = jnp.where(j > BAND, NEG, bias)


def _band_bias_tiles(rel_bias, head0):
    return pl.pallas_call(
        functools.partial(_band_bias_kernel, head0=head0),
        grid=(B_GROUPS, B_SLOTS),
        in_specs=[pl.BlockSpec(memory_space=pltpu.SMEM)],
        out_specs=pl.BlockSpec((1, 1, BAND, 2 * BAND), lambda g, h: (g, h, 0, 0)),
        out_shape=jax.ShapeDtypeStruct((B_GROUPS, B_SLOTS, BAND, 2 * BAND), F32),
        compiler_params=_cparams("arbitrary", "arbitrary"),
        name="band_bias_tiles",
    )(rel_bias)


_PROJ_OUTS = (
    ("aq", A_HEADS * HEAD_DIM, QK_SCALE),
    ("akvi", 4 * HEAD_DIM, 1.0),
    ("iq", IDX_HEADS * HEAD_DIM, QK_SCALE),
    ("iw", LANES, 1.0),
    ("bq0", B_SLOTS * HEAD_DIM, QK_SCALE),
    ("bq1", B_SLOTS * HEAD_DIM, QK_SCALE),
    ("bq2", B_SLOTS * HEAD_DIM, QK_SCALE),
    ("bk", B_SLOTS * HEAD_DIM, 1.0),
    ("bv", B_SLOTS * HEAD_DIM, 1.0),
    ("cq", C_HEADS * HEAD_DIM, QK_SCALE),
    ("ck", C_HEADS * HEAD_DIM, 1.0),
    ("cv", C_HEADS * HEAD_DIM, 1.0),
)
_PROJ_WIDTH = sum(w for _, w, _ in _PROJ_OUTS)


def _pack_w_in(w):
    widths = (384, 64, 64, 512, 64, 8, 768, 256, 256, 384, 384, 384)
    offs = np.concatenate([[0], np.cumsum(widths)])
    aq, ak, av, iq, ik, iw, bq, bk, bv, cq, ck, cv = (w[:, offs[n]:offs[n + 1]] for n in range(12))
    zeros = lambda n: jnp.zeros((w.shape[0], n), w.dtype)
    cols = [aq, ak, av, ik, zeros(HEAD_DIM), iq, iw, zeros(LANES - IDX_HEADS), bq, bk, bv, cq, ck, cv]
    return jnp.concatenate(cols, axis=1).astype(BF)


def _proj_kernel(x_ref, w_ref, *o_refs):
    xb = x_ref[...].astype(BF)
    off = 0
    for o_ref, (_, width, scale) in zip(o_refs, _PROJ_OUTS):
        res = _dot(xb, w_ref[:, off:off + width])
        if scale != 1.0:
            res = res * scale
        o_ref[...] = res.astype(o_ref.dtype)
        off += width


def _project(x2d, w_packed, tm):
    n = x2d.shape[0]
    out_shape = [jax.ShapeDtypeStruct((n, w), F32 if name == "iw" else BF) for name, w, _ in _PROJ_OUTS]
    out_specs = [pl.BlockSpec((tm, w), lambda i: (i, 0)) for _, w, _ in _PROJ_OUTS]
    outs = pl.pallas_call(
        _proj_kernel,
        grid=(n // tm,),
        in_specs=[pl.BlockSpec((tm, D_MODEL), lambda i: (i, 0)),
                  pl.BlockSpec((D_MODEL, _PROJ_WIDTH), lambda i: (0, 0))],
        out_specs=out_specs,
        out_shape=out_shape,
        compiler_params=_cparams("parallel"),
        name="in_proj",
    )(x2d, w_packed)
    return {name: o for (name, _, _), o in zip(_PROJ_OUTS, outs)}


def _dsa_kernel(q_ref, iq_ref, iw_ref, kvi_ref, bias_ref, o_ref, key_ref, mask_ref, *, topk):
    i = pl.program_id(0)
    tq, s_len = key_ref.shape
    row = lax.broadcasted_iota(jnp.int32, (tq, s_len), 0)
    col = lax.broadcasted_iota(jnp.int32, (tq, s_len), 1)

    iq = iq_ref[0]
    iw = iw_ref[0]
    ik = kvi_ref[0, :, LANES:2 * LANES][:, :HEAD_DIM]
    index = jnp.zeros((tq, s_len), F32)
    for h in range(IDX_HEADS):
        sc = _dot_nt(iq[:, h * HEAD_DIM:(h + 1) * HEAD_DIM], ik)
        index = index + iw[:, h:h + 1] * jnp.maximum(sc, 0.0)
    index = jnp.where(col <= i * tq + row, index + 0.0, -jnp.inf)
    bits = pltpu.bitcast(index, jnp.int32)
    key_ref[...] = jnp.where(bits < 0, bits ^ jnp.int32(0x7FFFFFFF), bits)

    def count(pred):
        return jnp.sum(jnp.where(pred, 1.0, 0.0), axis=1, keepdims=True)

    k_f = float(topk)
    c0 = count(key_ref[...] >= 0)
    thr0 = jnp.where(c0 >= k_f, 0, INT_MIN).astype(jnp.int32)
    cnt0 = jnp.where(c0 >= k_f, c0, float(s_len))

    def thr_step(it, carry):
        thr, cnt = carry
        cand = thr | (jnp.int32(1) << (30 - it))
        c = count(key_ref[...] >= cand)
        ok = c >= k_f
        return jnp.where(ok, cand, thr), jnp.where(ok, c, cnt)

    thr, cnt_ge = lax.fori_loop(0, 31, thr_step, (thr0, cnt0))

    def tie_search():
        remaining = k_f - count(key_ref[...] > thr)

        def tie_step(it, last):
            cand = last | (jnp.int32(1) << (10 - it))
            c = count((key_ref[...] == thr) & (col < cand))
            return jnp.where(c < remaining, cand, last)

        return lax.fori_loop(0, 11, tie_step, jnp.zeros((tq, 1), jnp.int32))

    last_tie = lax.cond(jnp.max(cnt_ge) > k_f, tie_search,
                        lambda: jnp.full((tq, 1), s_len, jnp.int32))
    key = key_ref[...]
    selected = (key > thr) | ((key == thr) & (col <= last_tie))
    mask_ref[...] = jnp.where(selected, 0.0, NEG)

    q = q_ref[0]
    kv = kvi_ref[0, :, 0:LANES]
    k = kv[:, 0:HEAD_DIM]
    v = kv[:, HEAD_DIM:2 * HEAD_DIM]
    outs = []
    for h in range(A_HEADS):
        logits = _dot_nt(q[:, h * HEAD_DIM:(h + 1) * HEAD_DIM], k) + bias_ref[0, h] + mask_ref[...]
        m = jnp.max(logits, axis=1, keepdims=True)
        e = jnp.exp(logits - m)
        den = jnp.sum(e, axis=1, keepdims=True)
        outs.append(_dot(e.astype(BF), v) / den)
    o_ref[0] = jnp.concatenate(outs, axis=1).astype(o_ref.dtype)


def _dsa(aq, iq, iw, akvi, bias_tiles):
    bsz, s_len, _ = aq.shape
    n_t = s_len // TQ
    topk = min(A_TOPK_MAX, s_len // 4)
    return pl.pallas_call(
        functools.partial(_dsa_kernel, topk=topk),
        grid=(n_t, bsz),
        in_specs=[pl.BlockSpec((1, TQ, A_HEADS * HEAD_DIM), lambda i, b: (b, i, 0)),
                  pl.BlockSpec((1, TQ, IDX_HEADS * HEAD_DIM), lambda i, b: (b, i, 0)),
                  pl.BlockSpec((1, TQ, LANES), lambda i, b: (b, i, 0)),
                  pl.BlockSpec((1, s_len, 4 * HEAD_DIM), lambda i, b: (b, 0, 0)),
                  pl.BlockSpec((1, A_HEADS, TQ, s_len), lambda i, b: (i, 0, 0, 0))],
        out_specs=pl.BlockSpec((1, TQ, A_HEADS * HEAD_DIM), lambda i, b: (b, i, 0)),
        out_shape=jax.ShapeDtypeStruct((bsz, s_len, A_HEADS * HEAD_DIM), BF),
        scratch_shapes=[pltpu.VMEM((TQ, s_len), jnp.int32), pltpu.VMEM((TQ, s_len), F32)],
        compiler_params=_cparams("arbitrary", "arbitrary"),
        name="dsa_mixer",
    )(aq, iq, iw, akvi, bias_tiles)


_STATE_W = B_SLOTS * HEAD_DIM + 2 * LANES
_SLOT_LANES = LANES // B_SLOTS


def _band_tile(q, k_own, v_own, k_prev, v_prev, bias_ref, h, prev_bias):
    sl = slice(h * HEAD_DIM, (h + 1) * HEAD_DIM)
    l_own = _dot_nt(q[:, sl], k_own[:, sl]) + bias_ref[0, h, :, BAND:]
    m = jnp.max(l_own, axis=1, keepdims=True)
    if k_prev is not None:
        l_prev = _dot_nt(q[:, sl], k_prev[:, sl]) + bias_ref[0, h, :, :BAND] + prev_bias
        m = jnp.maximum(m, jnp.max(l_prev, axis=1, keepdims=True))
    e_own = jnp.exp(l_own - m)
    den = jnp.sum(e_own, axis=1, keepdims=True)
    acc = _dot(e_own.astype(BF), v_own[:, sl])
    if k_prev is not None:
        e_prev = jnp.exp(l_prev - m)
        den = den + jnp.sum(e_prev, axis=1, keepdims=True)
        acc = acc + _dot(e_prev.astype(BF), v_prev[:, sl])
    return acc, m, den


def _slot_stats(vals):
    lane = lax.broadcasted_iota(jnp.int32, (BAND, LANES), 1)
    out = jnp.broadcast_to(vals[B_SLOTS - 1], (BAND, LANES))
    for h in range(B_SLOTS - 2, -1, -1):
        out = jnp.where(lane < (h + 1) * _SLOT_LANES, vals[h], out)
    return out


def _band_state_kernel(q_ref, k_ref, v_ref, kp_ref, vp_ref, bias_ref, st_ref, *, n_cls):
    c = pl.program_id(1)
    n_tiles = q_ref.shape[1] // BAND
    w = B_SLOTS * HEAD_DIM
    first_prev_bias = jnp.where(c > 0, 0.0, NEG)
    for r in range(n_cls):
        cs = slice(r * w, (r + 1) * w)
        for t in range(n_tiles):
            rs = slice(t * BAND, (t + 1) * BAND)
            q, k_own, v_own = q_ref[0, rs, cs], k_ref[0, rs, cs], v_ref[0, rs, cs]
            if t == 0:
                k_prev, v_prev, prev_bias = kp_ref[0, :, cs], vp_ref[0, :, cs], first_prev_bias
            else:
                ps = slice((t - 1) * BAND, t * BAND)
                k_prev, v_prev, prev_bias = k_ref[0, ps, cs], v_ref[0, ps, cs], 0.0
            parts = [_band_tile(q, k_own, v_own, k_prev, v_prev, bias_ref, h, prev_bias) for h in range(B_SLOTS)]
            state = jnp.concatenate([p[0] for p in parts]
                                    + [_slot_stats([p[1] for p in parts]), _slot_stats([p[2] for p in parts])],
                                    axis=1)
            st_ref[0, rs, r * _STATE_W:(r + 1) * _STATE_W] = state


def _band_state(q, k, v, bias_tiles, g, rows, n_cls):
    bsz, length, width = q.shape
    n_chunks = length // rows
    prev_blocks = rows // BAND
    main = pl.BlockSpec((1, rows, width), lambda b, c: (b, c, 0))
    prev = pl.BlockSpec((1, BAND, width), lambda b, c: (b, jnp.maximum(c * prev_blocks - 1, 0), 0))
    return pl.pallas_call(
        functools.partial(_band_state_kernel, n_cls=n_cls),
        grid=(bsz, n_chunks),
        in_specs=[main, main, main, prev, prev,
                  pl.BlockSpec((1, B_SLOTS, BAND, 2 * BAND), lambda b, c: (g, 0, 0, 0))],
        out_specs=pl.BlockSpec((1, rows, n_cls * _STATE_W), lambda b, c: (b, c, 0)),
        out_shape=jax.ShapeDtypeStruct((bsz, length, n_cls * _STATE_W), F32),
        compiler_params=_cparams("parallel", "arbitrary"),
        name=f"band_state_{g}",
    )(q, k, v, k, v, bias_tiles)


def _band_merge_kernel(q_ref, k_ref, v_ref, st1_ref, st2_ref, bias_ref, o_ref, *, n_cls):
    w = B_SLOTS * HEAD_DIM
    for r in range(n_cls):
        cs = slice(r * w, (r + 1) * w)
        q, k_own, v_own = q_ref[0, :, cs], k_ref[0, :, cs], v_ref[0, :, cs]
        base = r * _STATE_W
        acc1, acc2 = st1_ref[0, :, base:base + w], st2_ref[0, :, base:base + w]
        max1, max2 = st1_ref[0, :, base + w:base + w + LANES], st2_ref[0, :, base + w:base + w + LANES]
        den1 = st1_ref[0, :, base + w + LANES:base + _STATE_W]
        den2 = st2_ref[0, :, base + w + LANES:base + _STATE_W]
        outs = []
        for h in range(B_SLOTS):
            acc3, m3, den3 = _band_tile(q, k_own, v_own, None, None, bias_ref, h, 0.0)
            hs = slice(h * HEAD_DIM, (h + 1) * HEAD_DIM)
            ss = slice(h * _SLOT_LANES, h * _SLOT_LANES + 1)
            m1, m2 = max1[:, ss], max2[:, ss]
            m_all = jnp.maximum(jnp.maximum(m1, m2), m3)
            w1, w2, w3 = jnp.exp(m1 - m_all), jnp.exp(m2 - m_all), jnp.exp(m3 - m_all)
            num = w1 * acc1[:, hs] + w2 * acc2[:, hs] + w3 * acc3
            den = w1 * den1[:, ss] + w2 * den2[:, ss] + w3 * den3
            outs.append(num / den)
        o_ref[0, :, cs] = jnp.concatenate(outs, axis=1).astype(o_ref.dtype)


def _band_merge(q, k, v, st1, st2, bias_tiles, g, n_cls):
    bsz, length, width = q.shape
    full = lambda wd: pl.BlockSpec((1, length, wd), lambda b: (b, 0, 0))
    return pl.pallas_call(
        functools.partial(_band_merge_kernel, n_cls=n_cls),
        grid=(bsz,),
        in_specs=[full(width), full(width), full(width), full(n_cls * _STATE_W), full(n_cls * _STATE_W),
                  pl.BlockSpec((1, B_SLOTS, BAND, 2 * BAND), lambda b: (g, 0, 0, 0))],
        out_specs=full(width),
        out_shape=jax.ShapeDtypeStruct((bsz, length, width), BF),
        compiler_params=_cparams("parallel"),
        name="band_merge",
    )(q, k, v, st1, st2, bias_tiles)


def _dilated(bq, bk, bv, bias_tiles):
    bsz, s_len, w = bk.shape
    view = lambda a, dil: a.reshape(bsz, s_len // dil, dil * a.shape[-1])
    (_, d0), (_, d1), (_, d2) = B_PATTERNS
    st0 = _band_state(view(bq[0], d0), view(bk, d0), view(bv, d0), bias_tiles, 0, rows=4 * BAND, n_cls=d0)
    st1 = _band_state(view(bq[1], d1), view(bk, d1), view(bv, d1), bias_tiles, 1, rows=BAND, n_cls=d1)
    st0 = view(st0.reshape(bsz, s_len, _STATE_W), d2)
    st1 = view(st1.reshape(bsz, s_len, _STATE_W), d2)
    out = _band_merge(view(bq[2], d2), view(bk, d2), view(bv, d2), st0, st1, bias_tiles, 2, n_cls=d2)
    return out.reshape(bsz, s_len, w)


def _moba_kernel(q_ref, k_ref, v_ref, bias_ref, expand_ref, o_ref, kmean_ref):
    i = pl.program_id(0)
    tq = q_ref.shape[1]
    s_len = k_ref.shape[1]
    n_blk = s_len // C_BLOCK
    cur = (i * tq) // C_BLOCK

    kmean_ref[...] = jnp.zeros_like(kmean_ref)
    for j in range(n_blk):
        blk = k_ref[0, j * C_BLOCK:(j + 1) * C_BLOCK, :].astype(F32)
        kmean_ref[j:j + 1, :] = jnp.sum(blk, axis=0, keepdims=True) * (1.0 / C_BLOCK)

    blk_id = lax.broadcasted_iota(jnp.int32, (tq, LANES), 1)
    q_all, k_all, v_all = q_ref[0], k_ref[0], v_ref[0]
    kmean = kmean_ref[...].astype(BF)
    outs = []
    for h in range(C_HEADS):
        hs = slice(h * HEAD_DIM, (h + 1) * HEAD_DIM)
        q = q_all[:, hs]
        gate = _dot_nt(q, kmean[:, hs])
        gate = jnp.where(blk_id < cur, gate, -jnp.inf)
        chosen = jnp.where(blk_id == cur, 1.0, 0.0)
        for j in range(n_blk):
            gj = gate[:, j:j + 1]
            beats = (gate > gj) | ((gate == gj) & (blk_id < j))
            rank = jnp.sum(jnp.where(beats, 1.0, 0.0), axis=1, keepdims=True)
            take = (blk_id == j) & (blk_id < cur) & (rank < float(C_TOPK))
            chosen = jnp.where(take, 1.0, chosen)
        key_on = _dot(chosen.astype(BF), expand_ref[...])
        logits = _dot_nt(q, k_all[:, hs]) + bias_ref[0, h]
        logits = jnp.where(key_on > 0.5, logits, NEG)
        m = jnp.max(logits, axis=1, keepdims=True)
        e = jnp.exp(logits - m)
        den = jnp.sum(e, axis=1, keepdims=True)
        outs.append(_dot(e.astype(BF), v_all[:, hs]) / den)
    o_ref[0] = jnp.concatenate(outs, axis=1).astype(o_ref.dtype)


def _moba(cq, ck, cv, bias_tiles):
    bsz, s_len, w = cq.shape
    n_t = s_len // TQ
    blk_of_key = np.arange(s_len) // C_BLOCK
    expand = jnp.asarray(blk_of_key[None, :] == np.arange(LANES)[:, None], BF)
    return pl.pallas_call(
        _moba_kernel,
        grid=(n_t, bsz),
        in_specs=[pl.BlockSpec((1, TQ, w), lambda i, b: (b, i, 0)),
                  pl.BlockSpec((1, s_len, w), lambda i, b: (b, 0, 0)),
                  pl.BlockSpec((1, s_len, w), lambda i, b: (b, 0, 0)),
                  pl.BlockSpec((1, C_HEADS, TQ, s_len), lambda i, b: (i, 0, 0, 0)),
                  pl.BlockSpec((LANES, s_len), lambda i, b: (0, 0))],
        out_specs=pl.BlockSpec((1, TQ, w), lambda i, b: (b, i, 0)),
        out_shape=jax.ShapeDtypeStruct((bsz, s_len, w), BF),
        scratch_shapes=[pltpu.VMEM((LANES, w), F32)],
        compiler_params=_cparams("arbitrary", "arbitrary"),
        name="moba_mixer",
    )(cq, ck, cv, bias_tiles, expand)


def _layer_norm(y, g_ref, b_ref):
    mu = jnp.mean(y, axis=-1, keepdims=True)
    yc = y - mu
    var = jnp.mean(yc * yc, axis=-1, keepdims=True)
    return yc * lax.rsqrt(var + LN_EPS) * g_ref[...] + b_ref[...]


def _merge_kernel(x_ref, oa_ref, ob_ref, oc_ref, wg_ref, wa_ref, wb_ref, wc_ref, wo_ref, g_ref, b_ref, y_ref):
    x = x_ref[...]
    xb = x.astype(BF)
    merged = None
    for n, (o_ref, w_ref) in enumerate(((oa_ref, wa_ref), (ob_ref, wb_ref), (oc_ref, wc_ref))):
        gate = jax.nn.sigmoid(_dot(xb, wg_ref[:, n * D_MODEL:(n + 1) * D_MODEL]))
        term = gate * _dot(o_ref[...], w_ref[...])
        merged = term if merged is None else merged + term
    y = ALPHA * x + _dot(merged.astype(BF), wo_ref[...])
    y_ref[...] = _layer_norm(y, g_ref, b_ref)


def _const_spec(shape):
    return pl.BlockSpec(shape, lambda i: (0,) * len(shape), pipeline_mode=pl.Buffered(1))


def _merge(x2d, oa, ob, oc, wg, wa, wb, wc, wo, ln_g, ln_b, tm):
    n = x2d.shape[0]
    rows = lambda w: pl.BlockSpec((tm, w), lambda i: (i, 0))
    return pl.pallas_call(
        _merge_kernel,
        grid=(n // tm,),
        in_specs=[rows(D_MODEL), rows(oa.shape[1]), rows(ob.shape[1]), rows(oc.shape[1]),
                  _const_spec(wg.shape), _const_spec(wa.shape), _const_spec(wb.shape), _const_spec(wc.shape),
                  _const_spec(wo.shape), _const_spec(ln_g.shape), _const_spec(ln_b.shape)],
        out_specs=rows(D_MODEL),
        out_shape=jax.ShapeDtypeStruct((n, D_MODEL), F32),
        compiler_params=_cparams("parallel"),
        name="merge_out_ln",
    )(x2d, oa, ob, oc, wg, wa, wb, wc, wo, ln_g, ln_b)


_FF_CHUNK = 1024


def _ffn_kernel(x_ref, p_ref, wu_ref, wd_ref, wpg_ref, wp_ref, g_ref, b_ref, y_ref):
    x = x_ref[...]
    xb = x.astype(BF)
    y = ALPHA * x + jax.nn.sigmoid(_dot(xb, wpg_ref[...])) * _dot(p_ref[...].astype(BF), wp_ref[...])
    for c in range(D_FF // _FF_CHUNK):
        cs = slice(c * _FF_CHUNK, (c + 1) * _FF_CHUNK)
        u = jnp.maximum(_dot(xb, wu_ref[:, cs]), 0.0)
        y = y + _dot((u * u).astype(BF), wd_ref[cs, :])
    y_ref[...] = _layer_norm(y, g_ref, b_ref)


def _ffn(x2d, p2d, wu, wd, wpg, wp, ln_g, ln_b, tm):
    n = x2d.shape[0]
    rows = lambda w: pl.BlockSpec((tm, w), lambda i: (i, 0))
    return pl.pallas_call(
        _ffn_kernel,
        grid=(n // tm,),
        in_specs=[rows(D_MODEL), rows(PLE_DIM), _const_spec(wu.shape), _const_spec(wd.shape),
                  _const_spec(wpg.shape), _const_spec(wp.shape), _const_spec(ln_g.shape), _const_spec(ln_b.shape)],
        out_specs=rows(D_MODEL),
        out_shape=jax.ShapeDtypeStruct((n, D_MODEL), F32),
        compiler_params=_cparams("parallel"),
        name="ffn_ple_ln",
    )(x2d, p2d, wu, wd, wpg, wp, ln_g, ln_b)


def _token_tile(n):
    for tm in (512, 256, 128):
        if n % tm == 0:
            return tm
    raise ValueError(f"token count {n} is not a multiple of 128")


def kernel(x, p, w_in, w_gate, w_br_a, w_br_b, w_br_c, w_out, ln1_g, ln1_b,
           w_up, w_down, w_ple_gate, w_ple, ln2_g, ln2_b, rel_bias):
    bsz, s_len, d_model = x.shape
    assert d_model == D_MODEL and s_len == MAX_DISTANCE, (x.shape,)
    n_tok = bsz * s_len
    tm = _token_tile(n_tok)

    b_head0 = A_HEADS
    c_head0 = A_HEADS + B_GROUPS * B_SLOTS
    bias_a = _causal_bias_tiles(rel_bias, 0, A_HEADS, s_len)
    bias_b = _band_bias_tiles(rel_bias, b_head0)
    bias_c = _causal_bias_tiles(rel_bias, c_head0, C_HEADS, s_len)

    x2d = x.reshape(n_tok, D_MODEL)
    for i in range(DEPTH):
        pr = _project(x2d, _pack_w_in(w_in[i]), tm)
        seq = lambda name: pr[name].reshape(bsz, s_len, -1)
        o_a = _dsa(seq("aq"), seq("iq"), seq("iw"), seq("akvi"), bias_a)
        o_b = _dilated((seq("bq0"), seq("bq1"), seq("bq2")), seq("bk"), seq("bv"), bias_b)
        o_c = _moba(seq("cq"), seq("ck"), seq("cv"), bias_c)
        flat = lambda a: a.reshape(n_tok, -1)
        row = lambda a: a.reshape(1, D_MODEL)
        x2d = _merge(x2d, flat(o_a), flat(o_b), flat(o_c), w_gate[i].astype(BF), w_br_a[i].astype(BF),
                     w_br_b[i].astype(BF), w_br_c[i].astype(BF), w_out[i].astype(BF),
                     row(ln1_g[i]), row(ln1_b[i]), tm)
        x2d = _ffn(x2d, p[i].reshape(n_tok, PLE_DIM), w_up[i].astype(BF), w_down[i].astype(BF),
                   w_ple_gate[i].astype(BF), w_ple[i].astype(BF), row(ln2_g[i]), row(ln2_b[i]), tm)
    return x2d.reshape(bsz, s_len, D_MODEL)
```

```python
import functools
import math

import numpy as np
import jax
import jax.numpy as jnp
from jax import lax
from jax.experimental import pallas as pl
from jax.experimental.pallas import tpu as pltpu

D_MODEL = 1024
HEAD_DIM = 64
A_HEADS = 6
IDX_HEADS = 8
A_TOPK_MAX = 256
B_SLOTS = 4
B_PATTERNS = ((128, 1), (512, 4), (2048, 16))
B_GROUPS = 3
C_HEADS = 6
C_BLOCK = 256
C_TOPK = 3
N_BUCKETS = 32
MAX_DISTANCE = 2048
D_FF = 4 * D_MODEL
PLE_DIM = 256
DEPTH = 2
ALPHA = (2 * DEPTH) ** 0.25
LN_EPS = 1e-5
NEG = -1e30
QK_SCALE = HEAD_DIM ** -0.5

LANES = 128
BF16_ROWS = 16
VMEM_LIMIT = 56 * 1024 * 1024

TQ = 128
TILES_PER_GROUP = C_BLOCK // TQ
BAND = 128
BF = jnp.bfloat16
F32 = jnp.float32
INT_MIN = -2 ** 31

_NT = (((1,), (1,)), ((), ()))


def _dot(a, b):
    return jnp.dot(a, b, preferred_element_type=F32)


def _dot_nt(a, b):
    return lax.dot_general(a, b, _NT, preferred_element_type=F32)


def _cparams(*sem):
    return pltpu.CompilerParams(dimension_semantics=sem, vmem_limit_bytes=VMEM_LIMIT)


def _bucket_starts():
    d = np.arange(0, MAX_DISTANCE + 1)
    max_exact = N_BUCKETS // 2
    nf = np.maximum(d, 1).astype(np.float32)
    large = max_exact + (np.log(nf / np.float32(max_exact)) / np.float32(math.log(MAX_DISTANCE / max_exact))
                         * np.float32(N_BUCKETS - max_exact)).astype(np.int32)
    bucket = np.where(d < max_exact, d, np.minimum(large, N_BUCKETS - 1))
    return [int(np.argmax(bucket >= b)) if np.any(bucket >= b) else None for b in range(N_BUCKETS)]


_BUCKET_START = _bucket_starts()


def _bias_from_distance(dist, tab_ref, col):
    val = jnp.full(dist.shape, tab_ref[0, col], F32)
    for b in range(1, N_BUCKETS):
        if _BUCKET_START[b] is not None:
            val = jnp.where(dist >= _BUCKET_START[b], tab_ref[b, col], val)
    return jnp.where(dist < 0, NEG, val)


def _causal_bias_kernel(tab_ref, o_ref, *, head0):
    i = pl.program_id(0)
    h = pl.program_id(1)
    s_len, tq = o_ref.shape[2], o_ref.shape[3]
    s = lax.broadcasted_iota(jnp.int32, (s_len, tq), 0)
    t = i * tq + lax.broadcasted_iota(jnp.int32, (s_len, tq), 1)
    o_ref[0, 0] = _bias_from_distance(t - s, tab_ref, head0 + h)


def _causal_bias_tiles(rel_bias, head0, n_heads, s_len):
    n_t = s_len // TQ
    return pl.pallas_call(
        functools.partial(_causal_bias_kernel, head0=head0),
        grid=(n_t, n_heads),
        in_specs=[pl.BlockSpec(memory_space=pltpu.SMEM)],
        out_specs=pl.BlockSpec((1, 1, s_len, TQ), lambda i, h: (i, h, 0, 0)),
        out_shape=jax.ShapeDtypeStruct((n_t, n_heads, s_len, TQ), F32),
        compiler_params=_cparams("arbitrary", "arbitrary"),
        name="causal_bias_tiles",
    )(rel_bias)


def _band_bias_kernel(tab_ref, o_ref, *, head0):
    g = pl.program_id(0)
    h = pl.program_id(1)
    row = lax.broadcasted_iota(jnp.int32, (BAND, 2 * BAND), 0)
    col = lax.broadcasted_iota(jnp.int32, (BAND, 2 * BAND), 1)
    j = row + BAND - col
    for gi, (_, dil) in enumerate(B_PATTERNS):
        @pl.when(g == gi)
        def _(dil=dil):
            bias = _bias_from_distance(j * dil, tab_ref, head0 + g * B_SLOTS + h)
            o_ref[0, 0] = jnp.where(j > BAND, NEG, bias)


def _band_bias_tiles(rel_bias, head0):
    return pl.pallas_call(
        functools.partial(_band_bias_kernel, head0=head0),
        grid=(B_GROUPS, B_SLOTS),
        in_specs=[pl.BlockSpec(memory_space=pltpu.SMEM)],
        out_specs=pl.BlockSpec((1, 1, BAND, 2 * BAND), lambda g, h: (g, h, 0, 0)),
        out_shape=jax.ShapeDtypeStruct((B_GROUPS, B_SLOTS, BAND, 2 * BAND), F32),
        compiler_params=_cparams("arbitrary", "arbitrary"),
        name="band_bias_tiles",
    )(rel_bias)


_PROJ_OUTS = (
    ("aq", A_HEADS * HEAD_DIM, QK_SCALE),
    ("akk", 2 * HEAD_DIM, 1.0),
    ("aii", 2 * HEAD_DIM, 1.0),
    ("iq", IDX_HEADS * HEAD_DIM, QK_SCALE),
    ("bq0", B_SLOTS * HEAD_DIM, QK_SCALE),
    ("bq1", B_SLOTS * HEAD_DIM, QK_SCALE),
    ("bq2", B_SLOTS * HEAD_DIM, QK_SCALE),
    ("bk", B_SLOTS * HEAD_DIM, 1.0),
    ("bv", B_SLOTS * HEAD_DIM, 1.0),
    ("cq", C_HEADS * HEAD_DIM, QK_SCALE),
    ("ck", C_HEADS * HEAD_DIM, 1.0),
)
_PROJ_WIDTH = sum(w for _, w, _ in _PROJ_OUTS)
_PROJ_OUTS_T = (
    ("avT", HEAD_DIM, BF),
    ("cvT", C_HEADS * HEAD_DIM, BF),
    ("iwT", BF16_ROWS, F32),
)
_PROJ_ROWS_T = sum(r for _, r, _ in _PROJ_OUTS_T)


def _pack_w_in(w):
    widths = (384, 64, 64, 512, 64, 8, 768, 256, 256, 384, 384, 384)
    offs = np.concatenate([[0], np.cumsum(widths)])
    aq, ak, av, iq, ik, iw, bq, bk, bv, cq, ck, cv = (w[:, offs[n]:offs[n + 1]] for n in range(12))
    cols = jnp.concatenate([aq, ak, ak, ik, ik, iq, bq, bk, bv, cq, ck], axis=1)
    iw_pad = jnp.concatenate([iw, jnp.zeros((w.shape[0], BF16_ROWS - IDX_HEADS), w.dtype)], axis=1)
    rows = jnp.concatenate([av, cv, iw_pad], axis=1).T
    return cols.astype(BF), rows.astype(BF)


def _proj_kernel(x_ref, w_ref, wt_ref, *o_refs):
    xb = x_ref[...].astype(BF)
    off = 0
    for o_ref, (_, width, scale) in zip(o_refs, _PROJ_OUTS):
        res = _dot(xb, w_ref[:, off:off + width])
        if scale != 1.0:
            res = res * scale
        o_ref[...] = res.astype(o_ref.dtype)
        off += width
    off = 0
    for o_ref, (_, rows, _) in zip(o_refs[len(_PROJ_OUTS):], _PROJ_OUTS_T):
        o_ref[0] = _dot_nt(wt_ref[off:off + rows, :], xb).astype(o_ref.dtype)
        off += rows


def _project(x2d, w_cols, w_rows, tm, s_len):
    n = x2d.shape[0]
    tiles_per_seq = s_len // tm
    out_shape = [jax.ShapeDtypeStruct((n, w), BF) for _, w, _ in _PROJ_OUTS]
    out_specs = [pl.BlockSpec((tm, w), lambda i: (i, 0)) for _, w, _ in _PROJ_OUTS]
    out_shape += [jax.ShapeDtypeStruct((n // s_len, r, s_len), dt) for _, r, dt in _PROJ_OUTS_T]
    out_specs += [pl.BlockSpec((1, r, tm), lambda i: (i // tiles_per_seq, 0, i % tiles_per_seq))
                  for _, r, _ in _PROJ_OUTS_T]
    outs = pl.pallas_call(
        _proj_kernel,
        grid=(n // tm,),
        in_specs=[pl.BlockSpec((tm, D_MODEL), lambda i: (i, 0)),
                  pl.BlockSpec((D_MODEL, _PROJ_WIDTH), lambda i: (0, 0)),
                  pl.BlockSpec((_PROJ_ROWS_T, D_MODEL), lambda i: (0, 0))],
        out_specs=out_specs,
        out_shape=out_shape,
        compiler_params=_cparams("parallel"),
        name="in_proj",
    )(x2d, w_cols, w_rows)
    names = [name for name, _, _ in _PROJ_OUTS] + [name for name, _, _ in _PROJ_OUTS_T]
    return dict(zip(names, outs))


def _head_pair_rhs(pair):
    lane = lax.broadcasted_iota(jnp.int32, pair.shape, 1)
    zero = jnp.zeros_like(pair)
    return jnp.concatenate([jnp.where(lane < HEAD_DIM, pair, zero), jnp.where(lane < HEAD_DIM, zero, pair)], axis=0)


def _softmax_pv(logits, v_t_ones):
    m = jnp.max(logits, axis=0, keepdims=True)
    e = jnp.exp(logits - m).astype(BF)
    o = _dot(v_t_ones, e)
    return o[:HEAD_DIM] / o[HEAD_DIM:HEAD_DIM + 1]


def _with_ones_rows(v_t):
    return jnp.concatenate([v_t, jnp.ones((BF16_ROWS, v_t.shape[1]), v_t.dtype)], axis=0)


def _grouped_tiles(call_group, s_len):
    n_groups = s_len // (TILES_PER_GROUP * TQ)
    return jnp.concatenate([call_group(g, (g + 1) * TILES_PER_GROUP * TQ) for g in range(n_groups)], axis=1)


def _dsa_kernel(q_ref, iq_ref, iw_ref, kk_ref, ii_ref, vt_ref, bias_ref, o_ref, key_ref, mask_ref, *, topk, tile0):
    i = tile0 + pl.program_id(0)
    n_keys, tq = key_ref.shape
    pos = lax.broadcasted_iota(jnp.int32, (n_keys, tq), 0)
    qry = i * tq + lax.broadcasted_iota(jnp.int32, (n_keys, tq), 1)

    if n_keys > topk:
        iq = iq_ref[0]
        iw = iw_ref[0]
        ii = ii_ref[0]
        index = jnp.zeros((n_keys, tq), F32)
        for p in range(IDX_HEADS // 2):
            sc = _dot_nt(ii, _head_pair_rhs(iq[:, p * LANES:(p + 1) * LANES]))
            index = index + iw[2 * p:2 * p + 1, :] * jnp.maximum(sc[:, :tq], 0.0)
            index = index + iw[2 * p + 1:2 * p + 2, :] * jnp.maximum(sc[:, tq:], 0.0)
        index = jnp.where(pos <= qry, index + 0.0, -jnp.inf)
        bits = pltpu.bitcast(index, jnp.int32)
        key_ref[...] = jnp.where(bits < 0, bits ^ jnp.int32(0x7FFFFFFF), bits)

        def count(pred):
            return jnp.sum(jnp.where(pred, 1.0, 0.0), axis=0, keepdims=True)

        k_f = float(topk)
        c0 = count(key_ref[...] >= 0)
        thr0 = jnp.where(c0 >= k_f, 0, INT_MIN).astype(jnp.int32)
        cnt0 = jnp.where(c0 >= k_f, c0, float(n_keys))

        def thr_step(it, carry):
            thr, cnt = carry
            cand = thr | (jnp.int32(1) << (30 - it))
            c = count(key_ref[...] >= cand)
            ok = c >= k_f
            return jnp.where(ok, cand, thr), jnp.where(ok, c, cnt)

        thr, cnt_ge = lax.fori_loop(0, 31, thr_step, (thr0, cnt0))

        def tie_search():
            remaining = k_f - count(key_ref[...] > thr)
            n_bits = (n_keys - 1).bit_length()

            def tie_step(it, last):
                cand = last | (jnp.int32(1) << (n_bits - 1 - it))
                c = count((key_ref[...] == thr) & (pos < cand))
                return jnp.where(c < remaining, cand, last)

            return lax.fori_loop(0, n_bits, tie_step, jnp.zeros((1, tq), jnp.int32))

        last_tie = lax.cond(jnp.max(cnt_ge) > k_f, tie_search,
                            lambda: jnp.full((1, tq), n_keys, jnp.int32))
        key = key_ref[...]
        selected = (key > thr) | ((key == thr) & (pos <= last_tie))
        mask_ref[...] = jnp.where(selected, 0.0, NEG)
    else:
        mask_ref[...] = jnp.zeros_like(mask_ref)

    q = q_ref[0]
    kk = kk_ref[0]
    v_t = _with_ones_rows(vt_ref[0])
    outs = []
    for p in range(A_HEADS // 2):
        logits = _dot_nt(kk, _head_pair_rhs(q[:, p * LANES:(p + 1) * LANES]))
        for half in range(2):
            lg = logits[:, half * tq:(half + 1) * tq] + bias_ref[0, 2 * p + half] + mask_ref[...]
            outs.append(_softmax_pv(lg, v_t))
    o_ref[0] = jnp.concatenate(outs, axis=0).T.astype(o_ref.dtype)


def _dsa(aq, iq, iw_t, akk, aii, av_t, bias_tiles):
    bsz, s_len, _ = aq.shape
    topk = min(A_TOPK_MAX, s_len // 4)
    rows = TILES_PER_GROUP * TQ

    def call_group(g, n_keys):
        tile0 = g * TILES_PER_GROUP
        q_spec = lambda w: pl.BlockSpec((1, TQ, w), lambda i, b: (b, tile0 + i, 0))
        keys = lambda w: pl.BlockSpec((1, n_keys, w), lambda i, b: (b, 0, 0))
        return pl.pallas_call(
            functools.partial(_dsa_kernel, topk=topk, tile0=tile0),
            grid=(TILES_PER_GROUP, bsz),
            in_specs=[q_spec(A_HEADS * HEAD_DIM), q_spec(IDX_HEADS * HEAD_DIM),
                      pl.BlockSpec((1, BF16_ROWS, TQ), lambda i, b: (b, 0, tile0 + i)),
                      keys(2 * HEAD_DIM), keys(2 * HEAD_DIM),
                      pl.BlockSpec((1, HEAD_DIM, n_keys), lambda i, b: (b, 0, 0)),
                      pl.BlockSpec((1, A_HEADS, n_keys, TQ), lambda i, b: (tile0 + i, 0, 0, 0))],
            out_specs=pl.BlockSpec((1, TQ, A_HEADS * HEAD_DIM), lambda i, b: (b, i, 0)),
            out_shape=jax.ShapeDtypeStruct((bsz, rows, A_HEADS * HEAD_DIM), BF),
            scratch_shapes=[pltpu.VMEM((n_keys, TQ), jnp.int32), pltpu.VMEM((n_keys, TQ), F32)],
            compiler_params=_cparams("arbitrary", "arbitrary"),
            name=f"dsa_mixer_{g}",
        )(aq, iq, iw_t, akk, aii, av_t, bias_tiles)

    return _grouped_tiles(call_group, s_len)


_STATE_W = B_SLOTS * HEAD_DIM + 2 * LANES
_SLOT_LANES = LANES // B_SLOTS


def _band_tile(q, k_own, v_own, k_prev, v_prev, bias_ref, h, prev_bias):
    sl = slice(h * HEAD_DIM, (h + 1) * HEAD_DIM)
    l_own = _dot_nt(q[:, sl], k_own[:, sl]) + bias_ref[0, h, :, BAND:]
    m = jnp.max(l_own, axis=1, keepdims=True)
    if k_prev is not None:
        l_prev = _dot_nt(q[:, sl], k_prev[:, sl]) + bias_ref[0, h, :, :BAND] + prev_bias
        m = jnp.maximum(m, jnp.max(l_prev, axis=1, keepdims=True))
    e_own = jnp.exp(l_own - m)
    den = jnp.sum(e_own, axis=1, keepdims=True)
    acc = _dot(e_own.astype(BF), v_own[:, sl])
    if k_prev is not None:
        e_prev = jnp.exp(l_prev - m)
        den = den + jnp.sum(e_prev, axis=1, keepdims=True)
        acc = acc + _dot(e_prev.astype(BF), v_prev[:, sl])
    return acc, m, den


def _slot_stats(vals):
    lane = lax.broadcasted_iota(jnp.int32, (BAND, LANES), 1)
    out = jnp.broadcast_to(vals[B_SLOTS - 1], (BAND, LANES))
    for h in range(B_SLOTS - 2, -1, -1):
        out = jnp.where(lane < (h + 1) * _SLOT_LANES, vals[h], out)
    return out


def _band_state_kernel(q_ref, k_ref, v_ref, kp_ref, vp_ref, bias_ref, st_ref, *, n_cls):
    c = pl.program_id(1)
    n_tiles = q_ref.shape[1] // BAND
    w = B_SLOTS * HEAD_DIM
    first_prev_bias = jnp.where(c > 0, 0.0, NEG)
    for r in range(n_cls):
        cs = slice(r * w, (r + 1) * w)
        for t in range(n_tiles):
            rs = slice(t * BAND, (t + 1) * BAND)
            q, k_own, v_own = q_ref[0, rs, cs], k_ref[0, rs, cs], v_ref[0, rs, cs]
            if t == 0:
                k_prev, v_prev, prev_bias = kp_ref[0, :, cs], vp_ref[0, :, cs], first_prev_bias
            else:
                ps = slice((t - 1) * BAND, t * BAND)
                k_prev, v_prev, prev_bias = k_ref[0, ps, cs], v_ref[0, ps, cs], 0.0
            parts = [_band_tile(q, k_own, v_own, k_prev, v_prev, bias_ref, h, prev_bias) for h in range(B_SLOTS)]
            state = jnp.concatenate([p[0] for p in parts]
                                    + [_slot_stats([p[1] for p in parts]), _slot_stats([p[2] for p in parts])],
                                    axis=1)
            st_ref[0, rs, r * _STATE_W:(r + 1) * _STATE_W] = state


def _band_state(q, k, v, bias_tiles, g, rows, n_cls):
    bsz, length, width = q.shape
    n_chunks = length // rows
    prev_blocks = rows // BAND
    main = pl.BlockSpec((1, rows, width), lambda b, c: (b, c, 0))
    prev = pl.BlockSpec((1, BAND, width), lambda b, c: (b, jnp.maximum(c * prev_blocks - 1, 0), 0))
    return pl.pallas_call(
        functools.partial(_band_state_kernel, n_cls=n_cls),
        grid=(bsz, n_chunks),
        in_specs=[main, main, main, prev, prev,
                  pl.BlockSpec((1, B_SLOTS, BAND, 2 * BAND), lambda b, c: (g, 0, 0, 0))],
        out_specs=pl.BlockSpec((1, rows, n_cls * _STATE_W), lambda b, c: (b, c, 0)),
        out_shape=jax.ShapeDtypeStruct((bsz, length, n_cls * _STATE_W), F32),
        compiler_params=_cparams("parallel", "arbitrary"),
        name=f"band_state_{g}",
    )(q, k, v, k, v, bias_tiles)


def _band_merge_kernel(q_ref, k_ref, v_ref, st1_ref, st2_ref, bias_ref, o_ref, *, n_cls):
    w = B_SLOTS * HEAD_DIM
    for r in range(n_cls):
        cs = slice(r * w, (r + 1) * w)
        q, k_own, v_own = q_ref[0, :, cs], k_ref[0, :, cs], v_ref[0, :, cs]
        base = r * _STATE_W
        acc1, acc2 = st1_ref[0, :, base:base + w], st2_ref[0, :, base:base + w]
        max1, max2 = st1_ref[0, :, base + w:base + w + LANES], st2_ref[0, :, base + w:base + w + LANES]
        den1 = st1_ref[0, :, base + w + LANES:base + _STATE_W]
        den2 = st2_ref[0, :, base + w + LANES:base + _STATE_W]
        outs = []
        for h in range(B_SLOTS):
            acc3, m3, den3 = _band_tile(q, k_own, v_own, None, None, bias_ref, h, 0.0)
            hs = slice(h * HEAD_DIM, (h + 1) * HEAD_DIM)
            ss = slice(h * _SLOT_LANES, h * _SLOT_LANES + 1)
            m1, m2 = max1[:, ss], max2[:, ss]
            m_all = jnp.maximum(jnp.maximum(m1, m2), m3)
            w1, w2, w3 = jnp.exp(m1 - m_all), jnp.exp(m2 - m_all), jnp.exp(m3 - m_all)
            num = w1 * acc1[:, hs] + w2 * acc2[:, hs] + w3 * acc3
            den = w1 * den1[:, ss] + w2 * den2[:, ss] + w3 * den3
            outs.append(num / den)
        o_ref[0, :, cs] = jnp.concatenate(outs, axis=1).astype(o_ref.dtype)


def _band_merge(q, k, v, st1, st2, bias_tiles, g, n_cls):
    bsz, length, width = q.shape
    full = lambda wd: pl.BlockSpec((1, length, wd), lambda b: (b, 0, 0))
    return pl.pallas_call(
        functools.partial(_band_merge_kernel, n_cls=n_cls),
        grid=(bsz,),
        in_specs=[full(width), full(width), full(width), full(n_cls * _STATE_W), full(n_cls * _STATE_W),
                  pl.BlockSpec((1, B_SLOTS, BAND, 2 * BAND), lambda b: (g, 0, 0, 0))],
        out_specs=full(width),
        out_shape=jax.ShapeDtypeStruct((bsz, length, width), BF),
        compiler_params=_cparams("parallel"),
        name="band_merge",
    )(q, k, v, st1, st2, bias_tiles)


def _dilated(bq, bk, bv, bias_tiles):
    bsz, s_len, w = bk.shape
    view = lambda a, dil: a.reshape(bsz, s_len // dil, dil * a.shape[-1])
    (_, d0), (_, d1), (_, d2) = B_PATTERNS
    st0 = _band_state(view(bq[0], d0), view(bk, d0), view(bv, d0), bias_tiles, 0, rows=4 * BAND, n_cls=d0)
    st1 = _band_state(view(bq[1], d1), view(bk, d1), view(bv, d1), bias_tiles, 1, rows=BAND, n_cls=d1)
    st0 = view(st0.reshape(bsz, s_len, _STATE_W), d2)
    st1 = view(st1.reshape(bsz, s_len, _STATE_W), d2)
    out = _band_merge(view(bq[2], d2), view(bk, d2), view(bv, d2), st0, st1, bias_tiles, 2, n_cls=d2)
    return out.reshape(bsz, s_len, w)


def _block_mean_kernel(k_ref, o_ref):
    n_blk = k_ref.shape[1] // C_BLOCK
    o_ref[...] = jnp.zeros_like(o_ref)
    for j in range(n_blk):
        blk = k_ref[0, j * C_BLOCK:(j + 1) * C_BLOCK, :].astype(F32)
        o_ref[0, j:j + 1, :] = (jnp.sum(blk, axis=0, keepdims=True) * (1.0 / C_BLOCK)).astype(o_ref.dtype)


def _block_means(ck):
    bsz, s_len, w = ck.shape
    assert s_len // C_BLOCK <= BF16_ROWS
    return pl.pallas_call(
        _block_mean_kernel,
        grid=(bsz,),
        in_specs=[pl.BlockSpec((1, s_len, w), lambda b: (b, 0, 0))],
        out_specs=pl.BlockSpec((1, BF16_ROWS, w), lambda b: (b, 0, 0)),
        out_shape=jax.ShapeDtypeStruct((bsz, BF16_ROWS, w), BF),
        compiler_params=_cparams("parallel"),
        name="moba_block_means",
    )(ck)


def _moba_kernel(q_ref, k_ref, vt_ref, kmean_ref, bias_ref, o_ref, *, cur):
    tq = q_ref.shape[1]
    q = q_ref[0]
    kmean = kmean_ref[0]
    blk = lax.broadcasted_iota(jnp.int32, (BF16_ROWS, tq), 0)
    outs = []
    for p in range(C_HEADS // 2):
        ps = slice(p * LANES, (p + 1) * LANES)
        rhs = _head_pair_rhs(q[:, ps])
        logits = _dot_nt(k_ref[0, :, ps], rhs)
        gates = _dot_nt(kmean[:, ps], rhs)
        for half in range(2):
            h = 2 * p + half
            hq = slice(half * tq, (half + 1) * tq)
            if cur > C_TOPK:
                gate = jnp.where(blk < cur, gates[:, hq], -jnp.inf)
                off = jnp.where(blk == cur, 0.0, NEG)
                for j in range(cur):
                    gj = gate[j:j + 1, :]
                    beats = (gate > gj) | ((gate == gj) & (blk < j))
                    rank = jnp.sum(jnp.where(beats, 1.0, 0.0), axis=0, keepdims=True)
                    off = jnp.where((blk == j) & (rank < float(C_TOPK)), 0.0, off)
            pieces = []
            for j in range(cur + 1):
                ks = slice(j * C_BLOCK, (j + 1) * C_BLOCK)
                piece = logits[ks, hq] + bias_ref[0, h, ks, :]
                if cur > C_TOPK:
                    piece = piece + off[j:j + 1, :]
                pieces.append(piece)
            lg = jnp.concatenate(pieces, axis=0)
            outs.append(_softmax_pv(lg, _with_ones_rows(vt_ref[0, h * HEAD_DIM:(h + 1) * HEAD_DIM, :])))
    o_ref[0] = jnp.concatenate(outs, axis=0).T.astype(o_ref.dtype)


def _moba(cq, ck, cv_t, bias_tiles):
    bsz, s_len, w = cq.shape
    assert TILES_PER_GROUP * TQ == C_BLOCK
    kmean = _block_means(ck)

    def call_group(g, n_keys):
        tile0 = g * TILES_PER_GROUP
        return pl.pallas_call(
            functools.partial(_moba_kernel, cur=g),
            grid=(TILES_PER_GROUP, bsz),
            in_specs=[pl.BlockSpec((1, TQ, w), lambda i, b: (b, tile0 + i, 0)),
                      pl.BlockSpec((1, n_keys, w), lambda i, b: (b, 0, 0)),
                      pl.BlockSpec((1, w, n_keys), lambda i, b: (b, 0, 0)),
                      pl.BlockSpec((1, BF16_ROWS, w), lambda i, b: (b, 0, 0)),
                      pl.BlockSpec((1, C_HEADS, n_keys, TQ), lambda i, b: (tile0 + i, 0, 0, 0))],
            out_specs=pl.BlockSpec((1, TQ, w), lambda i, b: (b, i, 0)),
            out_shape=jax.ShapeDtypeStruct((bsz, C_BLOCK, w), BF),
            compiler_params=_cparams("arbitrary", "arbitrary"),
            name=f"moba_mixer_{g}",
        )(cq, ck, cv_t, kmean, bias_tiles)

    return _grouped_tiles(call_group, s_len)


def _layer_norm(y, g_ref, b_ref):
    mu = jnp.mean(y, axis=-1, keepdims=True)
    yc = y - mu
    var = jnp.mean(yc * yc, axis=-1, keepdims=True)
    return yc * lax.rsqrt(var + LN_EPS) * g_ref[...] + b_ref[...]


def _merge_kernel(x_ref, oa_ref, ob_ref, oc_ref, wg_ref, wa_ref, wb_ref, wc_ref, wo_ref, g_ref, b_ref, y_ref):
    x = x_ref[...]
    xb = x.astype(BF)
    merged = None
    for n, (o_ref, w_ref) in enumerate(((oa_ref, wa_ref), (ob_ref, wb_ref), (oc_ref, wc_ref))):
        gate = jax.nn.sigmoid(_dot(xb, wg_ref[:, n * D_MODEL:(n + 1) * D_MODEL]))
        term = gate * _dot(o_ref[...], w_ref[...])
        merged = term if merged is None else merged + term
    y = ALPHA * x + _dot(merged.astype(BF), wo_ref[...])
    y_ref[...] = _layer_norm(y, g_ref, b_ref)


def _const_spec(shape):
    return pl.BlockSpec(shape, lambda i: (0,) * len(shape), pipeline_mode=pl.Buffered(1))


def _merge(x2d, oa, ob, oc, wg, wa, wb, wc, wo, ln_g, ln_b, tm):
    n = x2d.shape[0]
    rows = lambda w: pl.BlockSpec((tm, w), lambda i: (i, 0))
    return pl.pallas_call(
        _merge_kernel,
        grid=(n // tm,),
        in_specs=[rows(D_MODEL), rows(oa.shape[1]), rows(ob.shape[1]), rows(oc.shape[1]),
                  _const_spec(wg.shape), _const_spec(wa.shape), _const_spec(wb.shape), _const_spec(wc.shape),
                  _const_spec(wo.shape), _const_spec(ln_g.shape), _const_spec(ln_b.shape)],
        out_specs=rows(D_MODEL),
        out_shape=jax.ShapeDtypeStruct((n, D_MODEL), F32),
        compiler_params=_cparams("parallel"),
        name="merge_out_ln",
    )(x2d, oa, ob, oc, wg, wa, wb, wc, wo, ln_g, ln_b)


_FF_CHUNK = 1024


def _ffn_kernel(x_ref, p_ref, wu_ref, wd_ref, wpg_ref, wp_ref, g_ref, b_ref, y_ref):
    x = x_ref[...]
    xb = x.astype(BF)
    y = ALPHA * x + jax.nn.sigmoid(_dot(xb, wpg_ref[...])) * _dot(p_ref[...].astype(BF), wp_ref[...])
    for c in range(D_FF // _FF_CHUNK):
        cs = slice(c * _FF_CHUNK, (c + 1) * _FF_CHUNK)
        u = jnp.maximum(_dot(xb, wu_ref[:, cs]), 0.0)
        y = y + _dot((u * u).astype(BF), wd_ref[cs, :])
    y_ref[...] = _layer_norm(y, g_ref, b_ref)


def _ffn(x2d, p2d, wu, wd, wpg, wp, ln_g, ln_b, tm):
    n = x2d.shape[0]
    rows = lambda w: pl.BlockSpec((tm, w), lambda i: (i, 0))
    return pl.pallas_call(
        _ffn_kernel,
        grid=(n // tm,),
        in_specs=[rows(D_MODEL), rows(PLE_DIM), _const_spec(wu.shape), _const_spec(wd.shape),
                  _const_spec(wpg.shape), _const_spec(wp.shape), _const_spec(ln_g.shape), _const_spec(ln_b.shape)],
        out_specs=rows(D_MODEL),
        out_shape=jax.ShapeDtypeStruct((n, D_MODEL), F32),
        compiler_params=_cparams("parallel"),
        name="ffn_ple_ln",
    )(x2d, p2d, wu, wd, wpg, wp, ln_g, ln_b)


def kernel(x, p, w_in, w_gate, w_br_a, w_br_b, w_br_c, w_out, ln1_g, ln1_b,
           w_up, w_down, w_ple_gate, w_ple, ln2_g, ln2_b, rel_bias):
    bsz, s_len, d_model = x.shape
    assert d_model == D_MODEL and s_len == MAX_DISTANCE, (x.shape,)
    n_tok = bsz * s_len
    tm = 512

    b_head0 = A_HEADS
    c_head0 = A_HEADS + B_GROUPS * B_SLOTS
    bias_a = _causal_bias_tiles(rel_bias, 0, A_HEADS, s_len)
    bias_b = _band_bias_tiles(rel_bias, b_head0)
    bias_c = _causal_bias_tiles(rel_bias, c_head0, C_HEADS, s_len)

    x2d = x.reshape(n_tok, D_MODEL)
    for i in range(DEPTH):
        pr = _project(x2d, *_pack_w_in(w_in[i]), tm, s_len)
        seq = lambda name: pr[name].reshape(bsz, s_len, -1)
        o_a = _dsa(seq("aq"), seq("iq"), pr["iwT"], seq("akk"), seq("aii"), pr["avT"], bias_a)
        o_b = _dilated((seq("bq0"), seq("bq1"), seq("bq2")), seq("bk"), seq("bv"), bias_b)
        o_c = _moba(seq("cq"), seq("ck"), pr["cvT"], bias_c)
        flat = lambda a: a.reshape(n_tok, -1)
        row = lambda a: a.reshape(1, D_MODEL)
        x2d = _merge(x2d, flat(o_a), flat(o_b), flat(o_c), w_gate[i].astype(BF), w_br_a[i].astype(BF),
                     w_br_b[i].astype(BF), w_br_c[i].astype(BF), w_out[i].astype(BF),
                     row(ln1_g[i]), row(ln1_b[i]), tm)
        x2d = _ffn(x2d, p[i].reshape(n_tok, PLE_DIM), w_up[i].astype(BF), w_down[i].astype(BF),
                   w_ple_gate[i].astype(BF), w_ple[i].astype(BF), row(ln2_g[i]), row(ln2_b[i]), tm)
    return x2d.reshape(bsz, s_len, D_MODEL)
```

```python
import functools
import math

import numpy as np
import jax
import jax.numpy as jnp
from jax import lax
from jax.experimental import pallas as pl
from jax.experimental.pallas import tpu as pltpu

D_MODEL = 1024
HEAD_DIM = 64
A_HEADS = 6
IDX_HEADS = 8
A_TOPK_MAX = 256
B_SLOTS = 4
B_PATTERNS = ((128, 1), (512, 4), (2048, 16))
B_GROUPS = 3
C_HEADS = 6
C_BLOCK = 256
C_TOPK = 3
N_BUCKETS = 32
MAX_DISTANCE = 2048
D_FF = 4 * D_MODEL
PLE_DIM = 256
DEPTH = 2
ALPHA = (2 * DEPTH) ** 0.25
LN_EPS = 1e-5
NEG = -1e30
QK_SCALE = HEAD_DIM ** -0.5

LANES = 128
SUBLANES = 8
BF16_ROWS = 16
VMEM_LIMIT = 56 * 1024 * 1024

TQ = 128
TILES_PER_GROUP = C_BLOCK // TQ
BAND = 128
BF = jnp.bfloat16
F32 = jnp.float32
INT_MIN = -2 ** 31

_NT = (((1,), (1,)), ((), ()))


def _dot(a, b):
    return jnp.dot(a, b, preferred_element_type=F32)


def _dot_nt(a, b):
    return lax.dot_general(a, b, _NT, preferred_element_type=F32)


def _cparams(*sem):
    return pltpu.CompilerParams(dimension_semantics=sem, vmem_limit_bytes=VMEM_LIMIT)


def _bucket_starts():
    d = np.arange(0, MAX_DISTANCE + 1)
    max_exact = N_BUCKETS // 2
    nf = np.maximum(d, 1).astype(np.float32)
    large = max_exact + (np.log(nf / np.float32(max_exact)) / np.float32(math.log(MAX_DISTANCE / max_exact))
                         * np.float32(N_BUCKETS - max_exact)).astype(np.int32)
    bucket = np.where(d < max_exact, d, np.minimum(large, N_BUCKETS - 1))
    return [int(np.argmax(bucket >= b)) if np.any(bucket >= b) else None for b in range(N_BUCKETS)]


_BUCKET_START = _bucket_starts()


def _bias_from_distance(dist, tab_ref, col):
    val = jnp.full(dist.shape, tab_ref[0, col], F32)
    for b in range(1, N_BUCKETS):
        if _BUCKET_START[b] is not None:
            val = jnp.where(dist >= _BUCKET_START[b], tab_ref[b, col], val)
    return jnp.where(dist < 0, NEG, val)


def _causal_bias_kernel(tab_ref, o_ref, *, head0):
    i = pl.program_id(0)
    h = pl.program_id(1)
    s_len, tq = o_ref.shape[2], o_ref.shape[3]
    s = lax.broadcasted_iota(jnp.int32, (s_len, tq), 0)
    t = i * tq + lax.broadcasted_iota(jnp.int32, (s_len, tq), 1)
    o_ref[0, 0] = _bias_from_distance(t - s, tab_ref, head0 + h)


def _causal_bias_tiles(rel_bias, head0, n_heads, s_len):
    n_t = s_len // TQ
    return pl.pallas_call(
        functools.partial(_causal_bias_kernel, head0=head0),
        grid=(n_t, n_heads),
        in_specs=[pl.BlockSpec(memory_space=pltpu.SMEM)],
        out_specs=pl.BlockSpec((1, 1, s_len, TQ), lambda i, h: (i, h, 0, 0)),
        out_shape=jax.ShapeDtypeStruct((n_t, n_heads, s_len, TQ), F32),
        compiler_params=_cparams("arbitrary", "arbitrary"),
        name="causal_bias_tiles",
    )(rel_bias)


def _band_bias_kernel(tab_ref, o_ref, *, head0):
    g = pl.program_id(0)
    h = pl.program_id(1)
    row = lax.broadcasted_iota(jnp.int32, (BAND, 2 * BAND), 0)
    col = lax.broadcasted_iota(jnp.int32, (BAND, 2 * BAND), 1)
    j = row + BAND - col
    for gi, (_, dil) in enumerate(B_PATTERNS):
        @pl.when(g == gi)
        def _(dil=dil):
            bias = _bias_from_distance(j * dil, tab_ref, head0 + g * B_SLOTS + h)
            o_ref[0, 0] = jnp.where(j > BAND, NEG, bias)


def _band_bias_tiles(rel_bias, head0):
    return pl.pallas_call(
        functools.partial(_band_bias_kernel, head0=head0),
        grid=(B_GROUPS, B_SLOTS),
        in_specs=[pl.BlockSpec(memory_space=pltpu.SMEM)],
        out_specs=pl.BlockSpec((1, 1, BAND, 2 * BAND), lambda g, h: (g, h, 0, 0)),
        out_shape=jax.ShapeDtypeStruct((B_GROUPS, B_SLOTS, BAND, 2 * BAND), F32),
        compiler_params=_cparams("arbitrary", "arbitrary"),
        name="band_bias_tiles",
    )(rel_bias)


_PROJ_OUTS = (
    ("aq", A_HEADS * HEAD_DIM, QK_SCALE),
    ("akk", 2 * HEAD_DIM, 1.0),
    ("aii", 2 * HEAD_DIM, 1.0),
    ("iq", IDX_HEADS * HEAD_DIM, QK_SCALE),
    ("bq0", B_SLOTS * HEAD_DIM, QK_SCALE),
    ("bq1", B_SLOTS * HEAD_DIM, QK_SCALE),
    ("bq2", B_SLOTS * HEAD_DIM, QK_SCALE),
    ("bk", B_SLOTS * HEAD_DIM, 1.0),
    ("bv", B_SLOTS * HEAD_DIM, 1.0),
    ("cq", C_HEADS * HEAD_DIM, QK_SCALE),
    ("ck", C_HEADS * HEAD_DIM, 1.0),
)
_PROJ_WIDTH = sum(w for _, w, _ in _PROJ_OUTS)
_PROJ_OUTS_T = (
    ("avT", HEAD_DIM, BF),
    ("cvT", C_HEADS * HEAD_DIM, BF),
    ("iwT", BF16_ROWS, F32),
)
_PROJ_ROWS_T = sum(r for _, r, _ in _PROJ_OUTS_T)


def _pack_w_in(w):
    widths = (384, 64, 64, 512, 64, 8, 768, 256, 256, 384, 384, 384)
    offs = np.concatenate([[0], np.cumsum(widths)])
    aq, ak, av, iq, ik, iw, bq, bk, bv, cq, ck, cv = (w[:, offs[n]:offs[n + 1]] for n in range(12))
    cols = jnp.concatenate([aq, ak, ak, ik, ik, iq, bq, bk, bv, cq, ck], axis=1)
    iw_pad = jnp.concatenate([iw, jnp.zeros((w.shape[0], BF16_ROWS - IDX_HEADS), w.dtype)], axis=1)
    rows = jnp.concatenate([av, cv, iw_pad], axis=1).T
    return cols.astype(BF), rows.astype(BF)


def _proj_kernel(x_ref, w_ref, wt_ref, *o_refs):
    xb = x_ref[...].astype(BF)
    off = 0
    for o_ref, (_, width, scale) in zip(o_refs, _PROJ_OUTS):
        res = _dot(xb, w_ref[:, off:off + width])
        if scale != 1.0:
            res = res * scale
        o_ref[...] = res.astype(o_ref.dtype)
        off += width
    off = 0
    for o_ref, (_, rows, _) in zip(o_refs[len(_PROJ_OUTS):], _PROJ_OUTS_T):
        o_ref[0] = _dot_nt(wt_ref[off:off + rows, :], xb).astype(o_ref.dtype)
        off += rows


def _project(x2d, w_cols, w_rows, tm, s_len):
    n = x2d.shape[0]
    tiles_per_seq = s_len // tm
    out_shape = [jax.ShapeDtypeStruct((n, w), BF) for _, w, _ in _PROJ_OUTS]
    out_specs = [pl.BlockSpec((tm, w), lambda i: (i, 0)) for _, w, _ in _PROJ_OUTS]
    out_shape += [jax.ShapeDtypeStruct((n // s_len, r, s_len), dt) for _, r, dt in _PROJ_OUTS_T]
    out_specs += [pl.BlockSpec((1, r, tm), lambda i: (i // tiles_per_seq, 0, i % tiles_per_seq))
                  for _, r, _ in _PROJ_OUTS_T]
    outs = pl.pallas_call(
        _proj_kernel,
        grid=(n // tm,),
        in_specs=[pl.BlockSpec((tm, D_MODEL), lambda i: (i, 0)),
                  pl.BlockSpec((D_MODEL, _PROJ_WIDTH), lambda i: (0, 0)),
                  pl.BlockSpec((_PROJ_ROWS_T, D_MODEL), lambda i: (0, 0))],
        out_specs=out_specs,
        out_shape=out_shape,
        compiler_params=_cparams("parallel"),
        name="in_proj",
    )(x2d, w_cols, w_rows)
    names = [name for name, _, _ in _PROJ_OUTS] + [name for name, _, _ in _PROJ_OUTS_T]
    return dict(zip(names, outs))


def _head_pair_rhs(pair):
    lane = lax.broadcasted_iota(jnp.int32, pair.shape, 1)
    zero = jnp.zeros_like(pair)
    return jnp.concatenate([jnp.where(lane < HEAD_DIM, pair, zero), jnp.where(lane < HEAD_DIM, zero, pair)], axis=0)


ROW_CHUNK = 8 * SUBLANES


def _max_rows(x):
    acc = x[:ROW_CHUNK]
    for c in range(1, x.shape[0] // ROW_CHUNK):
        acc = jnp.maximum(acc, x[c * ROW_CHUNK:(c + 1) * ROW_CHUNK])
    return jnp.max(acc, axis=0, keepdims=True)


def _softmax_pv(logits, v_t_ones):
    m = _max_rows(logits)
    e = jnp.exp(logits - m).astype(BF)
    o = _dot(v_t_ones, e)
    return o[:HEAD_DIM] / o[HEAD_DIM:HEAD_DIM + 1]


def _with_ones_rows(v_t):
    return jnp.concatenate([v_t, jnp.ones((BF16_ROWS, v_t.shape[1]), v_t.dtype)], axis=0)


def _grouped_tiles(call_group, s_len):
    n_groups = s_len // (TILES_PER_GROUP * TQ)
    return jnp.concatenate([call_group(g, (g + 1) * TILES_PER_GROUP * TQ) for g in range(n_groups)], axis=1)


def _dsa_kernel(q_ref, iq_ref, iw_ref, kk_ref, ii_ref, vt_ref, bias_ref, o_ref, key_ref, mask_ref, *, topk, tile0):
    i = tile0 + pl.program_id(0)
    n_keys, tq = key_ref.shape
    pos = lax.broadcasted_iota(jnp.int32, (n_keys, tq), 0)
    qry = i * tq + lax.broadcasted_iota(jnp.int32, (n_keys, tq), 1)

    if n_keys > topk:
        iq = iq_ref[0]
        iw = iw_ref[0]
        ii = ii_ref[0]
        index = jnp.zeros((n_keys, tq), F32)
        for p in range(IDX_HEADS // 2):
            sc = _dot_nt(ii, _head_pair_rhs(iq[:, p * LANES:(p + 1) * LANES]))
            index = index + iw[2 * p:2 * p + 1, :] * jnp.maximum(sc[:, :tq], 0.0)
            index = index + iw[2 * p + 1:2 * p + 2, :] * jnp.maximum(sc[:, tq:], 0.0)
        index = jnp.where(pos <= qry, index + 0.0, -jnp.inf)
        bits = pltpu.bitcast(index, jnp.int32)
        key_ref[...] = jnp.where(bits < 0, bits ^ jnp.int32(0x7FFFFFFF), bits)

        chunk_pos = lax.broadcasted_iota(jnp.int32, (ROW_CHUNK, tq), 0)

        def count(pred):
            acc = None
            for c in range(n_keys // ROW_CHUNK):
                hit = pred(key_ref[c * ROW_CHUNK:(c + 1) * ROW_CHUNK, :], chunk_pos + c * ROW_CHUNK)
                ones = jnp.where(hit, 1.0, 0.0)
                acc = ones if acc is None else acc + ones
            return jnp.sum(acc, axis=0, keepdims=True)

        k_f = float(topk)
        c0 = count(lambda k, _: k >= 0)
        thr0 = jnp.where(c0 >= k_f, 0, INT_MIN).astype(jnp.int32)
        cnt0 = jnp.where(c0 >= k_f, c0, float(n_keys))

        def thr_step(it, carry):
            thr, cnt = carry
            cand = thr | (jnp.int32(1) << (30 - it))
            c = count(lambda k, _: k >= cand)
            ok = c >= k_f
            return jnp.where(ok, cand, thr), jnp.where(ok, c, cnt)

        thr, cnt_ge = lax.fori_loop(0, 31, thr_step, (thr0, cnt0))

        def tie_search():
            remaining = k_f - count(lambda k, _: k > thr)
            n_bits = (n_keys - 1).bit_length()

            def tie_step(it, last):
                cand = last | (jnp.int32(1) << (n_bits - 1 - it))
                c = count(lambda k, kpos: (k == thr) & (kpos < cand))
                return jnp.where(c < remaining, cand, last)

            return lax.fori_loop(0, n_bits, tie_step, jnp.zeros((1, tq), jnp.int32))

        last_tie = lax.cond(jnp.max(cnt_ge) > k_f, tie_search,
                            lambda: jnp.full((1, tq), n_keys, jnp.int32))
        key = key_ref[...]
        selected = (key > thr) | ((key == thr) & (pos <= last_tie))
        mask_ref[...] = jnp.where(selected, 0.0, NEG)
    else:
        mask_ref[...] = jnp.zeros_like(mask_ref)

    q = q_ref[0]
    kk = kk_ref[0]
    v_t = _with_ones_rows(vt_ref[0])
    outs = []
    for p in range(A_HEADS // 2):
        logits = _dot_nt(kk, _head_pair_rhs(q[:, p * LANES:(p + 1) * LANES]))
        for half in range(2):
            lg = logits[:, half * tq:(half + 1) * tq] + bias_ref[0, 2 * p + half] + mask_ref[...]
            outs.append(_softmax_pv(lg, v_t))
    o_ref[0] = jnp.concatenate(outs, axis=0).T.astype(o_ref.dtype)


def _dsa(aq, iq, iw_t, akk, aii, av_t, bias_tiles):
    bsz, s_len, _ = aq.shape
    topk = min(A_TOPK_MAX, s_len // 4)
    rows = TILES_PER_GROUP * TQ

    def call_group(g, n_keys):
        tile0 = g * TILES_PER_GROUP
        q_spec = lambda w: pl.BlockSpec((1, TQ, w), lambda i, b: (b, tile0 + i, 0))
        keys = lambda w: pl.BlockSpec((1, n_keys, w), lambda i, b: (b, 0, 0))
        return pl.pallas_call(
            functools.partial(_dsa_kernel, topk=topk, tile0=tile0),
            grid=(TILES_PER_GROUP, bsz),
            in_specs=[q_spec(A_HEADS * HEAD_DIM), q_spec(IDX_HEADS * HEAD_DIM),
                      pl.BlockSpec((1, BF16_ROWS, TQ), lambda i, b: (b, 0, tile0 + i)),
                      keys(2 * HEAD_DIM), keys(2 * HEAD_DIM),
                      pl.BlockSpec((1, HEAD_DIM, n_keys), lambda i, b: (b, 0, 0)),
                      pl.BlockSpec((1, A_HEADS, n_keys, TQ), lambda i, b: (tile0 + i, 0, 0, 0))],
            out_specs=pl.BlockSpec((1, TQ, A_HEADS * HEAD_DIM), lambda i, b: (b, i, 0)),
            out_shape=jax.ShapeDtypeStruct((bsz, rows, A_HEADS * HEAD_DIM), BF),
            scratch_shapes=[pltpu.VMEM((n_keys, TQ), jnp.int32), pltpu.VMEM((n_keys, TQ), F32)],
            compiler_params=_cparams("arbitrary", "arbitrary"),
            name=f"dsa_mixer_{g}",
        )(aq, iq, iw_t, akk, aii, av_t, bias_tiles)

    return _grouped_tiles(call_group, s_len)


_STATE_W = B_SLOTS * HEAD_DIM + 2 * LANES
_SLOT_LANES = LANES // B_SLOTS


def _band_tile(q, k_own, v_own, k_prev, v_prev, bias_ref, h, prev_bias):
    sl = slice(h * HEAD_DIM, (h + 1) * HEAD_DIM)
    l_own = _dot_nt(q[:, sl], k_own[:, sl]) + bias_ref[0, h, :, BAND:]
    m = jnp.max(l_own, axis=1, keepdims=True)
    if k_prev is not None:
        l_prev = _dot_nt(q[:, sl], k_prev[:, sl]) + bias_ref[0, h, :, :BAND] + prev_bias
        m = jnp.maximum(m, jnp.max(l_prev, axis=1, keepdims=True))
    e_own = jnp.exp(l_own - m)
    den = jnp.sum(e_own, axis=1, keepdims=True)
    acc = _dot(e_own.astype(BF), v_own[:, sl])
    if k_prev is not None:
        e_prev = jnp.exp(l_prev - m)
        den = den + jnp.sum(e_prev, axis=1, keepdims=True)
        acc = acc + _dot(e_prev.astype(BF), v_prev[:, sl])
    return acc, m, den


def _slot_stats(vals):
    lane = lax.broadcasted_iota(jnp.int32, (BAND, LANES), 1)
    out = jnp.broadcast_to(vals[B_SLOTS - 1], (BAND, LANES))
    for h in range(B_SLOTS - 2, -1, -1):
        out = jnp.where(lane < (h + 1) * _SLOT_LANES, vals[h], out)
    return out


def _band_state_kernel(q_ref, k_ref, v_ref, kp_ref, vp_ref, bias_ref, st_ref, *, n_cls):
    c = pl.program_id(1)
    n_tiles = q_ref.shape[1] // BAND
    w = B_SLOTS * HEAD_DIM
    first_prev_bias = jnp.where(c > 0, 0.0, NEG)
    for r in range(n_cls):
        cs = slice(r * w, (r + 1) * w)
        for t in range(n_tiles):
            rs = slice(t * BAND, (t + 1) * BAND)
            q, k_own, v_own = q_ref[0, rs, cs], k_ref[0, rs, cs], v_ref[0, rs, cs]
            if t == 0:
                k_prev, v_prev, prev_bias = kp_ref[0, :, cs], vp_ref[0, :, cs], first_prev_bias
            else:
                ps = slice((t - 1) * BAND, t * BAND)
                k_prev, v_prev, prev_bias = k_ref[0, ps, cs], v_ref[0, ps, cs], 0.0
            parts = [_band_tile(q, k_own, v_own, k_prev, v_prev, bias_ref, h, prev_bias) for h in range(B_SLOTS)]
            state = jnp.concatenate([p[0] for p in parts]
                                    + [_slot_stats([p[1] for p in parts]), _slot_stats([p[2] for p in parts])],
                                    axis=1)
            st_ref[0, rs, r * _STATE_W:(r + 1) * _STATE_W] = state


def _band_state(q, k, v, bias_tiles, g, rows, n_cls):
    bsz, length, width = q.shape
    n_chunks = length // rows
    prev_blocks = rows // BAND
    main = pl.BlockSpec((1, rows, width), lambda b, c: (b, c, 0))
    prev = pl.BlockSpec((1, BAND, width), lambda b, c: (b, jnp.maximum(c * prev_blocks - 1, 0), 0))
    return pl.pallas_call(
        functools.partial(_band_state_kernel, n_cls=n_cls),
        grid=(bsz, n_chunks),
        in_specs=[main, main, main, prev, prev,
                  pl.BlockSpec((1, B_SLOTS, BAND, 2 * BAND), lambda b, c: (g, 0, 0, 0))],
        out_specs=pl.BlockSpec((1, rows, n_cls * _STATE_W), lambda b, c: (b, c, 0)),
        out_shape=jax.ShapeDtypeStruct((bsz, length, n_cls * _STATE_W), F32),
        compiler_params=_cparams("parallel", "arbitrary"),
        name=f"band_state_{g}",
    )(q, k, v, k, v, bias_tiles)


def _band_merge_kernel(q_ref, k_ref, v_ref, st1_ref, st2_ref, bias_ref, o_ref, *, n_cls):
    w = B_SLOTS * HEAD_DIM
    for r in range(n_cls):
        cs = slice(r * w, (r + 1) * w)
        q, k_own, v_own = q_ref[0, :, cs], k_ref[0, :, cs], v_ref[0, :, cs]
        base = r * _STATE_W
        acc1, acc2 = st1_ref[0, :, base:base + w], st2_ref[0, :, base:base + w]
        max1, max2 = st1_ref[0, :, base + w:base + w + LANES], st2_ref[0, :, base + w:base + w + LANES]
        den1 = st1_ref[0, :, base + w + LANES:base + _STATE_W]
        den2 = st2_ref[0, :, base + w + LANES:base + _STATE_W]
        outs = []
        for h in range(B_SLOTS):
            acc3, m3, den3 = _band_tile(q, k_own, v_own, None, None, bias_ref, h, 0.0)
            hs = slice(h * HEAD_DIM, (h + 1) * HEAD_DIM)
            ss = slice(h * _SLOT_LANES, h * _SLOT_LANES + 1)
            m1, m2 = max1[:, ss], max2[:, ss]
            m_all = jnp.maximum(jnp.maximum(m1, m2), m3)
            w1, w2, w3 = jnp.exp(m1 - m_all), jnp.exp(m2 - m_all), jnp.exp(m3 - m_all)
            num = w1 * acc1[:, hs] + w2 * acc2[:, hs] + w3 * acc3
            den = w1 * den1[:, ss] + w2 * den2[:, ss] + w3 * den3
            outs.append(num / den)
        o_ref[0, :, cs] = jnp.concatenate(outs, axis=1).astype(o_ref.dtype)


def _band_merge(q, k, v, st1, st2, bias_tiles, g, n_cls):
    bsz, length, width = q.shape
    full = lambda wd: pl.BlockSpec((1, length, wd), lambda b: (b, 0, 0))
    return pl.pallas_call(
        functools.partial(_band_merge_kernel, n_cls=n_cls),
        grid=(bsz,),
        in_specs=[full(width), full(width), full(width), full(n_cls * _STATE_W), full(n_cls * _STATE_W),
                  pl.BlockSpec((1, B_SLOTS, BAND, 2 * BAND), lambda b: (g, 0, 0, 0))],
        out_specs=full(width),
        out_shape=jax.ShapeDtypeStruct((bsz, length, width), BF),
        compiler_params=_cparams("parallel"),
        name="band_merge",
    )(q, k, v, st1, st2, bias_tiles)


def _dilated(bq, bk, bv, bias_tiles):
    bsz, s_len, w = bk.shape
    view = lambda a, dil: a.reshape(bsz, s_len // dil, dil * a.shape[-1])
    (_, d0), (_, d1), (_, d2) = B_PATTERNS
    st0 = _band_state(view(bq[0], d0), view(bk, d0), view(bv, d0), bias_tiles, 0, rows=4 * BAND, n_cls=d0)
    st1 = _band_state(view(bq[1], d1), view(bk, d1), view(bv, d1), bias_tiles, 1, rows=BAND, n_cls=d1)
    st0 = view(st0.reshape(bsz, s_len, _STATE_W), d2)
    st1 = view(st1.reshape(bsz, s_len, _STATE_W), d2)
    out = _band_merge(view(bq[2], d2), view(bk, d2), view(bv, d2), st0, st1, bias_tiles, 2, n_cls=d2)
    return out.reshape(bsz, s_len, w)


def _block_mean_kernel(k_ref, o_ref):
    n_blk = k_ref.shape[1] // C_BLOCK
    o_ref[...] = jnp.zeros_like(o_ref)
    for j in range(n_blk):
        blk = k_ref[0, j * C_BLOCK:(j + 1) * C_BLOCK, :].astype(F32)
        o_ref[0, j:j + 1, :] = (jnp.sum(blk, axis=0, keepdims=True) * (1.0 / C_BLOCK)).astype(o_ref.dtype)


def _block_means(ck):
    bsz, s_len, w = ck.shape
    assert s_len // C_BLOCK <= BF16_ROWS
    return pl.pallas_call(
        _block_mean_kernel,
        grid=(bsz,),
        in_specs=[pl.BlockSpec((1, s_len, w), lambda b: (b, 0, 0))],
        out_specs=pl.BlockSpec((1, BF16_ROWS, w), lambda b: (b, 0, 0)),
        out_shape=jax.ShapeDtypeStruct((bsz, BF16_ROWS, w), BF),
        compiler_params=_cparams("parallel"),
        name="moba_block_means",
    )(ck)


def _moba_kernel(q_ref, k_ref, vt_ref, kmean_ref, bias_ref, o_ref, *, cur):
    tq = q_ref.shape[1]
    q = q_ref[0]
    kmean = kmean_ref[0]
    blk = lax.broadcasted_iota(jnp.int32, (BF16_ROWS, tq), 0)
    outs = []
    for p in range(C_HEADS // 2):
        ps = slice(p * LANES, (p + 1) * LANES)
        rhs = _head_pair_rhs(q[:, ps])
        logits = _dot_nt(k_ref[0, :, ps], rhs)
        gates = _dot_nt(kmean[:, ps], rhs)
        for half in range(2):
            h = 2 * p + half
            hq = slice(half * tq, (half + 1) * tq)
            if cur > C_TOPK:
                gate = jnp.where(blk < cur, gates[:, hq], -jnp.inf)
                off = jnp.where(blk == cur, 0.0, NEG)
                for j in range(cur):
                    gj = gate[j:j + 1, :]
                    beats = (gate > gj) | ((gate == gj) & (blk < j))
                    rank = jnp.sum(jnp.where(beats, 1.0, 0.0), axis=0, keepdims=True)
                    off = jnp.where((blk == j) & (rank < float(C_TOPK)), 0.0, off)
            pieces = []
            for j in range(cur + 1):
                ks = slice(j * C_BLOCK, (j + 1) * C_BLOCK)
                piece = logits[ks, hq] + bias_ref[0, h, ks, :]
                if cur > C_TOPK:
                    piece = piece + off[j:j + 1, :]
                pieces.append(piece)
            lg = jnp.concatenate(pieces, axis=0)
            outs.append(_softmax_pv(lg, _with_ones_rows(vt_ref[0, h * HEAD_DIM:(h + 1) * HEAD_DIM, :])))
    o_ref[0] = jnp.concatenate(outs, axis=0).T.astype(o_ref.dtype)


def _moba(cq, ck, cv_t, bias_tiles):
    bsz, s_len, w = cq.shape
    assert TILES_PER_GROUP * TQ == C_BLOCK
    kmean = _block_means(ck)

    def call_group(g, n_keys):
        tile0 = g * TILES_PER_GROUP
        return pl.pallas_call(
            functools.partial(_moba_kernel, cur=g),
            grid=(TILES_PER_GROUP, bsz),
            in_specs=[pl.BlockSpec((1, TQ, w), lambda i, b: (b, tile0 + i, 0)),
                      pl.BlockSpec((1, n_keys, w), lambda i, b: (b, 0, 0)),
                      pl.BlockSpec((1, w, n_keys), lambda i, b: (b, 0, 0)),
                      pl.BlockSpec((1, BF16_ROWS, w), lambda i, b: (b, 0, 0)),
                      pl.BlockSpec((1, C_HEADS, n_keys, TQ), lambda i, b: (tile0 + i, 0, 0, 0))],
            out_specs=pl.BlockSpec((1, TQ, w), lambda i, b: (b, i, 0)),
            out_shape=jax.ShapeDtypeStruct((bsz, C_BLOCK, w), BF),
            compiler_params=_cparams("arbitrary", "arbitrary"),
            name=f"moba_mixer_{g}",
        )(cq, ck, cv_t, kmean, bias_tiles)

    return _grouped_tiles(call_group, s_len)


def _layer_norm(y, g_ref, b_ref):
    mu = jnp.mean(y, axis=-1, keepdims=True)
    yc = y - mu
    var = jnp.mean(yc * yc, axis=-1, keepdims=True)
    return yc * lax.rsqrt(var + LN_EPS) * g_ref[...] + b_ref[...]


def _merge_kernel(x_ref, oa_ref, ob_ref, oc_ref, wg_ref, wa_ref, wb_ref, wc_ref, wo_ref, g_ref, b_ref, y_ref):
    x = x_ref[...]
    xb = x.astype(BF)
    merged = None
    for n, (o_ref, w_ref) in enumerate(((oa_ref, wa_ref), (ob_ref, wb_ref), (oc_ref, wc_ref))):
        gate = jax.nn.sigmoid(_dot(xb, wg_ref[:, n * D_MODEL:(n + 1) * D_MODEL]))
        term = gate * _dot(o_ref[...], w_ref[...])
        merged = term if merged is None else merged + term
    y = ALPHA * x + _dot(merged.astype(BF), wo_ref[...])
    y_ref[...] = _layer_norm(y, g_ref, b_ref)


def _const_spec(shape):
    return pl.BlockSpec(shape, lambda i: (0,) * len(shape), pipeline_mode=pl.Buffered(1))


def _merge(x2d, oa, ob, oc, wg, wa, wb, wc, wo, ln_g, ln_b, tm):
    n = x2d.shape[0]
    rows = lambda w: pl.BlockSpec((tm, w), lambda i: (i, 0))
    return pl.pallas_call(
        _merge_kernel,
        grid=(n // tm,),
        in_specs=[rows(D_MODEL), rows(oa.shape[1]), rows(ob.shape[1]), rows(oc.shape[1]),
                  _const_spec(wg.shape), _const_spec(wa.shape), _const_spec(wb.shape), _const_spec(wc.shape),
                  _const_spec(wo.shape), _const_spec(ln_g.shape), _const_spec(ln_b.shape)],
        out_specs=rows(D_MODEL),
        out_shape=jax.ShapeDtypeStruct((n, D_MODEL), F32),
        compiler_params=_cparams("parallel"),
        name="merge_out_ln",
    )(x2d, oa, ob, oc, wg, wa, wb, wc, wo, ln_g, ln_b)


_FF_CHUNK = 1024


def _ffn_kernel(x_ref, p_ref, wu_ref, wd_ref, wpg_ref, wp_ref, g_ref, b_ref, y_ref):
    x = x_ref[...]
    xb = x.astype(BF)
    y = ALPHA * x + jax.nn.sigmoid(_dot(xb, wpg_ref[...])) * _dot(p_ref[...].astype(BF), wp_ref[...])
    for c in range(D_FF // _FF_CHUNK):
        cs = slice(c * _FF_CHUNK, (c + 1) * _FF_CHUNK)
        u = jnp.maximum(_dot(xb, wu_ref[:, cs]), 0.0)
        y = y + _dot((u * u).astype(BF), wd_ref[cs, :])
    y_ref[...] = _layer_norm(y, g_ref, b_ref)


def _ffn(x2d, p2d, wu, wd, wpg, wp, ln_g, ln_b, tm):
    n = x2d.shape[0]
    rows = lambda w: pl.BlockSpec((tm, w), lambda i: (i, 0))
    return pl.pallas_call(
        _ffn_kernel,
        grid=(n // tm,),
        in_specs=[rows(D_MODEL), rows(PLE_DIM), _const_spec(wu.shape), _const_spec(wd.shape),
                  _const_spec(wpg.shape), _const_spec(wp.shape), _const_spec(ln_g.shape), _const_spec(ln_b.shape)],
        out_specs=rows(D_MODEL),
        out_shape=jax.ShapeDtypeStruct((n, D_MODEL), F32),
        compiler_params=_cparams("parallel"),
        name="ffn_ple_ln",
    )(x2d, p2d, wu, wd, wpg, wp, ln_g, ln_b)


def kernel(x, p, w_in, w_gate, w_br_a, w_br_b, w_br_c, w_out, ln1_g, ln1_b,
           w_up, w_down, w_ple_gate, w_ple, ln2_g, ln2_b, rel_bias):
    bsz, s_len, d_model = x.shape
    assert d_model == D_MODEL and s_len == MAX_DISTANCE, (x.shape,)
    n_tok = bsz * s_len
    tm = 512

    b_head0 = A_HEADS
    c_head0 = A_HEADS + B_GROUPS * B_SLOTS
    bias_a = _causal_bias_tiles(rel_bias, 0, A_HEADS, s_len)
    bias_b = _band_bias_tiles(rel_bias, b_head0)
    bias_c = _causal_bias_tiles(rel_bias, c_head0, C_HEADS, s_len)

    x2d = x.reshape(n_tok, D_MODEL)
    for i in range(DEPTH):
        pr = _project(x2d, *_pack_w_in(w_in[i]), tm, s_len)
        seq = lambda name: pr[name].reshape(bsz, s_len, -1)
        o_a = _dsa(seq("aq"), seq("iq"), pr["iwT"], seq("akk"), seq("aii"), pr["avT"], bias_a)
        o_b = _dilated((seq("bq0"), seq("bq1"), seq("bq2")), seq("bk"), seq("bv"), bias_b)
        o_c = _moba(seq("cq"), seq("ck"), pr["cvT"], bias_c)
        flat = lambda a: a.reshape(n_tok, -1)
        row = lambda a: a.reshape(1, D_MODEL)
        x2d = _merge(x2d, flat(o_a), flat(o_b), flat(o_c), w_gate[i].astype(BF), w_br_a[i].astype(BF),
                     w_br_b[i].astype(BF), w_br_c[i].astype(BF), w_out[i].astype(BF),
                     row(ln1_g[i]), row(ln1_b[i]), tm)
        x2d = _ffn(x2d, p[i].reshape(n_tok, PLE_DIM), w_up[i].astype(BF), w_down[i].astype(BF),
                   w_ple_gate[i].astype(BF), w_ple[i].astype(BF), row(ln2_g[i]), row(ln2_b[i]), tm)
    return x2d.reshape(bsz, s_len, D_MODEL)
```

```python
import functools
import math

import numpy as np
import jax
import jax.numpy as jnp
from jax import lax
from jax.experimental import pallas as pl
from jax.experimental.pallas import tpu as pltpu

D_MODEL = 1024
HEAD_DIM = 64
A_HEADS = 6
IDX_HEADS = 8
A_TOPK_MAX = 256
B_SLOTS = 4
B_PATTERNS = ((128, 1), (512, 4), (2048, 16))
B_GROUPS = 3
C_HEADS = 6
C_BLOCK = 256
C_TOPK = 3
N_BUCKETS = 32
MAX_DISTANCE = 2048
D_FF = 4 * D_MODEL
PLE_DIM = 256
DEPTH = 2
ALPHA = (2 * DEPTH) ** 0.25
LN_EPS = 1e-5
NEG = -1e30
QK_SCALE = HEAD_DIM ** -0.5

LANES = 128
SUBLANES = 8
BF16_ROWS = 16
VMEM_LIMIT = 56 * 1024 * 1024

TQ = 256
BIAS_TQ = LANES
TILES_PER_GROUP = C_BLOCK // TQ
BAND = 128
BF = jnp.bfloat16
F32 = jnp.float32
INT_MIN = -2 ** 31

_NT = (((1,), (1,)), ((), ()))


def _dot(a, b):
    return jnp.dot(a, b, preferred_element_type=F32)


def _dot_nt(a, b):
    return lax.dot_general(a, b, _NT, preferred_element_type=F32)


def _cparams(*sem):
    return pltpu.CompilerParams(dimension_semantics=sem, vmem_limit_bytes=VMEM_LIMIT)


def _bucket_starts():
    d = np.arange(0, MAX_DISTANCE + 1)
    max_exact = N_BUCKETS // 2
    nf = np.maximum(d, 1).astype(np.float32)
    large = max_exact + (np.log(nf / np.float32(max_exact)) / np.float32(math.log(MAX_DISTANCE / max_exact))
                         * np.float32(N_BUCKETS - max_exact)).astype(np.int32)
    bucket = np.where(d < max_exact, d, np.minimum(large, N_BUCKETS - 1))
    return [int(np.argmax(bucket >= b)) if np.any(bucket >= b) else None for b in range(N_BUCKETS)]


_BUCKET_START = _bucket_starts()


def _bias_from_distance(dist, tab_ref, col):
    val = jnp.full(dist.shape, tab_ref[0, col], F32)
    for b in range(1, N_BUCKETS):
        if _BUCKET_START[b] is not None:
            val = jnp.where(dist >= _BUCKET_START[b], tab_ref[b, col], val)
    return jnp.where(dist < 0, NEG, val)


def _causal_bias_kernel(tab_ref, o_ref, *, head0):
    h = pl.program_id(0)
    n_rows, tq = o_ref.shape[1], o_ref.shape[2]
    u = lax.broadcasted_iota(jnp.int32, (n_rows, tq), 0)
    t = lax.broadcasted_iota(jnp.int32, (n_rows, tq), 1)
    o_ref[0] = _bias_from_distance(t + (n_rows - tq) // 2 - u, tab_ref, head0 + h)


def _causal_bias_master(rel_bias, head0, n_heads, s_len):
    n_rows = 2 * s_len - BIAS_TQ
    return pl.pallas_call(
        functools.partial(_causal_bias_kernel, head0=head0),
        grid=(n_heads,),
        in_specs=[pl.BlockSpec(memory_space=pltpu.SMEM)],
        out_specs=pl.BlockSpec((1, n_rows, BIAS_TQ), lambda h: (h, 0, 0)),
        out_shape=jax.ShapeDtypeStruct((n_heads, n_rows, BIAS_TQ), F32),
        compiler_params=_cparams("arbitrary"),
        name="causal_bias_master",
    )(rel_bias)


def _bias_row0(t0, s_len):
    return pl.multiple_of(s_len - BIAS_TQ - t0, BIAS_TQ)


def _causal_bias(bias_ref, h, t0, first_key, n_keys, tq, s_len):
    parts = [bias_ref[h, pl.ds(_bias_row0(t0 + c * BIAS_TQ, s_len) + first_key, n_keys), :]
             for c in range(tq // BIAS_TQ)]
    return parts[0] if len(parts) == 1 else jnp.concatenate(parts, axis=1)


def _band_bias_kernel(tab_ref, o_ref, *, head0):
    g = pl.program_id(0)
    variant = pl.program_id(1)
    pair = pl.program_id(2)
    row = lax.broadcasted_iota(jnp.int32, (BAND, 2 * BAND), 0)
    col = lax.broadcasted_iota(jnp.int32, (BAND, 2 * BAND), 1)
    j = row + BAND - col
    no_prev = (variant == 1) & (col < BAND)
    for gi, (_, dil) in enumerate(B_PATTERNS):
        @pl.when(g == gi)
        def _(dil=dil):
            for half in range(2):
                bias = _bias_from_distance(j * dil, tab_ref, head0 + g * B_SLOTS + 2 * pair + half)
                bias = jnp.where((j > BAND) | no_prev, NEG, bias)
                o_ref[0, 0, 0, :, half * 2 * BAND:(half + 1) * 2 * BAND] = bias


def _band_bias_tiles(rel_bias, head0):
    return pl.pallas_call(
        functools.partial(_band_bias_kernel, head0=head0),
        grid=(B_GROUPS, 2, B_SLOTS // 2),
        in_specs=[pl.BlockSpec(memory_space=pltpu.SMEM)],
        out_specs=pl.BlockSpec((1, 1, 1, BAND, 4 * BAND), lambda g, v, p: (g, v, p, 0, 0)),
        out_shape=jax.ShapeDtypeStruct((B_GROUPS, 2, B_SLOTS // 2, BAND, 4 * BAND), F32),
        compiler_params=_cparams("arbitrary", "arbitrary", "arbitrary"),
        name="band_bias_tiles",
    )(rel_bias)


_PROJ_OUTS = (
    ("aq", A_HEADS * HEAD_DIM, QK_SCALE),
    ("akk", 2 * HEAD_DIM, 1.0),
    ("aii", 2 * HEAD_DIM, 1.0),
    ("iq", IDX_HEADS * HEAD_DIM, QK_SCALE),
    ("bq0", B_SLOTS * HEAD_DIM, QK_SCALE),
    ("bq1", B_SLOTS * HEAD_DIM, QK_SCALE),
    ("bq2", B_SLOTS * HEAD_DIM, QK_SCALE),
    ("bk", B_SLOTS * HEAD_DIM, 1.0),
    ("bv", B_SLOTS * HEAD_DIM, 1.0),
    ("cq", C_HEADS * HEAD_DIM, QK_SCALE),
    ("ck", C_HEADS * HEAD_DIM, 1.0),
)
_PROJ_WIDTH = sum(w for _, w, _ in _PROJ_OUTS)
_PROJ_LANE_TILED = ("bq0", "bq1", "bq2", "bk", "bv")
_PROJ_OUTS_T = (
    ("avT", HEAD_DIM, BF),
    ("cvT", C_HEADS * HEAD_DIM, BF),
    ("iwT", BF16_ROWS, F32),
)
_PROJ_ROWS_T = sum(r for _, r, _ in _PROJ_OUTS_T)


def _pack_w_in(w):
    widths = (384, 64, 64, 512, 64, 8, 768, 256, 256, 384, 384, 384)
    offs = np.concatenate([[0], np.cumsum(widths)])
    aq, ak, av, iq, ik, iw, bq, bk, bv, cq, ck, cv = (w[:, offs[n]:offs[n + 1]] for n in range(12))
    cols = jnp.concatenate([aq, ak, ak, ik, ik, iq, bq, bk, bv, cq, ck], axis=1)
    iw_pad = jnp.concatenate([iw, jnp.zeros((w.shape[0], BF16_ROWS - IDX_HEADS), w.dtype)], axis=1)
    rows = jnp.concatenate([av, cv, iw_pad], axis=1).T
    return cols.astype(BF), rows.astype(BF)


def _proj_kernel(x_ref, w_ref, wt_ref, *o_refs):
    xb = x_ref[...].astype(BF)
    off = 0
    for o_ref, (_, width, scale) in zip(o_refs, _PROJ_OUTS):
        res = _dot(xb, w_ref[:, off:off + width])
        if scale != 1.0:
            res = res * scale
        if len(o_ref.shape) == 4:
            for lt in range(width // LANES):
                o_ref[0, lt] = res[:, lt * LANES:(lt + 1) * LANES]
        else:
            o_ref[...] = res.astype(o_ref.dtype)
        off += width
    off = 0
    for o_ref, (_, rows, _) in zip(o_refs[len(_PROJ_OUTS):], _PROJ_OUTS_T):
        o_ref[0] = _dot_nt(wt_ref[off:off + rows, :], xb).astype(o_ref.dtype)
        off += rows


def _project(x2d, w_cols, w_rows, tm, s_len):
    n = x2d.shape[0]
    tiles_per_seq = s_len // tm
    out_shape, out_specs = [], []
    for name, w, _ in _PROJ_OUTS:
        if name in _PROJ_LANE_TILED:
            out_shape.append(jax.ShapeDtypeStruct((n // s_len, w // LANES, s_len, LANES), F32))
            out_specs.append(pl.BlockSpec((1, w // LANES, tm, LANES),
                                          lambda i: (i // tiles_per_seq, 0, i % tiles_per_seq, 0)))
        else:
            out_shape.append(jax.ShapeDtypeStruct((n, w), BF))
            out_specs.append(pl.BlockSpec((tm, w), lambda i: (i, 0)))
    out_shape += [jax.ShapeDtypeStruct((n // s_len, r, s_len), dt) for _, r, dt in _PROJ_OUTS_T]
    out_specs += [pl.BlockSpec((1, r, tm), lambda i: (i // tiles_per_seq, 0, i % tiles_per_seq))
                  for _, r, _ in _PROJ_OUTS_T]
    outs = pl.pallas_call(
        _proj_kernel,
        grid=(n // tm,),
        in_specs=[pl.BlockSpec((tm, D_MODEL), lambda i: (i, 0)),
                  pl.BlockSpec((D_MODEL, _PROJ_WIDTH), lambda i: (0, 0)),
                  pl.BlockSpec((_PROJ_ROWS_T, D_MODEL), lambda i: (0, 0))],
        out_specs=out_specs,
        out_shape=out_shape,
        compiler_params=_cparams("parallel"),
        name="in_proj",
    )(x2d, w_cols, w_rows)
    names = [name for name, _, _ in _PROJ_OUTS] + [name for name, _, _ in _PROJ_OUTS_T]
    return dict(zip(names, outs))


def _head_pair_rhs(pair):
    lane = lax.broadcasted_iota(jnp.int32, pair.shape, 1)
    zero = jnp.zeros_like(pair)
    return jnp.concatenate([jnp.where(lane < HEAD_DIM, pair, zero), jnp.where(lane < HEAD_DIM, zero, pair)], axis=0)


ROW_CHUNK = 8 * SUBLANES


def _max_rows(x):
    acc = x[:ROW_CHUNK]
    for c in range(1, x.shape[0] // ROW_CHUNK):
        acc = jnp.maximum(acc, x[c * ROW_CHUNK:(c + 1) * ROW_CHUNK])
    return jnp.max(acc, axis=0, keepdims=True)


def _softmax_pv(logits, v_t_ones):
    m = _max_rows(logits)
    e = jnp.exp(logits - m).astype(BF)
    o = _dot(v_t_ones, e)
    return o[:HEAD_DIM] / o[HEAD_DIM:HEAD_DIM + 1]


def _with_ones_rows(v_t):
    return jnp.concatenate([v_t, jnp.ones((BF16_ROWS, v_t.shape[1]), v_t.dtype)], axis=0)


def _const_spec2(shape):
    return pl.BlockSpec(shape, lambda i, b: (0,) * len(shape), pipeline_mode=pl.Buffered(1))


def _grouped_tiles(call_group, s_len):
    n_groups = s_len // (TILES_PER_GROUP * TQ)
    return jnp.concatenate([call_group(g, (g + 1) * TILES_PER_GROUP * TQ) for g in range(n_groups)], axis=1)


def _dsa_kernel(q_ref, iq_ref, iw_ref, kk_ref, ii_ref, vt_ref, bias_ref, o_ref, key_ref, mask_ref, *, topk, tile0, s_len):
    i = tile0 + pl.program_id(0)
    n_keys, tq = key_ref.shape
    pos = lax.broadcasted_iota(jnp.int32, (n_keys, tq), 0)
    qry = i * tq + lax.broadcasted_iota(jnp.int32, (n_keys, tq), 1)

    if n_keys > topk:
        iq = iq_ref[0]
        iw = iw_ref[0]
        ii = ii_ref[0]
        index = jnp.zeros((n_keys, tq), F32)
        for p in range(IDX_HEADS // 2):
            sc = _dot_nt(ii, _head_pair_rhs(iq[:, p * LANES:(p + 1) * LANES]))
            index = index + iw[2 * p:2 * p + 1, :] * jnp.maximum(sc[:, :tq], 0.0)
            index = index + iw[2 * p + 1:2 * p + 2, :] * jnp.maximum(sc[:, tq:], 0.0)
        index = jnp.where(pos <= qry, index + 0.0, -jnp.inf)
        bits = pltpu.bitcast(index, jnp.int32)
        key_ref[...] = jnp.where(bits < 0, bits ^ jnp.int32(0x7FFFFFFF), bits)

        chunk_pos = lax.broadcasted_iota(jnp.int32, (ROW_CHUNK, tq), 0)

        def count(pred):
            acc = None
            for c in range(n_keys // ROW_CHUNK):
                hit = pred(key_ref[c * ROW_CHUNK:(c + 1) * ROW_CHUNK, :], chunk_pos + c * ROW_CHUNK)
                ones = jnp.where(hit, 1.0, 0.0)
                acc = ones if acc is None else acc + ones
            return jnp.sum(acc, axis=0, keepdims=True)

        k_f = float(topk)
        c0 = count(lambda k, _: k >= 0)
        thr0 = jnp.where(c0 >= k_f, 0, INT_MIN).astype(jnp.int32)
        cnt0 = jnp.where(c0 >= k_f, c0, float(n_keys))

        def thr_step(it, carry):
            thr, cnt = carry
            cand = thr | (jnp.int32(1) << (30 - it))
            c = count(lambda k, _: k >= cand)
            ok = c >= k_f
            return jnp.where(ok, cand, thr), jnp.where(ok, c, cnt)

        thr, cnt_ge = lax.fori_loop(0, 31, thr_step, (thr0, cnt0))

        def tie_search():
            remaining = k_f - count(lambda k, _: k > thr)
            n_bits = (n_keys - 1).bit_length()

            def tie_step(it, last):
                cand = last | (jnp.int32(1) << (n_bits - 1 - it))
                c = count(lambda k, kpos: (k == thr) & (kpos < cand))
                return jnp.where(c < remaining, cand, last)

            return lax.fori_loop(0, n_bits, tie_step, jnp.zeros((1, tq), jnp.int32))

        last_tie = lax.cond(jnp.max(cnt_ge) > k_f, tie_search,
                            lambda: jnp.full((1, tq), n_keys, jnp.int32))
        key = key_ref[...]
        selected = (key > thr) | ((key == thr) & (pos <= last_tie))
        mask_ref[...] = jnp.where(selected, 0.0, NEG)
    else:
        mask_ref[...] = jnp.zeros_like(mask_ref)

    q = q_ref[0]
    kk = kk_ref[0]
    v_t = _with_ones_rows(vt_ref[0])
    outs = []
    for p in range(A_HEADS // 2):
        logits = _dot_nt(kk, _head_pair_rhs(q[:, p * LANES:(p + 1) * LANES]))
        for half in range(2):
            bias = _causal_bias(bias_ref, 2 * p + half, i * tq, 0, n_keys, tq, s_len)
            lg = logits[:, half * tq:(half + 1) * tq] + bias + mask_ref[...]
            outs.append(_softmax_pv(lg, v_t))
    o_ref[0] = jnp.concatenate(outs, axis=0).T.astype(o_ref.dtype)


def _dsa(aq, iq, iw_t, akk, aii, av_t, bias_master):
    bsz, s_len, _ = aq.shape
    topk = min(A_TOPK_MAX, s_len // 4)
    rows = TILES_PER_GROUP * TQ

    def call_group(g, n_keys):
        tile0 = g * TILES_PER_GROUP
        q_spec = lambda w: pl.BlockSpec((1, TQ, w), lambda i, b: (b, tile0 + i, 0))
        keys = lambda w: pl.BlockSpec((1, n_keys, w), lambda i, b: (b, 0, 0))
        return pl.pallas_call(
            functools.partial(_dsa_kernel, topk=topk, tile0=tile0, s_len=s_len),
            grid=(TILES_PER_GROUP, bsz),
            in_specs=[q_spec(A_HEADS * HEAD_DIM), q_spec(IDX_HEADS * HEAD_DIM),
                      pl.BlockSpec((1, BF16_ROWS, TQ), lambda i, b: (b, 0, tile0 + i)),
                      keys(2 * HEAD_DIM), keys(2 * HEAD_DIM),
                      pl.BlockSpec((1, HEAD_DIM, n_keys), lambda i, b: (b, 0, 0)),
                      _const_spec2(bias_master.shape)],
            out_specs=pl.BlockSpec((1, TQ, A_HEADS * HEAD_DIM), lambda i, b: (b, i, 0)),
            out_shape=jax.ShapeDtypeStruct((bsz, rows, A_HEADS * HEAD_DIM), BF),
            scratch_shapes=[pltpu.VMEM((n_keys, TQ), jnp.int32), pltpu.VMEM((n_keys, TQ), F32)],
            compiler_params=_cparams("arbitrary", "arbitrary"),
            name=f"dsa_mixer_{g}",
        )(aq, iq, iw_t, akk, aii, av_t, bias_master)

    return _grouped_tiles(call_group, s_len)


_PAIR_W = 2 * HEAD_DIM
_N_PAIRS = B_SLOTS // 2
_N_STATE = 2 * _N_PAIRS


def _class_rows(r, t, dil):
    start = r + t * BAND * dil
    return pl.ds(start, BAND, stride=dil) if dil > 1 else pl.ds(start, BAND)


def _pair_ones(n_keys):
    row = lax.broadcasted_iota(jnp.int32, (2 * n_keys, _PAIR_W), 0)
    lane = lax.broadcasted_iota(jnp.int32, (2 * n_keys, _PAIR_W), 1)
    return jnp.where((row < n_keys) == (lane < HEAD_DIM), 1.0, 0.0).astype(BF)


def _band_pairs(units, ones_bd):
    n_keys = units[0][1].shape[0]
    lane = lax.broadcasted_iota(jnp.int32, (BAND, _PAIR_W), 1)
    logits = [_dot_nt(q, _head_pair_rhs(k)) + bias for q, k, _, bias in units]
    maxes = [(jnp.max(lg[:, :n_keys], axis=1, keepdims=True), jnp.max(lg[:, n_keys:], axis=1, keepdims=True))
             for lg in logits]
    probs = [jnp.concatenate([jnp.exp(lg[:, :n_keys] - ma), jnp.exp(lg[:, n_keys:] - mb)], axis=1).astype(BF)
             for lg, (ma, mb) in zip(logits, maxes)]
    results = []
    for (_, _, v, _), e, (ma, mb) in zip(units, probs, maxes):
        acc = _dot(e, _head_pair_rhs(v))
        den = _dot(e, ones_bd)
        results.append((acc / den, jnp.where(lane < HEAD_DIM, ma, mb) + jnp.log(den)))
    return results


def _band_state_kernel(q_ref, k_ref, v_ref, kp_ref, vp_ref, bias_ref, st_ref, *, dil):
    c = pl.program_id(1)
    n_tiles = q_ref.shape[2] // (BAND * dil)
    first_variant = jnp.where(c == 0, 1, 0)
    load = lambda ref, p, rows: ref[0, p, rows, :].astype(BF)
    units, where = [], []
    for r in range(dil):
        for t in range(n_tiles):
            rows = _class_rows(r, t, dil)
            for p in range(_N_PAIRS):
                if t == 0:
                    prev = _class_rows(r, 0, dil)
                    k = jnp.concatenate([load(kp_ref, p, prev), load(k_ref, p, rows)], axis=0)
                    v = jnp.concatenate([load(vp_ref, p, prev), load(v_ref, p, rows)], axis=0)
                    bias = bias_ref[0, first_variant, p]
                else:
                    prev = _class_rows(r, t - 1, dil)
                    k = jnp.concatenate([load(k_ref, p, prev), load(k_ref, p, rows)], axis=0)
                    v = jnp.concatenate([load(v_ref, p, prev), load(v_ref, p, rows)], axis=0)
                    bias = bias_ref[0, 0, p]
                units.append((load(q_ref, p, rows), k, v, bias))
                where.append((rows, p))
    for (rows, p), (out, lse) in zip(where, _band_pairs(units, _pair_ones(2 * BAND))):
        st_ref[0, p, rows, :] = out
        st_ref[0, _N_PAIRS + p, rows, :] = lse


def _band_state(q, k, v, bias_tiles, g, dil):
    bsz, n_lt, s_len, _ = q.shape
    rows = 4 * BAND
    prev_rows = BAND * dil
    per = rows // prev_rows
    main = pl.BlockSpec((1, n_lt, rows, LANES), lambda b, c: (b, 0, c, 0))
    prev = pl.BlockSpec((1, n_lt, prev_rows, LANES), lambda b, c: (b, 0, jnp.maximum(c * per - 1, 0), 0))
    return pl.pallas_call(
        functools.partial(_band_state_kernel, dil=dil),
        grid=(bsz, s_len // rows),
        in_specs=[main, main, main, prev, prev,
                  pl.BlockSpec((1, 2, _N_PAIRS, BAND, 4 * BAND), lambda b, c: (g, 0, 0, 0, 0))],
        out_specs=pl.BlockSpec((1, _N_STATE, rows, LANES), lambda b, c: (b, 0, c, 0)),
        out_shape=jax.ShapeDtypeStruct((bsz, _N_STATE, s_len, LANES), F32),
        compiler_params=_cparams("parallel", "arbitrary"),
        name=f"band_state_{g}",
    )(q, k, v, k, v, bias_tiles)


def _band_merge_kernel(q_ref, k_ref, v_ref, st1_ref, st2_ref, bias_ref, o_ref, *, dil):
    load = lambda ref, p, rows: ref[0, p, rows, :].astype(BF)
    own_bias = [jnp.concatenate([bias_ref[0, 0, p, :, BAND:2 * BAND], bias_ref[0, 0, p, :, 3 * BAND:]], axis=1)
                for p in range(_N_PAIRS)]
    units, where = [], []
    for r in range(dil):
        rows = _class_rows(r, 0, dil)
        for p in range(_N_PAIRS):
            units.append((load(q_ref, p, rows), load(k_ref, p, rows), load(v_ref, p, rows), own_bias[p]))
            where.append((rows, p))
    for (rows, p), (out3, lse3) in zip(where, _band_pairs(units, _pair_ones(BAND))):
        lse1, lse2 = st1_ref[0, _N_PAIRS + p, rows, :], st2_ref[0, _N_PAIRS + p, rows, :]
        top = jnp.maximum(jnp.maximum(lse1, lse2), lse3)
        w1, w2, w3 = jnp.exp(lse1 - top), jnp.exp(lse2 - top), jnp.exp(lse3 - top)
        num = w1 * st1_ref[0, p, rows, :] + w2 * st2_ref[0, p, rows, :] + w3 * out3
        o_ref[0, p, rows, :] = num / (w1 + w2 + w3)


def _band_merge(q, k, v, st1, st2, bias_tiles, g, dil):
    bsz, n_lt, s_len, _ = q.shape
    assert s_len == BAND * dil
    full = lambda n: pl.BlockSpec((1, n, s_len, LANES), lambda b: (b, 0, 0, 0))
    return pl.pallas_call(
        functools.partial(_band_merge_kernel, dil=dil),
        grid=(bsz,),
        in_specs=[full(n_lt), full(n_lt), full(n_lt), full(_N_STATE), full(_N_STATE),
                  pl.BlockSpec((1, 2, _N_PAIRS, BAND, 4 * BAND), lambda b: (g, 0, 0, 0, 0))],
        out_specs=full(n_lt),
        out_shape=jax.ShapeDtypeStruct((bsz, n_lt, s_len, LANES), F32),
        compiler_params=_cparams("parallel"),
        name="band_merge",
    )(q, k, v, st1, st2, bias_tiles)


def _dilated(bq, bk, bv, bias_tiles):
    (_, d0), (_, d1), (_, d2) = B_PATTERNS
    st0 = _band_state(bq[0], bk, bv, bias_tiles, 0, d0)
    st1 = _band_state(bq[1], bk, bv, bias_tiles, 1, d1)
    return _band_merge(bq[2], bk, bv, st0, st1, bias_tiles, 2, d2)


def _block_mean_kernel(k_ref, o_ref):
    n_blk = k_ref.shape[1] // C_BLOCK
    o_ref[...] = jnp.zeros_like(o_ref)
    for j in range(n_blk):
        blk = k_ref[0, j * C_BLOCK:(j + 1) * C_BLOCK, :].astype(F32)
        o_ref[0, j:j + 1, :] = (jnp.sum(blk, axis=0, keepdims=True) * (1.0 / C_BLOCK)).astype(o_ref.dtype)


def _block_means(ck):
    bsz, s_len, w = ck.shape
    assert s_len // C_BLOCK <= BF16_ROWS
    return pl.pallas_call(
        _block_mean_kernel,
        grid=(bsz,),
        in_specs=[pl.BlockSpec((1, s_len, w), lambda b: (b, 0, 0))],
        out_specs=pl.BlockSpec((1, BF16_ROWS, w), lambda b: (b, 0, 0)),
        out_shape=jax.ShapeDtypeStruct((bsz, BF16_ROWS, w), BF),
        compiler_params=_cparams("parallel"),
        name="moba_block_means",
    )(ck)


def _moba_kernel(q_ref, k_ref, vt_ref, kmean_ref, bias_ref, o_ref, *, cur, s_len):
    tq = q_ref.shape[1]
    t0 = (cur * TILES_PER_GROUP + pl.program_id(0)) * tq
    q = q_ref[0]
    kmean = kmean_ref[0]
    blk = lax.broadcasted_iota(jnp.int32, (BF16_ROWS, tq), 0)
    outs = []
    for p in range(C_HEADS // 2):
        ps = slice(p * LANES, (p + 1) * LANES)
        rhs = _head_pair_rhs(q[:, ps])
        logits = _dot_nt(k_ref[0, :, ps], rhs)
        gates = _dot_nt(kmean[:, ps], rhs)
        for half in range(2):
            h = 2 * p + half
            hq = slice(half * tq, (half + 1) * tq)
            if cur > C_TOPK:
                gate = jnp.where(blk < cur, gates[:, hq], -jnp.inf)
                off = jnp.where(blk == cur, 0.0, NEG)
                for j in range(cur):
                    gj = gate[j:j + 1, :]
                    beats = (gate > gj) | ((gate == gj) & (blk < j))
                    rank = jnp.sum(jnp.where(beats, 1.0, 0.0), axis=0, keepdims=True)
                    off = jnp.where((blk == j) & (rank < float(C_TOPK)), 0.0, off)
            pieces = []
            for j in range(cur + 1):
                ks = slice(j * C_BLOCK, (j + 1) * C_BLOCK)
                piece = logits[ks, hq] + _causal_bias(bias_ref, h, t0, j * C_BLOCK, C_BLOCK, tq, s_len)
                if cur > C_TOPK:
                    piece = piece + off[j:j + 1, :]
                pieces.append(piece)
            lg = jnp.concatenate(pieces, axis=0)
            outs.append(_softmax_pv(lg, _with_ones_rows(vt_ref[0, h * HEAD_DIM:(h + 1) * HEAD_DIM, :])))
    o_ref[0] = jnp.concatenate(outs, axis=0).T.astype(o_ref.dtype)


def _moba(cq, ck, cv_t, bias_master):
    bsz, s_len, w = cq.shape
    assert TILES_PER_GROUP * TQ == C_BLOCK
    kmean = _block_means(ck)

    def call_group(g, n_keys):
        tile0 = g * TILES_PER_GROUP
        return pl.pallas_call(
            functools.partial(_moba_kernel, cur=g, s_len=s_len),
            grid=(TILES_PER_GROUP, bsz),
            in_specs=[pl.BlockSpec((1, TQ, w), lambda i, b: (b, tile0 + i, 0)),
                      pl.BlockSpec((1, n_keys, w), lambda i, b: (b, 0, 0)),
                      pl.BlockSpec((1, w, n_keys), lambda i, b: (b, 0, 0)),
                      pl.BlockSpec((1, BF16_ROWS, w), lambda i, b: (b, 0, 0)),
                      _const_spec2(bias_master.shape)],
            out_specs=pl.BlockSpec((1, TQ, w), lambda i, b: (b, i, 0)),
            out_shape=jax.ShapeDtypeStruct((bsz, C_BLOCK, w), BF),
            compiler_params=_cparams("arbitrary", "arbitrary"),
            name=f"moba_mixer_{g}",
        )(cq, ck, cv_t, kmean, bias_master)

    return _grouped_tiles(call_group, s_len)


def _layer_norm(y, g_ref, b_ref):
    mu = jnp.mean(y, axis=-1, keepdims=True)
    yc = y - mu
    var = jnp.mean(yc * yc, axis=-1, keepdims=True)
    return yc * lax.rsqrt(var + LN_EPS) * g_ref[...] + b_ref[...]


def _merge_kernel(x_ref, oa_ref, ob_ref, oc_ref, wg_ref, wa_ref, wb_ref, wc_ref, wo_ref, g_ref, b_ref, y_ref):
    x = x_ref[...]
    xb = x.astype(BF)
    merged = None
    o_b = jnp.concatenate([ob_ref[0, lt] for lt in range(ob_ref.shape[1])], axis=1).astype(BF)
    for n, (o, w_ref) in enumerate(((oa_ref[...], wa_ref), (o_b, wb_ref), (oc_ref[...], wc_ref))):
        gate = jax.nn.sigmoid(_dot(xb, wg_ref[:, n * D_MODEL:(n + 1) * D_MODEL]))
        term = gate * _dot(o, w_ref[...])
        merged = term if merged is None else merged + term
    y = ALPHA * x + _dot(merged.astype(BF), wo_ref[...])
    y_ref[...] = _layer_norm(y, g_ref, b_ref)


def _const_spec(shape):
    return pl.BlockSpec(shape, lambda i: (0,) * len(shape), pipeline_mode=pl.Buffered(1))


def _merge(x2d, oa, ob, oc, wg, wa, wb, wc, wo, ln_g, ln_b, tm):
    n = x2d.shape[0]
    tiles_per_seq = ob.shape[2] // tm
    rows = lambda w: pl.BlockSpec((tm, w), lambda i: (i, 0))
    ob_spec = pl.BlockSpec((1, ob.shape[1], tm, LANES), lambda i: (i // tiles_per_seq, 0, i % tiles_per_seq, 0))
    return pl.pallas_call(
        _merge_kernel,
        grid=(n // tm,),
        in_specs=[rows(D_MODEL), rows(oa.shape[1]), ob_spec, rows(oc.shape[1]),
                  _const_spec(wg.shape), _const_spec(wa.shape), _const_spec(wb.shape), _const_spec(wc.shape),
                  _const_spec(wo.shape), _const_spec(ln_g.shape), _const_spec(ln_b.shape)],
        out_specs=rows(D_MODEL),
        out_shape=jax.ShapeDtypeStruct((n, D_MODEL), F32),
        compiler_params=_cparams("parallel"),
        name="merge_out_ln",
    )(x2d, oa, ob, oc, wg, wa, wb, wc, wo, ln_g, ln_b)


_FF_CHUNK = 1024


def _ffn_kernel(x_ref, p_ref, wu_ref, wd_ref, wpg_ref, wp_ref, g_ref, b_ref, y_ref):
    x = x_ref[...]
    xb = x.astype(BF)
    y = ALPHA * x + jax.nn.sigmoid(_dot(xb, wpg_ref[...])) * _dot(p_ref[...].astype(BF), wp_ref[...])
    for c in range(D_FF // _FF_CHUNK):
        cs = slice(c * _FF_CHUNK, (c + 1) * _FF_CHUNK)
        u = jnp.maximum(_dot(xb, wu_ref[:, cs]), 0.0)
        y = y + _dot((u * u).astype(BF), wd_ref[cs, :])
    y_ref[...] = _layer_norm(y, g_ref, b_ref)


def _ffn(x2d, p2d, wu, wd, wpg, wp, ln_g, ln_b, tm):
    n = x2d.shape[0]
    rows = lambda w: pl.BlockSpec((tm, w), lambda i: (i, 0))
    return pl.pallas_call(
        _ffn_kernel,
        grid=(n // tm,),
        in_specs=[rows(D_MODEL), rows(PLE_DIM), _const_spec(wu.shape), _const_spec(wd.shape),
                  _const_spec(wpg.shape), _const_spec(wp.shape), _const_spec(ln_g.shape), _const_spec(ln_b.shape)],
        out_specs=rows(D_MODEL),
        out_shape=jax.ShapeDtypeStruct((n, D_MODEL), F32),
        compiler_params=_cparams("parallel"),
        name="ffn_ple_ln",
    )(x2d, p2d, wu, wd, wpg, wp, ln_g, ln_b)


def kernel(x, p, w_in, w_gate, w_br_a, w_br_b, w_br_c, w_out, ln1_g, ln1_b,
           w_up, w_down, w_ple_gate, w_ple, ln2_g, ln2_b, rel_bias):
    bsz, s_len, d_model = x.shape
    assert d_model == D_MODEL and s_len == MAX_DISTANCE, (x.shape,)
    n_tok = bsz * s_len
    tm = 512

    b_head0 = A_HEADS
    c_head0 = A_HEADS + B_GROUPS * B_SLOTS
    bias_a = _causal_bias_master(rel_bias, 0, A_HEADS, s_len)
    bias_b = _band_bias_tiles(rel_bias, b_head0)
    bias_c = _causal_bias_master(rel_bias, c_head0, C_HEADS, s_len)

    x2d = x.reshape(n_tok, D_MODEL)
    for i in range(DEPTH):
        pr = _project(x2d, *_pack_w_in(w_in[i]), tm, s_len)
        seq = lambda name: pr[name].reshape(bsz, s_len, -1)
        o_a = _dsa(seq("aq"), seq("iq"), pr["iwT"], seq("akk"), seq("aii"), pr["avT"], bias_a)
        o_b = _dilated((pr["bq0"], pr["bq1"], pr["bq2"]), pr["bk"], pr["bv"], bias_b)
        o_c = _moba(seq("cq"), seq("ck"), pr["cvT"], bias_c)
        flat = lambda a: a.reshape(n_tok, -1)
        row = lambda a: a.reshape(1, D_MODEL)
        x2d = _merge(x2d, flat(o_a), o_b, flat(o_c), w_gate[i].astype(BF), w_br_a[i].astype(BF),
                     w_br_b[i].astype(BF), w_br_c[i].astype(BF), w_out[i].astype(BF),
                     row(ln1_g[i]), row(ln1_b[i]), tm)
        x2d = _ffn(x2d, p[i].reshape(n_tok, PLE_DIM), w_up[i].astype(BF), w_down[i].astype(BF),
                   w_ple_gate[i].astype(BF), w_ple[i].astype(BF), row(ln2_g[i]), row(ln2_b[i]), tm)
    return x2d.reshape(bsz, s_len, D_MODEL)
```

```python
import functools
import math

import numpy as np
import jax
import jax.numpy as jnp
from jax import lax
from jax.experimental import pallas as pl
from jax.experimental.pallas import tpu as pltpu

D_MODEL = 1024
HEAD_DIM = 64
A_HEADS = 6
IDX_HEADS = 8
A_TOPK_MAX = 256
B_SLOTS = 4
B_PATTERNS = ((128, 1), (512, 4), (2048, 16))
B_GROUPS = 3
C_HEADS = 6
C_BLOCK = 256
C_TOPK = 3
N_BUCKETS = 32
MAX_DISTANCE = 2048
D_FF = 4 * D_MODEL
PLE_DIM = 256
DEPTH = 2
ALPHA = (2 * DEPTH) ** 0.25
LN_EPS = 1e-5
NEG = -1e30
QK_SCALE = HEAD_DIM ** -0.5

LANES = 128
SUBLANES = 8
BF16_ROWS = 16
VMEM_LIMIT = 56 * 1024 * 1024

TQ = 256
BIAS_TQ = LANES
TILES_PER_GROUP = C_BLOCK // TQ
BAND = 128
BF = jnp.bfloat16
F32 = jnp.float32
HALF_MIN = -2 ** 15

_NT = (((1,), (1,)), ((), ()))


def _dot(a, b):
    return jnp.dot(a, b, preferred_element_type=F32)


def _dot_nt(a, b):
    return lax.dot_general(a, b, _NT, preferred_element_type=F32)


def _cparams(*sem):
    return pltpu.CompilerParams(dimension_semantics=sem, vmem_limit_bytes=VMEM_LIMIT)


def _bucket_starts():
    d = np.arange(0, MAX_DISTANCE + 1)
    max_exact = N_BUCKETS // 2
    nf = np.maximum(d, 1).astype(np.float32)
    large = max_exact + (np.log(nf / np.float32(max_exact)) / np.float32(math.log(MAX_DISTANCE / max_exact))
                         * np.float32(N_BUCKETS - max_exact)).astype(np.int32)
    bucket = np.where(d < max_exact, d, np.minimum(large, N_BUCKETS - 1))
    return [int(np.argmax(bucket >= b)) if np.any(bucket >= b) else None for b in range(N_BUCKETS)]


_BUCKET_START = _bucket_starts()


def _bias_from_distance(dist, tab_ref, col):
    val = jnp.full(dist.shape, tab_ref[0, col], F32)
    for b in range(1, N_BUCKETS):
        if _BUCKET_START[b] is not None:
            val = jnp.where(dist >= _BUCKET_START[b], tab_ref[b, col], val)
    return jnp.where(dist < 0, NEG, val)


def _causal_bias_kernel(tab_ref, o_ref, *, head0):
    h = pl.program_id(0)
    n_rows, tq = o_ref.shape[1], o_ref.shape[2]
    u = lax.broadcasted_iota(jnp.int32, (n_rows, tq), 0)
    t = lax.broadcasted_iota(jnp.int32, (n_rows, tq), 1)
    o_ref[0] = _bias_from_distance(t + (n_rows - tq) // 2 - u, tab_ref, head0 + h)


def _causal_bias_master(rel_bias, head0, n_heads, s_len):
    n_rows = 2 * s_len - BIAS_TQ
    return pl.pallas_call(
        functools.partial(_causal_bias_kernel, head0=head0),
        grid=(n_heads,),
        in_specs=[pl.BlockSpec(memory_space=pltpu.SMEM)],
        out_specs=pl.BlockSpec((1, n_rows, BIAS_TQ), lambda h: (h, 0, 0)),
        out_shape=jax.ShapeDtypeStruct((n_heads, n_rows, BIAS_TQ), F32),
        compiler_params=_cparams("arbitrary"),
        name="causal_bias_master",
    )(rel_bias)


def _bias_row0(t0, s_len):
    return pl.multiple_of(s_len - BIAS_TQ - t0, BIAS_TQ)


def _causal_bias(bias_ref, h, t0, first_key, n_keys, tq, s_len):
    parts = [bias_ref[h, pl.ds(_bias_row0(t0 + c * BIAS_TQ, s_len) + first_key, n_keys), :]
             for c in range(tq // BIAS_TQ)]
    return parts[0] if len(parts) == 1 else jnp.concatenate(parts, axis=1)


def _band_bias_kernel(tab_ref, o_ref, *, head0):
    g = pl.program_id(0)
    variant = pl.program_id(1)
    pair = pl.program_id(2)
    row = lax.broadcasted_iota(jnp.int32, (BAND, 2 * BAND), 0)
    col = lax.broadcasted_iota(jnp.int32, (BAND, 2 * BAND), 1)
    j = row + BAND - col
    no_prev = (variant == 1) & (col < BAND)
    for gi, (_, dil) in enumerate(B_PATTERNS):
        @pl.when(g == gi)
        def _(dil=dil):
            for half in range(2):
                bias = _bias_from_distance(j * dil, tab_ref, head0 + g * B_SLOTS + 2 * pair + half)
                bias = jnp.where((j > BAND) | no_prev, NEG, bias)
                o_ref[0, 0, 0, :, half * 2 * BAND:(half + 1) * 2 * BAND] = bias


def _band_bias_tiles(rel_bias, head0):
    return pl.pallas_call(
        functools.partial(_band_bias_kernel, head0=head0),
        grid=(B_GROUPS, 2, B_SLOTS // 2),
        in_specs=[pl.BlockSpec(memory_space=pltpu.SMEM)],
        out_specs=pl.BlockSpec((1, 1, 1, BAND, 4 * BAND), lambda g, v, p: (g, v, p, 0, 0)),
        out_shape=jax.ShapeDtypeStruct((B_GROUPS, 2, B_SLOTS // 2, BAND, 4 * BAND), F32),
        compiler_params=_cparams("arbitrary", "arbitrary", "arbitrary"),
        name="band_bias_tiles",
    )(rel_bias)


_PROJ_OUTS = (
    ("aq", A_HEADS * HEAD_DIM, QK_SCALE),
    ("akk", 2 * HEAD_DIM, 1.0),
    ("aii", 2 * HEAD_DIM, 1.0),
    ("iq", IDX_HEADS * HEAD_DIM, QK_SCALE),
    ("bq0", B_SLOTS * HEAD_DIM, QK_SCALE),
    ("bq1", B_SLOTS * HEAD_DIM, QK_SCALE),
    ("bq2", B_SLOTS * HEAD_DIM, QK_SCALE),
    ("bk", B_SLOTS * HEAD_DIM, 1.0),
    ("bv", B_SLOTS * HEAD_DIM, 1.0),
    ("cq", C_HEADS * HEAD_DIM, QK_SCALE),
    ("ck", C_HEADS * HEAD_DIM, 1.0),
)
_PROJ_WIDTH = sum(w for _, w, _ in _PROJ_OUTS)
_PROJ_LANE_TILED = ("bq0", "bq1", "bq2", "bk", "bv")
_PROJ_OUTS_T = (
    ("avT", HEAD_DIM, BF),
    ("cvT", C_HEADS * HEAD_DIM, BF),
    ("iwT", BF16_ROWS, F32),
)
_PROJ_ROWS_T = sum(r for _, r, _ in _PROJ_OUTS_T)


def _pack_w_in(w):
    widths = (384, 64, 64, 512, 64, 8, 768, 256, 256, 384, 384, 384)
    offs = np.concatenate([[0], np.cumsum(widths)])
    aq, ak, av, iq, ik, iw, bq, bk, bv, cq, ck, cv = (w[:, offs[n]:offs[n + 1]] for n in range(12))
    cols = jnp.concatenate([aq, ak, ak, ik, ik, iq, bq, bk, bv, cq, ck], axis=1)
    iw_pad = jnp.concatenate([iw, jnp.zeros((w.shape[0], BF16_ROWS - IDX_HEADS), w.dtype)], axis=1)
    rows = jnp.concatenate([av, cv, iw_pad], axis=1).T
    return cols.astype(BF), rows.astype(BF)


def _proj_kernel(x_ref, w_ref, wt_ref, *o_refs):
    xb = x_ref[...].astype(BF)
    off = 0
    for o_ref, (_, width, scale) in zip(o_refs, _PROJ_OUTS):
        res = _dot(xb, w_ref[:, off:off + width])
        if scale != 1.0:
            res = res * scale
        if len(o_ref.shape) == 4:
            for lt in range(width // LANES):
                o_ref[0, lt] = res[:, lt * LANES:(lt + 1) * LANES]
        else:
            o_ref[...] = res.astype(o_ref.dtype)
        off += width
    off = 0
    for o_ref, (_, rows, _) in zip(o_refs[len(_PROJ_OUTS):], _PROJ_OUTS_T):
        o_ref[0] = _dot_nt(wt_ref[off:off + rows, :], xb).astype(o_ref.dtype)
        off += rows


def _project(x2d, w_cols, w_rows, tm, s_len):
    n = x2d.shape[0]
    tiles_per_seq = s_len // tm
    out_shape, out_specs = [], []
    for name, w, _ in _PROJ_OUTS:
        if name in _PROJ_LANE_TILED:
            out_shape.append(jax.ShapeDtypeStruct((n // s_len, w // LANES, s_len, LANES), F32))
            out_specs.append(pl.BlockSpec((1, w // LANES, tm, LANES),
                                          lambda i: (i // tiles_per_seq, 0, i % tiles_per_seq, 0)))
        else:
            out_shape.append(jax.ShapeDtypeStruct((n, w), BF))
            out_specs.append(pl.BlockSpec((tm, w), lambda i: (i, 0)))
    out_shape += [jax.ShapeDtypeStruct((n // s_len, r, s_len), dt) for _, r, dt in _PROJ_OUTS_T]
    out_specs += [pl.BlockSpec((1, r, tm), lambda i: (i // tiles_per_seq, 0, i % tiles_per_seq))
                  for _, r, _ in _PROJ_OUTS_T]
    outs = pl.pallas_call(
        _proj_kernel,
        grid=(n // tm,),
        in_specs=[pl.BlockSpec((tm, D_MODEL), lambda i: (i, 0)),
                  pl.BlockSpec((D_MODEL, _PROJ_WIDTH), lambda i: (0, 0)),
                  pl.BlockSpec((_PROJ_ROWS_T, D_MODEL), lambda i: (0, 0))],
        out_specs=out_specs,
        out_shape=out_shape,
        compiler_params=_cparams("parallel"),
        name="in_proj",
    )(x2d, w_cols, w_rows)
    names = [name for name, _, _ in _PROJ_OUTS] + [name for name, _, _ in _PROJ_OUTS_T]
    return dict(zip(names, outs))


def _head_pair_rhs(pair):
    lane = lax.broadcasted_iota(jnp.int32, pair.shape, 1)
    zero = jnp.zeros_like(pair)
    return jnp.concatenate([jnp.where(lane < HEAD_DIM, pair, zero), jnp.where(lane < HEAD_DIM, zero, pair)], axis=0)


ROW_CHUNK = 8 * SUBLANES


def _max_rows(x):
    acc = x[:ROW_CHUNK]
    for c in range(1, x.shape[0] // ROW_CHUNK):
        acc = jnp.maximum(acc, x[c * ROW_CHUNK:(c + 1) * ROW_CHUNK])
    return jnp.max(acc, axis=0, keepdims=True)


def _softmax_pv(logits, v_t_ones):
    m = _max_rows(logits)
    e = jnp.exp(logits - m).astype(BF)
    o = _dot(v_t_ones, e)
    return o[:HEAD_DIM] / o[HEAD_DIM:HEAD_DIM + 1]


def _with_ones_rows(v_t):
    return jnp.concatenate([v_t, jnp.ones((BF16_ROWS, v_t.shape[1]), v_t.dtype)], axis=0)


def _const_spec2(shape):
    return pl.BlockSpec(shape, lambda i, b: (0,) * len(shape), pipeline_mode=pl.Buffered(1))


def _grouped_tiles(call_group, out_shape):
    s_len = out_shape.shape[1]
    out = jnp.zeros(out_shape.shape, out_shape.dtype)
    for g in range(s_len // (TILES_PER_GROUP * TQ)):
        out = call_group(g, (g + 1) * TILES_PER_GROUP * TQ, out)
    return out


def _dsa_kernel(q_ref, iq_ref, iw_ref, kk_ref, ii_ref, vt_ref, bias_ref, _, o_ref, key_ref, half_ref, mask_ref, *,
                topk, tile0, s_len):
    i = tile0 + pl.program_id(0)
    n_keys, tq = key_ref.shape
    pos = lax.broadcasted_iota(jnp.int32, (n_keys, tq), 0)
    qry = i * tq + lax.broadcasted_iota(jnp.int32, (n_keys, tq), 1)

    if n_keys > topk:
        iq = iq_ref[0]
        iw = iw_ref[0]
        ii = ii_ref[0]
        index = jnp.zeros((n_keys, tq), F32)
        for p in range(IDX_HEADS // 2):
            sc = _dot_nt(ii, _head_pair_rhs(iq[:, p * LANES:(p + 1) * LANES]))
            index = index + iw[2 * p:2 * p + 1, :] * jnp.maximum(sc[:, :tq], 0.0)
            index = index + iw[2 * p + 1:2 * p + 2, :] * jnp.maximum(sc[:, tq:], 0.0)
        index = jnp.where(pos <= qry, index + 0.0, -jnp.inf)
        bits = pltpu.bitcast(index, jnp.int32)
        key_ref[...] = jnp.where(bits < 0, bits ^ jnp.int32(0x7FFFFFFF), bits)

        chunk_pos = lax.broadcasted_iota(jnp.int32, (ROW_CHUNK, tq), 0)

        def count(pred):
            acc = None
            for c in range(n_keys // ROW_CHUNK):
                hit = pred(key_ref[c * ROW_CHUNK:(c + 1) * ROW_CHUNK, :], chunk_pos + c * ROW_CHUNK)
                ones = jnp.where(hit, 1.0, 0.0)
                acc = ones if acc is None else acc + ones
            return jnp.sum(acc, axis=0, keepdims=True)

        def count_half(pred):
            acc = None
            for c in range(n_keys // (2 * ROW_CHUNK)):
                hit = pred(half_ref[c * 2 * ROW_CHUNK:(c + 1) * 2 * ROW_CHUNK, :])
                ones = jnp.where(hit, jnp.int16(1), jnp.int16(0))
                acc = ones if acc is None else acc + ones
            return jnp.sum(acc.astype(F32), axis=0, keepdims=True)

        def search_half(need, count_at_min):
            c0 = count_half(lambda h: h >= 0)
            t0 = jnp.where(c0 >= need, 0, HALF_MIN).astype(jnp.int32)
            n0 = jnp.where(c0 >= need, c0, count_at_min)

            def step(it, carry):
                t, n = carry
                cand = t | (jnp.int32(1) << (14 - it))
                cand16 = cand.astype(jnp.int16)
                c = count_half(lambda h: h >= cand16)
                ok = c >= need
                return jnp.where(ok, cand, t), jnp.where(ok, c, n)

            return lax.fori_loop(0, 15, step, (t0, n0))

        k_f = float(topk)
        half_ref[...] = (key_ref[...] >> 16).astype(jnp.int16)
        t_hi, cnt_ge_hi = search_half(k_f, float(n_keys))
        t_hi16 = t_hi.astype(jnp.int16)
        cnt_gt_hi = count_half(lambda h: h > t_hi16)
        key = key_ref[...]
        low = (key & 0xFFFF) + HALF_MIN
        half_ref[...] = jnp.where((key >> 16) == t_hi, low, HALF_MIN).astype(jnp.int16)
        t_lo, cnt_lo = search_half(k_f - cnt_gt_hi, cnt_ge_hi - cnt_gt_hi)
        thr = t_hi * 65536 + (t_lo - HALF_MIN)
        cnt_ge = cnt_gt_hi + cnt_lo

        def tie_search():
            remaining = k_f - count(lambda k, _: k > thr)
            n_bits = (n_keys - 1).bit_length()

            def tie_step(it, last):
                cand = last | (jnp.int32(1) << (n_bits - 1 - it))
                c = count(lambda k, kpos: (k == thr) & (kpos < cand))
                return jnp.where(c < remaining, cand, last)

            return lax.fori_loop(0, n_bits, tie_step, jnp.zeros((1, tq), jnp.int32))

        has_ties = jnp.max(cnt_ge) > k_f

        @pl.when(has_ties)
        def _():
            last_tie = tie_search()
            key = key_ref[...]
            selected = (key > thr) | ((key == thr) & (pos <= last_tie))
            mask_ref[...] = jnp.where(selected, 0.0, NEG)

        @pl.when(jnp.logical_not(has_ties))
        def _():
            mask_ref[...] = jnp.where(key_ref[...] >= thr, 0.0, NEG)
    else:
        mask_ref[...] = jnp.zeros_like(mask_ref)

    q = q_ref[0]
    kk = kk_ref[0]
    v_t = _with_ones_rows(vt_ref[0])
    outs = []
    for p in range(A_HEADS // 2):
        logits = _dot_nt(kk, _head_pair_rhs(q[:, p * LANES:(p + 1) * LANES]))
        for half in range(2):
            bias = _causal_bias(bias_ref, 2 * p + half, i * tq, 0, n_keys, tq, s_len)
            lg = logits[:, half * tq:(half + 1) * tq] + bias + mask_ref[...]
            outs.append(_softmax_pv(lg, v_t))
    o_ref[0] = jnp.concatenate(outs, axis=0).T.astype(o_ref.dtype)


def _dsa(aq, iq, iw_t, akk, aii, av_t, bias_master):
    bsz, s_len, _ = aq.shape
    topk = min(A_TOPK_MAX, s_len // 4)
    out_shape = jax.ShapeDtypeStruct(aq.shape, BF)

    def call_group(g, n_keys, out):
        tile0 = g * TILES_PER_GROUP
        q_spec = lambda w: pl.BlockSpec((1, TQ, w), lambda i, b: (b, tile0 + i, 0))
        keys = lambda w: pl.BlockSpec((1, n_keys, w), lambda i, b: (b, 0, 0))
        return pl.pallas_call(
            functools.partial(_dsa_kernel, topk=topk, tile0=tile0, s_len=s_len),
            grid=(TILES_PER_GROUP, bsz),
            in_specs=[q_spec(A_HEADS * HEAD_DIM), q_spec(IDX_HEADS * HEAD_DIM),
                      pl.BlockSpec((1, BF16_ROWS, TQ), lambda i, b: (b, 0, tile0 + i)),
                      keys(2 * HEAD_DIM), keys(2 * HEAD_DIM),
                      pl.BlockSpec((1, HEAD_DIM, n_keys), lambda i, b: (b, 0, 0)),
                      _const_spec2(bias_master.shape), pl.BlockSpec(memory_space=pl.ANY)],
            out_specs=q_spec(A_HEADS * HEAD_DIM),
            out_shape=out_shape,
            input_output_aliases={7: 0},
            scratch_shapes=[pltpu.VMEM((n_keys, TQ), jnp.int32), pltpu.VMEM((n_keys, TQ), jnp.int16),
                            pltpu.VMEM((n_keys, TQ), F32)],
            compiler_params=_cparams("arbitrary", "arbitrary"),
            name=f"dsa_mixer_{g}",
        )(aq, iq, iw_t, akk, aii, av_t, bias_master, out)

    return _grouped_tiles(call_group, out_shape)


_PAIR_W = 2 * HEAD_DIM
_N_PAIRS = B_SLOTS // 2
_N_STATE = 2 * _N_PAIRS


def _class_rows(r, t, dil):
    start = r + t * BAND * dil
    return pl.ds(start, BAND, stride=dil) if dil > 1 else pl.ds(start, BAND)


def _pair_ones(n_keys):
    row = lax.broadcasted_iota(jnp.int32, (2 * n_keys, _PAIR_W), 0)
    lane = lax.broadcasted_iota(jnp.int32, (2 * n_keys, _PAIR_W), 1)
    return jnp.where((row < n_keys) == (lane < HEAD_DIM), 1.0, 0.0).astype(BF)


def _band_pairs(units, ones_bd):
    n_keys = units[0][1].shape[0]
    lane = lax.broadcasted_iota(jnp.int32, (BAND, _PAIR_W), 1)
    logits = [_dot_nt(q, _head_pair_rhs(k)) + bias for q, k, _, bias in units]
    maxes = [(jnp.max(lg[:, :n_keys], axis=1, keepdims=True), jnp.max(lg[:, n_keys:], axis=1, keepdims=True))
             for lg in logits]
    probs = [jnp.concatenate([jnp.exp(lg[:, :n_keys] - ma), jnp.exp(lg[:, n_keys:] - mb)], axis=1).astype(BF)
             for lg, (ma, mb) in zip(logits, maxes)]
    results = []
    for (_, _, v, _), e, (ma, mb) in zip(units, probs, maxes):
        acc = _dot(e, _head_pair_rhs(v))
        den = _dot(e, ones_bd)
        results.append((acc / den, jnp.where(lane < HEAD_DIM, ma, mb) + jnp.log(den)))
    return results


def _band_state_kernel(q_ref, k_ref, v_ref, kp_ref, vp_ref, bias_ref, st_ref, *, dil):
    c = pl.program_id(1)
    n_tiles = q_ref.shape[2] // (BAND * dil)
    first_variant = jnp.where(c == 0, 1, 0)
    load = lambda ref, p, rows: ref[0, p, rows, :].astype(BF)
    units, where = [], []
    for r in range(dil):
        for t in range(n_tiles):
            rows = _class_rows(r, t, dil)
            for p in range(_N_PAIRS):
                if t == 0:
                    prev = _class_rows(r, 0, dil)
                    k = jnp.concatenate([load(kp_ref, p, prev), load(k_ref, p, rows)], axis=0)
                    v = jnp.concatenate([load(vp_ref, p, prev), load(v_ref, p, rows)], axis=0)
                    bias = bias_ref[0, first_variant, p]
                else:
                    prev = _class_rows(r, t - 1, dil)
                    k = jnp.concatenate([load(k_ref, p, prev), load(k_ref, p, rows)], axis=0)
                    v = jnp.concatenate([load(v_ref, p, prev), load(v_ref, p, rows)], axis=0)
                    bias = bias_ref[0, 0, p]
                units.append((load(q_ref, p, rows), k, v, bias))
                where.append((rows, p))
    for (rows, p), (out, lse) in zip(where, _band_pairs(units, _pair_ones(2 * BAND))):
        st_ref[0, p, rows, :] = out
        st_ref[0, _N_PAIRS + p, rows, :] = lse


def _band_state(q, k, v, bias_tiles, g, dil):
    bsz, n_lt, s_len, _ = q.shape
    rows = 4 * BAND
    prev_rows = BAND * dil
    per = rows // prev_rows
    main = pl.BlockSpec((1, n_lt, rows, LANES), lambda b, c: (b, 0, c, 0))
    prev = pl.BlockSpec((1, n_lt, prev_rows, LANES), lambda b, c: (b, 0, jnp.maximum(c * per - 1, 0), 0))
    return pl.pallas_call(
        functools.partial(_band_state_kernel, dil=dil),
        grid=(bsz, s_len // rows),
        in_specs=[main, main, main, prev, prev,
                  pl.BlockSpec((1, 2, _N_PAIRS, BAND, 4 * BAND), lambda b, c: (g, 0, 0, 0, 0))],
        out_specs=pl.BlockSpec((1, _N_STATE, rows, LANES), lambda b, c: (b, 0, c, 0)),
        out_shape=jax.ShapeDtypeStruct((bsz, _N_STATE, s_len, LANES), F32),
        compiler_params=_cparams("parallel", "arbitrary"),
        name=f"band_state_{g}",
    )(q, k, v, k, v, bias_tiles)


def _band_merge_kernel(q_ref, k_ref, v_ref, st1_ref, st2_ref, bias_ref, o_ref, *, dil):
    load = lambda ref, p, rows: ref[0, p, rows, :].astype(BF)
    own_bias = [jnp.concatenate([bias_ref[0, 0, p, :, BAND:2 * BAND], bias_ref[0, 0, p, :, 3 * BAND:]], axis=1)
                for p in range(_N_PAIRS)]
    units, where = [], []
    for r in range(dil):
        rows = _class_rows(r, 0, dil)
        for p in range(_N_PAIRS):
            units.append((load(q_ref, p, rows), load(k_ref, p, rows), load(v_ref, p, rows), own_bias[p]))
            where.append((rows, p))
    for (rows, p), (out3, lse3) in zip(where, _band_pairs(units, _pair_ones(BAND))):
        lse1, lse2 = st1_ref[0, _N_PAIRS + p, rows, :], st2_ref[0, _N_PAIRS + p, rows, :]
        top = jnp.maximum(jnp.maximum(lse1, lse2), lse3)
        w1, w2, w3 = jnp.exp(lse1 - top), jnp.exp(lse2 - top), jnp.exp(lse3 - top)
        num = w1 * st1_ref[0, p, rows, :] + w2 * st2_ref[0, p, rows, :] + w3 * out3
        o_ref[0, p, rows, :] = num / (w1 + w2 + w3)


def _band_merge(q, k, v, st1, st2, bias_tiles, g, dil):
    bsz, n_lt, s_len, _ = q.shape
    assert s_len == BAND * dil
    full = lambda n: pl.BlockSpec((1, n, s_len, LANES), lambda b: (b, 0, 0, 0))
    return pl.pallas_call(
        functools.partial(_band_merge_kernel, dil=dil),
        grid=(bsz,),
        in_specs=[full(n_lt), full(n_lt), full(n_lt), full(_N_STATE), full(_N_STATE),
                  pl.BlockSpec((1, 2, _N_PAIRS, BAND, 4 * BAND), lambda b: (g, 0, 0, 0, 0))],
        out_specs=full(n_lt),
        out_shape=jax.ShapeDtypeStruct((bsz, n_lt, s_len, LANES), F32),
        compiler_params=_cparams("parallel"),
        name="band_merge",
    )(q, k, v, st1, st2, bias_tiles)


def _dilated(bq, bk, bv, bias_tiles):
    (_, d0), (_, d1), (_, d2) = B_PATTERNS
    st0 = _band_state(bq[0], bk, bv, bias_tiles, 0, d0)
    st1 = _band_state(bq[1], bk, bv, bias_tiles, 1, d1)
    return _band_merge(bq[2], bk, bv, st0, st1, bias_tiles, 2, d2)


def _block_mean_kernel(k_ref, o_ref):
    n_blk = k_ref.shape[1] // C_BLOCK
    o_ref[...] = jnp.zeros_like(o_ref)
    for j in range(n_blk):
        blk = k_ref[0, j * C_BLOCK:(j + 1) * C_BLOCK, :].astype(F32)
        o_ref[0, j:j + 1, :] = (jnp.sum(blk, axis=0, keepdims=True) * (1.0 / C_BLOCK)).astype(o_ref.dtype)


def _block_means(ck):
    bsz, s_len, w = ck.shape
    assert s_len // C_BLOCK <= BF16_ROWS
    return pl.pallas_call(
        _block_mean_kernel,
        grid=(bsz,),
        in_specs=[pl.BlockSpec((1, s_len, w), lambda b: (b, 0, 0))],
        out_specs=pl.BlockSpec((1, BF16_ROWS, w), lambda b: (b, 0, 0)),
        out_shape=jax.ShapeDtypeStruct((bsz, BF16_ROWS, w), BF),
        compiler_params=_cparams("parallel"),
        name="moba_block_means",
    )(ck)


def _moba_kernel(q_ref, k_ref, vt_ref, kmean_ref, bias_ref, _, o_ref, *, cur, s_len):
    tq = q_ref.shape[1]
    t0 = (cur * TILES_PER_GROUP + pl.program_id(0)) * tq
    q = q_ref[0]
    kmean = kmean_ref[0]
    blk = lax.broadcasted_iota(jnp.int32, (BF16_ROWS, tq), 0)
    outs = []
    for p in range(C_HEADS // 2):
        ps = slice(p * LANES, (p + 1) * LANES)
        rhs = _head_pair_rhs(q[:, ps])
        logits = _dot_nt(k_ref[0, :, ps], rhs)
        gates = _dot_nt(kmean[:, ps], rhs)
        for half in range(2):
            h = 2 * p + half
            hq = slice(half * tq, (half + 1) * tq)
            if cur > C_TOPK:
                gate = jnp.where(blk < cur, gates[:, hq], -jnp.inf)
                off = jnp.where(blk == cur, 0.0, NEG)
                for j in range(cur):
                    gj = gate[j:j + 1, :]
                    beats = (gate > gj) | ((gate == gj) & (blk < j))
                    rank = jnp.sum(jnp.where(beats, 1.0, 0.0), axis=0, keepdims=True)
                    off = jnp.where((blk == j) & (rank < float(C_TOPK)), 0.0, off)
            pieces = []
            for j in range(cur + 1):
                ks = slice(j * C_BLOCK, (j + 1) * C_BLOCK)
                piece = logits[ks, hq] + _causal_bias(bias_ref, h, t0, j * C_BLOCK, C_BLOCK, tq, s_len)
                if cur > C_TOPK:
                    piece = piece + off[j:j + 1, :]
                pieces.append(piece)
            lg = jnp.concatenate(pieces, axis=0)
            outs.append(_softmax_pv(lg, _with_ones_rows(vt_ref[0, h * HEAD_DIM:(h + 1) * HEAD_DIM, :])))
    o_ref[0] = jnp.concatenate(outs, axis=0).T.astype(o_ref.dtype)


def _moba(cq, ck, cv_t, bias_master):
    bsz, s_len, w = cq.shape
    assert TILES_PER_GROUP * TQ == C_BLOCK
    kmean = _block_means(ck)
    out_shape = jax.ShapeDtypeStruct(cq.shape, BF)

    def call_group(g, n_keys, out):
        tile0 = g * TILES_PER_GROUP
        return pl.pallas_call(
            functools.partial(_moba_kernel, cur=g, s_len=s_len),
            grid=(TILES_PER_GROUP, bsz),
            in_specs=[pl.BlockSpec((1, TQ, w), lambda i, b: (b, tile0 + i, 0)),
                      pl.BlockSpec((1, n_keys, w), lambda i, b: (b, 0, 0)),
                      pl.BlockSpec((1, w, n_keys), lambda i, b: (b, 0, 0)),
                      pl.BlockSpec((1, BF16_ROWS, w), lambda i, b: (b, 0, 0)),
                      _const_spec2(bias_master.shape), pl.BlockSpec(memory_space=pl.ANY)],
            out_specs=pl.BlockSpec((1, TQ, w), lambda i, b: (b, tile0 + i, 0)),
            out_shape=out_shape,
            input_output_aliases={5: 0},
            compiler_params=_cparams("arbitrary", "arbitrary"),
            name=f"moba_mixer_{g}",
        )(cq, ck, cv_t, kmean, bias_master, out)

    return _grouped_tiles(call_group, out_shape)


def _layer_norm(y, g_ref, b_ref):
    mu = jnp.mean(y, axis=-1, keepdims=True)
    yc = y - mu
    var = jnp.mean(yc * yc, axis=-1, keepdims=True)
    return yc * lax.rsqrt(var + LN_EPS) * g_ref[...] + b_ref[...]


def _merge_kernel(x_ref, oa_ref, ob_ref, oc_ref, wg_ref, wa_ref, wb_ref, wc_ref, wo_ref, g_ref, b_ref, y_ref):
    x = x_ref[...]
    xb = x.astype(BF)
    merged = None
    o_b = jnp.concatenate([ob_ref[0, lt] for lt in range(ob_ref.shape[1])], axis=1).astype(BF)
    for n, (o, w_ref) in enumerate(((oa_ref[...], wa_ref), (o_b, wb_ref), (oc_ref[...], wc_ref))):
        gate = jax.nn.sigmoid(_dot(xb, wg_ref[:, n * D_MODEL:(n + 1) * D_MODEL]))
        term = gate * _dot(o, w_ref[...])
        merged = term if merged is None else merged + term
    y = ALPHA * x + _dot(merged.astype(BF), wo_ref[...])
    y_ref[...] = _layer_norm(y, g_ref, b_ref)


def _const_spec(shape):
    return pl.BlockSpec(shape, lambda i: (0,) * len(shape), pipeline_mode=pl.Buffered(1))


def _merge(x2d, oa, ob, oc, wg, wa, wb, wc, wo, ln_g, ln_b, tm):
    n = x2d.shape[0]
    tiles_per_seq = ob.shape[2] // tm
    rows = lambda w: pl.BlockSpec((tm, w), lambda i: (i, 0))
    ob_spec = pl.BlockSpec((1, ob.shape[1], tm, LANES), lambda i: (i // tiles_per_seq, 0, i % tiles_per_seq, 0))
    return pl.pallas_call(
        _merge_kernel,
        grid=(n // tm,),
        in_specs=[rows(D_MODEL), rows(oa.shape[1]), ob_spec, rows(oc.shape[1]),
                  _const_spec(wg.shape), _const_spec(wa.shape), _const_spec(wb.shape), _const_spec(wc.shape),
                  _const_spec(wo.shape), _const_spec(ln_g.shape), _const_spec(ln_b.shape)],
        out_specs=rows(D_MODEL),
        out_shape=jax.ShapeDtypeStruct((n, D_MODEL), F32),
        compiler_params=_cparams("parallel"),
        name="merge_out_ln",
    )(x2d, oa, ob, oc, wg, wa, wb, wc, wo, ln_g, ln_b)


_FF_CHUNK = 1024


def _ffn_kernel(x_ref, p_ref, wu_ref, wd_ref, wpg_ref, wp_ref, g_ref, b_ref, y_ref):
    x = x_ref[...]
    xb = x.astype(BF)
    y = ALPHA * x + jax.nn.sigmoid(_dot(xb, wpg_ref[...])) * _dot(p_ref[...].astype(BF), wp_ref[...])
    for c in range(D_FF // _FF_CHUNK):
        cs = slice(c * _FF_CHUNK, (c + 1) * _FF_CHUNK)
        u = jnp.maximum(_dot(xb, wu_ref[:, cs]), 0.0)
        y = y + _dot((u * u).astype(BF), wd_ref[cs, :])
    y_ref[...] = _layer_norm(y, g_ref, b_ref)


def _ffn(x2d, p2d, wu, wd, wpg, wp, ln_g, ln_b, tm):
    n = x2d.shape[0]
    rows = lambda w: pl.BlockSpec((tm, w), lambda i: (i, 0))
    return pl.pallas_call(
        _ffn_kernel,
        grid=(n // tm,),
        in_specs=[rows(D_MODEL), rows(PLE_DIM), _const_spec(wu.shape), _const_spec(wd.shape),
                  _const_spec(wpg.shape), _const_spec(wp.shape), _const_spec(ln_g.shape), _const_spec(ln_b.shape)],
        out_specs=rows(D_MODEL),
        out_shape=jax.ShapeDtypeStruct((n, D_MODEL), F32),
        compiler_params=_cparams("parallel"),
        name="ffn_ple_ln",
    )(x2d, p2d, wu, wd, wpg, wp, ln_g, ln_b)


def kernel(x, p, w_in, w_gate, w_br_a, w_br_b, w_br_c, w_out, ln1_g, ln1_b,
           w_up, w_down, w_ple_gate, w_ple, ln2_g, ln2_b, rel_bias):
    bsz, s_len, d_model = x.shape
    assert d_model == D_MODEL and s_len == MAX_DISTANCE, (x.shape,)
    n_tok = bsz * s_len
    tm = 512

    b_head0 = A_HEADS
    c_head0 = A_HEADS + B_GROUPS * B_SLOTS
    bias_a = _causal_bias_master(rel_bias, 0, A_HEADS, s_len)
    bias_b = _band_bias_tiles(rel_bias, b_head0)
    bias_c = _causal_bias_master(rel_bias, c_head0, C_HEADS, s_len)

    x2d = x.reshape(n_tok, D_MODEL)
    for i in range(DEPTH):
        pr = _project(x2d, *_pack_w_in(w_in[i]), tm, s_len)
        seq = lambda name: pr[name].reshape(bsz, s_len, -1)
        o_a = _dsa(seq("aq"), seq("iq"), pr["iwT"], seq("akk"), seq("aii"), pr["avT"], bias_a)
        o_b = _dilated((pr["bq0"], pr["bq1"], pr["bq2"]), pr["bk"], pr["bv"], bias_b)
        o_c = _moba(seq("cq"), seq("ck"), pr["cvT"], bias_c)
        flat = lambda a: a.reshape(n_tok, -1)
        row = lambda a: a.reshape(1, D_MODEL)
        x2d = _merge(x2d, flat(o_a), o_b, flat(o_c), w_gate[i].astype(BF), w_br_a[i].astype(BF),
                     w_br_b[i].astype(BF), w_br_c[i].astype(BF), w_out[i].astype(BF),
                     row(ln1_g[i]), row(ln1_b[i]), tm)
        x2d = _ffn(x2d, p[i].reshape(n_tok, PLE_DIM), w_up[i].astype(BF), w_down[i].astype(BF),
                   w_ple_gate[i].astype(BF), w_ple[i].astype(BF), row(ln2_g[i]), row(ln2_b[i]), tm)
    return x2d.reshape(bsz, s_len, D_MODEL)
```

```python
import functools
import math

import numpy as np
import jax
import jax.numpy as jnp
from jax import lax
from jax.experimental import pallas as pl
from jax.experimental.pallas import tpu as pltpu

D_MODEL = 1024
HEAD_DIM = 64
A_HEADS = 6
IDX_HEADS = 8
A_TOPK_MAX = 256
B_SLOTS = 4
B_PATTERNS = ((128, 1), (512, 4), (2048, 16))
B_GROUPS = 3
C_HEADS = 6
C_BLOCK = 256
C_TOPK = 3
N_BUCKETS = 32
MAX_DISTANCE = 2048
D_FF = 4 * D_MODEL
PLE_DIM = 256
DEPTH = 2
ALPHA = (2 * DEPTH) ** 0.25
LN_EPS = 1e-5
NEG = -1e30
QK_SCALE = HEAD_DIM ** -0.5

LANES = 128
SUBLANES = 8
BF16_ROWS = 16
VMEM_LIMIT = 56 * 1024 * 1024

TQ = 256
BIAS_TQ = LANES
TILES_PER_GROUP = C_BLOCK // TQ
BAND = 128
BF = jnp.bfloat16
F32 = jnp.float32
HALF_BITS = 16
HALF_MIN = -2 ** (HALF_BITS - 1)

_NT = (((1,), (1,)), ((), ()))


def _dot(a, b):
    return jnp.dot(a, b, preferred_element_type=F32)


def _dot_nt(a, b):
    return lax.dot_general(a, b, _NT, preferred_element_type=F32)


def _cparams(*sem):
    return pltpu.CompilerParams(dimension_semantics=sem, vmem_limit_bytes=VMEM_LIMIT)


def _bucket_starts():
    d = np.arange(0, MAX_DISTANCE + 1)
    max_exact = N_BUCKETS // 2
    nf = np.maximum(d, 1).astype(np.float32)
    large = max_exact + (np.log(nf / np.float32(max_exact)) / np.float32(math.log(MAX_DISTANCE / max_exact))
                         * np.float32(N_BUCKETS - max_exact)).astype(np.int32)
    bucket = np.where(d < max_exact, d, np.minimum(large, N_BUCKETS - 1))
    return [int(np.argmax(bucket >= b)) if np.any(bucket >= b) else None for b in range(N_BUCKETS)]


_BUCKET_START = _bucket_starts()


def _bias_from_distance(dist, tab_ref, col):
    val = jnp.full(dist.shape, tab_ref[0, col], F32)
    for b in range(1, N_BUCKETS):
        if _BUCKET_START[b] is not None:
            val = jnp.where(dist >= _BUCKET_START[b], tab_ref[b, col], val)
    return jnp.where(dist < 0, NEG, val)


def _causal_bias_kernel(tab_ref, o_ref, *, head0):
    h = pl.program_id(0)
    n_rows, tq = o_ref.shape[1], o_ref.shape[2]
    u = lax.broadcasted_iota(jnp.int32, (n_rows, tq), 0)
    t = lax.broadcasted_iota(jnp.int32, (n_rows, tq), 1)
    o_ref[0] = _bias_from_distance(t + (n_rows - tq) // 2 - u, tab_ref, head0 + h).astype(o_ref.dtype)


def _causal_bias_master(rel_bias, head0, n_heads, s_len, dtype):
    n_rows = 2 * s_len - BIAS_TQ
    return pl.pallas_call(
        functools.partial(_causal_bias_kernel, head0=head0),
        grid=(n_heads,),
        in_specs=[pl.BlockSpec(memory_space=pltpu.SMEM)],
        out_specs=pl.BlockSpec((1, n_rows, BIAS_TQ), lambda h: (h, 0, 0)),
        out_shape=jax.ShapeDtypeStruct((n_heads, n_rows, BIAS_TQ), dtype),
        compiler_params=_cparams("arbitrary"),
        name="causal_bias_master",
    )(rel_bias)


def _bias_row0(t0, s_len):
    return pl.multiple_of(s_len - BIAS_TQ - t0, BIAS_TQ)


def _causal_bias(bias_ref, h, t0, first_key, n_keys, tq, s_len):
    parts = [bias_ref[h, pl.ds(_bias_row0(t0 + c * BIAS_TQ, s_len) + first_key, n_keys), :]
             for c in range(tq // BIAS_TQ)]
    return parts[0] if len(parts) == 1 else jnp.concatenate(parts, axis=1)


def _band_bias_kernel(tab_ref, o_ref, *, head0):
    g = pl.program_id(0)
    variant = pl.program_id(1)
    pair = pl.program_id(2)
    row = lax.broadcasted_iota(jnp.int32, (BAND, 2 * BAND), 0)
    col = lax.broadcasted_iota(jnp.int32, (BAND, 2 * BAND), 1)
    j = row + BAND - col
    no_prev = (variant == 1) & (col < BAND)
    for gi, (_, dil) in enumerate(B_PATTERNS):
        @pl.when(g == gi)
        def _(dil=dil):
            for half in range(2):
                bias = _bias_from_distance(j * dil, tab_ref, head0 + g * B_SLOTS + 2 * pair + half)
                bias = jnp.where((j > BAND) | no_prev, NEG, bias)
                o_ref[0, 0, 0, :, half * 2 * BAND:(half + 1) * 2 * BAND] = bias


def _band_bias_tiles(rel_bias, head0):
    return pl.pallas_call(
        functools.partial(_band_bias_kernel, head0=head0),
        grid=(B_GROUPS, 2, B_SLOTS // 2),
        in_specs=[pl.BlockSpec(memory_space=pltpu.SMEM)],
        out_specs=pl.BlockSpec((1, 1, 1, BAND, 4 * BAND), lambda g, v, p: (g, v, p, 0, 0)),
        out_shape=jax.ShapeDtypeStruct((B_GROUPS, 2, B_SLOTS // 2, BAND, 4 * BAND), F32),
        compiler_params=_cparams("arbitrary", "arbitrary", "arbitrary"),
        name="band_bias_tiles",
    )(rel_bias)


_PROJ_OUTS = (
    ("aq", A_HEADS * HEAD_DIM, QK_SCALE),
    ("akk", 2 * HEAD_DIM, 1.0),
    ("aii", 2 * HEAD_DIM, 1.0),
    ("iq", IDX_HEADS * HEAD_DIM, QK_SCALE),
    ("bq0", B_SLOTS * HEAD_DIM, QK_SCALE),
    ("bq1", B_SLOTS * HEAD_DIM, QK_SCALE),
    ("bq2", B_SLOTS * HEAD_DIM, QK_SCALE),
    ("bk", B_SLOTS * HEAD_DIM, 1.0),
    ("bv", B_SLOTS * HEAD_DIM, 1.0),
    ("cq", C_HEADS * HEAD_DIM, QK_SCALE),
    ("ck", C_HEADS * HEAD_DIM, 1.0),
)
_PROJ_WIDTH = sum(w for _, w, _ in _PROJ_OUTS)
_PROJ_LANE_TILED = ("bq0", "bq1", "bq2", "bk", "bv")
_PROJ_OUTS_T = (
    ("avT", HEAD_DIM, BF),
    ("cvT", C_HEADS * HEAD_DIM, BF),
    ("iwT", BF16_ROWS, F32),
)
_PROJ_ROWS_T = sum(r for _, r, _ in _PROJ_OUTS_T)


def _pack_w_in(w):
    widths = (384, 64, 64, 512, 64, 8, 768, 256, 256, 384, 384, 384)
    offs = np.concatenate([[0], np.cumsum(widths)])
    aq, ak, av, iq, ik, iw, bq, bk, bv, cq, ck, cv = (w[:, offs[n]:offs[n + 1]] for n in range(12))
    cols = jnp.concatenate([aq, ak, ak, ik, ik, iq, bq, bk, bv, cq, ck], axis=1)
    iw_pad = jnp.concatenate([iw, jnp.zeros((w.shape[0], BF16_ROWS - IDX_HEADS), w.dtype)], axis=1)
    rows = jnp.concatenate([av, cv, iw_pad], axis=1).T
    return cols.astype(BF), rows.astype(BF)


def _proj_kernel(x_ref, w_ref, wt_ref, *o_refs):
    xb = x_ref[...].astype(BF)
    off = 0
    for o_ref, (_, width, scale) in zip(o_refs, _PROJ_OUTS):
        res = _dot(xb, w_ref[:, off:off + width])
        if scale != 1.0:
            res = res * scale
        if len(o_ref.shape) == 4:
            for lt in range(width // LANES):
                o_ref[0, lt] = res[:, lt * LANES:(lt + 1) * LANES]
        else:
            o_ref[...] = res.astype(o_ref.dtype)
        off += width
    off = 0
    for o_ref, (_, rows, _) in zip(o_refs[len(_PROJ_OUTS):], _PROJ_OUTS_T):
        o_ref[0] = _dot_nt(wt_ref[off:off + rows, :], xb).astype(o_ref.dtype)
        off += rows


def _project(x2d, w_cols, w_rows, tm, s_len):
    n = x2d.shape[0]
    tiles_per_seq = s_len // tm
    out_shape, out_specs = [], []
    for name, w, _ in _PROJ_OUTS:
        if name in _PROJ_LANE_TILED:
            out_shape.append(jax.ShapeDtypeStruct((n // s_len, w // LANES, s_len, LANES), F32))
            out_specs.append(pl.BlockSpec((1, w // LANES, tm, LANES),
                                          lambda i: (i // tiles_per_seq, 0, i % tiles_per_seq, 0)))
        else:
            out_shape.append(jax.ShapeDtypeStruct((n, w), BF))
            out_specs.append(pl.BlockSpec((tm, w), lambda i: (i, 0)))
    out_shape += [jax.ShapeDtypeStruct((n // s_len, r, s_len), dt) for _, r, dt in _PROJ_OUTS_T]
    out_specs += [pl.BlockSpec((1, r, tm), lambda i: (i // tiles_per_seq, 0, i % tiles_per_seq))
                  for _, r, _ in _PROJ_OUTS_T]
    outs = pl.pallas_call(
        _proj_kernel,
        grid=(n // tm,),
        in_specs=[pl.BlockSpec((tm, D_MODEL), lambda i: (i, 0)),
                  pl.BlockSpec((D_MODEL, _PROJ_WIDTH), lambda i: (0, 0)),
                  pl.BlockSpec((_PROJ_ROWS_T, D_MODEL), lambda i: (0, 0))],
        out_specs=out_specs,
        out_shape=out_shape,
        compiler_params=_cparams("parallel"),
        name="in_proj",
    )(x2d, w_cols, w_rows)
    names = [name for name, _, _ in _PROJ_OUTS] + [name for name, _, _ in _PROJ_OUTS_T]
    return dict(zip(names, outs))


def _head_pair_rhs(pair):
    lane = lax.broadcasted_iota(jnp.int32, pair.shape, 1)
    zero = jnp.zeros_like(pair)
    return jnp.concatenate([jnp.where(lane < HEAD_DIM, pair, zero), jnp.where(lane < HEAD_DIM, zero, pair)], axis=0)


ROW_CHUNK = 8 * SUBLANES


def _max_rows(x):
    chunk = ROW_CHUNK * (4 // x.dtype.itemsize)
    acc = x[:chunk]
    for c in range(1, x.shape[0] // chunk):
        acc = jnp.maximum(acc, x[c * chunk:(c + 1) * chunk])
    return jnp.max(acc.astype(F32), axis=0, keepdims=True)


def _softmax_pv(logits, v_t_ones):
    m = _max_rows(logits)
    e = jnp.exp(logits - m.astype(logits.dtype)).astype(BF)
    o = _dot(v_t_ones, e)
    return o[:HEAD_DIM] / o[HEAD_DIM:HEAD_DIM + 1]


def _with_ones_rows(v_t):
    return jnp.concatenate([v_t, jnp.ones((BF16_ROWS, v_t.shape[1]), v_t.dtype)], axis=0)


def _const_spec2(shape):
    return pl.BlockSpec(shape, lambda i, b: (0,) * len(shape), pipeline_mode=pl.Buffered(1))


def _grouped_tiles(call_group, out_shape):
    s_len = out_shape.shape[1]
    out = jnp.zeros(out_shape.shape, out_shape.dtype)
    for g in range(s_len // (TILES_PER_GROUP * TQ)):
        out = call_group(g, (g + 1) * TILES_PER_GROUP * TQ, out)
    return out


def _dsa_kernel(q_ref, iq_ref, iw_ref, kk_ref, ii_ref, vt_ref, bias_ref, _, o_ref, key_ref, half_ref, mask_ref, *,
                topk, tile0, s_len):
    i = tile0 + pl.program_id(0)
    n_keys, tq = key_ref.shape
    pos = lax.broadcasted_iota(jnp.int32, (n_keys, tq), 0)
    qry = i * tq + lax.broadcasted_iota(jnp.int32, (n_keys, tq), 1)

    if n_keys > topk:
        iq = iq_ref[0]
        iw = iw_ref[0]
        ii = ii_ref[0]
        index = jnp.zeros((n_keys, tq), F32)
        for p in range(IDX_HEADS // 2):
            sc = _dot_nt(ii, _head_pair_rhs(iq[:, p * LANES:(p + 1) * LANES]))
            index = index + iw[2 * p:2 * p + 1, :] * jnp.maximum(sc[:, :tq], 0.0)
            index = index + iw[2 * p + 1:2 * p + 2, :] * jnp.maximum(sc[:, tq:], 0.0)
        index = jnp.where(pos <= qry, index + 0.0, -jnp.inf)
        bits = pltpu.bitcast(index, jnp.int32)
        key_ref[...] = jnp.where(bits < 0, bits ^ jnp.int32(0x7FFFFFFF), bits)

        chunk_pos = lax.broadcasted_iota(jnp.int32, (ROW_CHUNK, tq), 0)

        def count(pred):
            acc = None
            for c in range(n_keys // ROW_CHUNK):
                hit = pred(key_ref[c * ROW_CHUNK:(c + 1) * ROW_CHUNK, :], chunk_pos + c * ROW_CHUNK)
                ones = jnp.where(hit, 1.0, 0.0)
                acc = ones if acc is None else acc + ones
            return jnp.sum(acc, axis=0, keepdims=True)

        def count_half(pred):
            acc = None
            for c in range(n_keys // (2 * ROW_CHUNK)):
                hit = pred(half_ref[c * 2 * ROW_CHUNK:(c + 1) * 2 * ROW_CHUNK, :])
                ones = jnp.where(hit, jnp.int16(1), jnp.int16(0))
                acc = ones if acc is None else acc + ones
            return jnp.sum(acc.astype(F32), axis=0, keepdims=True)

        def search_half(need, count_at_min):
            c0 = count_half(lambda h: h >= 0)
            t0 = jnp.where(c0 >= need, 0, HALF_MIN).astype(jnp.int32)
            n0 = jnp.where(c0 >= need, c0, count_at_min)

            def step(it, carry):
                t, n = carry
                cand = t | (jnp.int32(1) << (HALF_BITS - 2 - it))
                cand16 = cand.astype(jnp.int16)
                c = count_half(lambda h: h >= cand16)
                ok = c >= need
                return jnp.where(ok, cand, t), jnp.where(ok, c, n)

            return lax.fori_loop(0, HALF_BITS - 1, step, (t0, n0))

        k_f = float(topk)
        half_ref[...] = (key_ref[...] >> HALF_BITS).astype(jnp.int16)
        t_hi, cnt_ge_hi = search_half(k_f, float(n_keys))
        t_hi16 = t_hi.astype(jnp.int16)
        cnt_gt_hi = count_half(lambda h: h > t_hi16)
        key = key_ref[...]
        low = (key & (2 ** HALF_BITS - 1)) + HALF_MIN
        half_ref[...] = jnp.where((key >> HALF_BITS) == t_hi, low, HALF_MIN).astype(jnp.int16)
        t_lo, cnt_lo = search_half(k_f - cnt_gt_hi, cnt_ge_hi - cnt_gt_hi)
        thr = t_hi * 2 ** HALF_BITS + (t_lo - HALF_MIN)
        cnt_ge = cnt_gt_hi + cnt_lo

        def tie_search():
            remaining = k_f - count(lambda k, _: k > thr)
            n_bits = (n_keys - 1).bit_length()

            def tie_step(it, last):
                cand = last | (jnp.int32(1) << (n_bits - 1 - it))
                c = count(lambda k, kpos: (k == thr) & (kpos < cand))
                return jnp.where(c < remaining, cand, last)

            return lax.fori_loop(0, n_bits, tie_step, jnp.zeros((1, tq), jnp.int32))

        has_ties = jnp.max(cnt_ge) > k_f

        @pl.when(has_ties)
        def _():
            last_tie = tie_search()
            key = key_ref[...]
            selected = (key > thr) | ((key == thr) & (pos <= last_tie))
            mask_ref[...] = jnp.where(selected, 0.0, NEG)

        @pl.when(jnp.logical_not(has_ties))
        def _():
            mask_ref[...] = jnp.where(key_ref[...] >= thr, 0.0, NEG)
    else:
        mask_ref[...] = jnp.zeros_like(mask_ref)

    q = q_ref[0]
    kk = kk_ref[0]
    v_t = _with_ones_rows(vt_ref[0])
    outs = []
    for p in range(A_HEADS // 2):
        logits = _dot_nt(kk, _head_pair_rhs(q[:, p * LANES:(p + 1) * LANES]))
        for half in range(2):
            bias = _causal_bias(bias_ref, 2 * p + half, i * tq, 0, n_keys, tq, s_len)
            lg = logits[:, half * tq:(half + 1) * tq] + bias + mask_ref[...]
            outs.append(_softmax_pv(lg, v_t))
    o_ref[0] = jnp.concatenate(outs, axis=0).T.astype(o_ref.dtype)


def _dsa(aq, iq, iw_t, akk, aii, av_t, bias_master):
    bsz, s_len, _ = aq.shape
    topk = min(A_TOPK_MAX, s_len // 4)
    out_shape = jax.ShapeDtypeStruct(aq.shape, BF)

    def call_group(g, n_keys, out):
        tile0 = g * TILES_PER_GROUP
        q_spec = lambda w: pl.BlockSpec((1, TQ, w), lambda i, b: (b, tile0 + i, 0))
        keys = lambda w: pl.BlockSpec((1, n_keys, w), lambda i, b: (b, 0, 0))
        return pl.pallas_call(
            functools.partial(_dsa_kernel, topk=topk, tile0=tile0, s_len=s_len),
            grid=(TILES_PER_GROUP, bsz),
            in_specs=[q_spec(A_HEADS * HEAD_DIM), q_spec(IDX_HEADS * HEAD_DIM),
                      pl.BlockSpec((1, BF16_ROWS, TQ), lambda i, b: (b, 0, tile0 + i)),
                      keys(2 * HEAD_DIM), keys(2 * HEAD_DIM),
                      pl.BlockSpec((1, HEAD_DIM, n_keys), lambda i, b: (b, 0, 0)),
                      _const_spec2(bias_master.shape), pl.BlockSpec(memory_space=pl.ANY)],
            out_specs=q_spec(A_HEADS * HEAD_DIM),
            out_shape=out_shape,
            input_output_aliases={7: 0},
            scratch_shapes=[pltpu.VMEM((n_keys, TQ), jnp.int32), pltpu.VMEM((n_keys, TQ), jnp.int16),
                            pltpu.VMEM((n_keys, TQ), F32)],
            compiler_params=_cparams("arbitrary", "arbitrary"),
            name=f"dsa_mixer_{g}",
        )(aq, iq, iw_t, akk, aii, av_t, bias_master, out)

    return _grouped_tiles(call_group, out_shape)


_PAIR_W = 2 * HEAD_DIM
_N_PAIRS = B_SLOTS // 2
_N_STATE = 2 * _N_PAIRS


def _class_rows(r, t, dil):
    start = r + t * BAND * dil
    return pl.ds(start, BAND, stride=dil) if dil > 1 else pl.ds(start, BAND)


def _pair_ones(n_keys):
    row = lax.broadcasted_iota(jnp.int32, (2 * n_keys, _PAIR_W), 0)
    lane = lax.broadcasted_iota(jnp.int32, (2 * n_keys, _PAIR_W), 1)
    return jnp.where((row < n_keys) == (lane < HEAD_DIM), 1.0, 0.0).astype(BF)


def _band_pairs(units, ones_bd):
    n_keys = units[0][1].shape[0]
    lane = lax.broadcasted_iota(jnp.int32, (BAND, _PAIR_W), 1)
    logits = [_dot_nt(q, _head_pair_rhs(k)) + bias for q, k, _, bias in units]
    maxes = [(jnp.max(lg[:, :n_keys], axis=1, keepdims=True), jnp.max(lg[:, n_keys:], axis=1, keepdims=True))
             for lg in logits]
    probs = [jnp.concatenate([jnp.exp(lg[:, :n_keys] - ma), jnp.exp(lg[:, n_keys:] - mb)], axis=1).astype(BF)
             for lg, (ma, mb) in zip(logits, maxes)]
    results = []
    for (_, _, v, _), e, (ma, mb) in zip(units, probs, maxes):
        acc = _dot(e, _head_pair_rhs(v))
        den = _dot(e, ones_bd)
        results.append((acc / den, jnp.where(lane < HEAD_DIM, ma, mb) + jnp.log(den)))
    return results


def _band_state_kernel(q_ref, k_ref, v_ref, kp_ref, vp_ref, bias_ref, st_ref, *, dil):
    c = pl.program_id(1)
    n_tiles = q_ref.shape[2] // (BAND * dil)
    first_variant = jnp.where(c == 0, 1, 0)
    load = lambda ref, p, rows: ref[0, p, rows, :].astype(BF)
    units, where = [], []
    for r in range(dil):
        for t in range(n_tiles):
            rows = _class_rows(r, t, dil)
            for p in range(_N_PAIRS):
                if t == 0:
                    prev = _class_rows(r, 0, dil)
                    k = jnp.concatenate([load(kp_ref, p, prev), load(k_ref, p, rows)], axis=0)
                    v = jnp.concatenate([load(vp_ref, p, prev), load(v_ref, p, rows)], axis=0)
                    bias = bias_ref[0, first_variant, p]
                else:
                    prev = _class_rows(r, t - 1, dil)
                    k = jnp.concatenate([load(k_ref, p, prev), load(k_ref, p, rows)], axis=0)
                    v = jnp.concatenate([load(v_ref, p, prev), load(v_ref, p, rows)], axis=0)
                    bias = bias_ref[0, 0, p]
                units.append((load(q_ref, p, rows), k, v, bias))
                where.append((rows, p))
    for (rows, p), (out, lse) in zip(where, _band_pairs(units, _pair_ones(2 * BAND))):
        st_ref[0, p, rows, :] = out
        st_ref[0, _N_PAIRS + p, rows, :] = lse


def _band_state(q, k, v, bias_tiles, g, dil):
    bsz, n_lt, s_len, _ = q.shape
    rows = 4 * BAND
    prev_rows = BAND * dil
    per = rows // prev_rows
    main = pl.BlockSpec((1, n_lt, rows, LANES), lambda b, c: (b, 0, c, 0))
    prev = pl.BlockSpec((1, n_lt, prev_rows, LANES), lambda b, c: (b, 0, jnp.maximum(c * per - 1, 0), 0))
    return pl.pallas_call(
        functools.partial(_band_state_kernel, dil=dil),
        grid=(bsz, s_len // rows),
        in_specs=[main, main, main, prev, prev,
                  pl.BlockSpec((1, 2, _N_PAIRS, BAND, 4 * BAND), lambda b, c: (g, 0, 0, 0, 0))],
        out_specs=pl.BlockSpec((1, _N_STATE, rows, LANES), lambda b, c: (b, 0, c, 0)),
        out_shape=jax.ShapeDtypeStruct((bsz, _N_STATE, s_len, LANES), F32),
        compiler_params=_cparams("parallel", "arbitrary"),
        name=f"band_state_{g}",
    )(q, k, v, k, v, bias_tiles)


def _band_merge_kernel(q_ref, k_ref, v_ref, st1_ref, st2_ref, bias_ref, o_ref, *, dil):
    load = lambda ref, p, rows: ref[0, p, rows, :].astype(BF)
    own_bias = [jnp.concatenate([bias_ref[0, 0, p, :, BAND:2 * BAND], bias_ref[0, 0, p, :, 3 * BAND:]], axis=1)
                for p in range(_N_PAIRS)]
    units, where = [], []
    for r in range(dil):
        rows = _class_rows(r, 0, dil)
        for p in range(_N_PAIRS):
            units.append((load(q_ref, p, rows), load(k_ref, p, rows), load(v_ref, p, rows), own_bias[p]))
            where.append((rows, p))
    for (rows, p), (out3, lse3) in zip(where, _band_pairs(units, _pair_ones(BAND))):
        lse1, lse2 = st1_ref[0, _N_PAIRS + p, rows, :], st2_ref[0, _N_PAIRS + p, rows, :]
        top = jnp.maximum(jnp.maximum(lse1, lse2), lse3)
        w1, w2, w3 = jnp.exp(lse1 - top), jnp.exp(lse2 - top), jnp.exp(lse3 - top)
        num = w1 * st1_ref[0, p, rows, :] + w2 * st2_ref[0, p, rows, :] + w3 * out3
        o_ref[0, p, rows, :] = num / (w1 + w2 + w3)


def _band_merge(q, k, v, st1, st2, bias_tiles, g, dil):
    bsz, n_lt, s_len, _ = q.shape
    assert s_len == BAND * dil
    full = lambda n: pl.BlockSpec((1, n, s_len, LANES), lambda b: (b, 0, 0, 0))
    return pl.pallas_call(
        functools.partial(_band_merge_kernel, dil=dil),
        grid=(bsz,),
        in_specs=[full(n_lt), full(n_lt), full(n_lt), full(_N_STATE), full(_N_STATE),
                  pl.BlockSpec((1, 2, _N_PAIRS, BAND, 4 * BAND), lambda b: (g, 0, 0, 0, 0))],
        out_specs=full(n_lt),
        out_shape=jax.ShapeDtypeStruct((bsz, n_lt, s_len, LANES), F32),
        compiler_params=_cparams("parallel"),
        name="band_merge",
    )(q, k, v, st1, st2, bias_tiles)


def _dilated(bq, bk, bv, bias_tiles):
    (_, d0), (_, d1), (_, d2) = B_PATTERNS
    st0 = _band_state(bq[0], bk, bv, bias_tiles, 0, d0)
    st1 = _band_state(bq[1], bk, bv, bias_tiles, 1, d1)
    return _band_merge(bq[2], bk, bv, st0, st1, bias_tiles, 2, d2)


def _block_mean_kernel(k_ref, o_ref):
    n_blk = k_ref.shape[1] // C_BLOCK
    o_ref[...] = jnp.zeros_like(o_ref)
    for j in range(n_blk):
        blk = k_ref[0, j * C_BLOCK:(j + 1) * C_BLOCK, :].astype(F32)
        o_ref[0, j:j + 1, :] = (jnp.sum(blk, axis=0, keepdims=True) * (1.0 / C_BLOCK)).astype(o_ref.dtype)


def _block_means(ck):
    bsz, s_len, w = ck.shape
    assert s_len // C_BLOCK <= BF16_ROWS
    return pl.pallas_call(
        _block_mean_kernel,
        grid=(bsz,),
        in_specs=[pl.BlockSpec((1, s_len, w), lambda b: (b, 0, 0))],
        out_specs=pl.BlockSpec((1, BF16_ROWS, w), lambda b: (b, 0, 0)),
        out_shape=jax.ShapeDtypeStruct((bsz, BF16_ROWS, w), BF),
        compiler_params=_cparams("parallel"),
        name="moba_block_means",
    )(ck)


def _moba_kernel(q_ref, k_ref, vt_ref, kmean_ref, bias_ref, _, o_ref, *, cur, s_len):
    tq = q_ref.shape[1]
    t0 = (cur * TILES_PER_GROUP + pl.program_id(0)) * tq
    q = q_ref[0]
    kmean = kmean_ref[0]
    blk = lax.broadcasted_iota(jnp.int32, (BF16_ROWS, tq), 0)
    outs = []
    for p in range(C_HEADS // 2):
        ps = slice(p * LANES, (p + 1) * LANES)
        rhs = _head_pair_rhs(q[:, ps])
        logits = _dot_nt(k_ref[0, :, ps], rhs).astype(BF)
        gates = _dot_nt(kmean[:, ps], rhs)
        for half in range(2):
            h = 2 * p + half
            hq = slice(half * tq, (half + 1) * tq)
            if cur > C_TOPK:
                gate = jnp.where(blk < cur, gates[:, hq], -jnp.inf)
                off = jnp.where(blk == cur, 0.0, NEG)
                for j in range(cur):
                    gj = gate[j:j + 1, :]
                    beats = (gate > gj) | ((gate == gj) & (blk < j))
                    rank = jnp.sum(jnp.where(beats, 1.0, 0.0), axis=0, keepdims=True)
                    off = jnp.where((blk == j) & (rank < float(C_TOPK)), 0.0, off)
            pieces = []
            for j in range(cur + 1):
                ks = slice(j * C_BLOCK, (j + 1) * C_BLOCK)
                piece = logits[ks, hq] + _causal_bias(bias_ref, h, t0, j * C_BLOCK, C_BLOCK, tq, s_len)
                if cur > C_TOPK:
                    piece = piece + off[j:j + 1, :].astype(BF)
                pieces.append(piece)
            lg = jnp.concatenate(pieces, axis=0)
            outs.append(_softmax_pv(lg, _with_ones_rows(vt_ref[0, h * HEAD_DIM:(h + 1) * HEAD_DIM, :])))
    o_ref[0] = jnp.concatenate(outs, axis=0).T.astype(o_ref.dtype)


def _moba(cq, ck, cv_t, bias_master):
    bsz, s_len, w = cq.shape
    assert TILES_PER_GROUP * TQ == C_BLOCK
    kmean = _block_means(ck)
    out_shape = jax.ShapeDtypeStruct(cq.shape, BF)

    def call_group(g, n_keys, out):
        tile0 = g * TILES_PER_GROUP
        return pl.pallas_call(
            functools.partial(_moba_kernel, cur=g, s_len=s_len),
            grid=(TILES_PER_GROUP, bsz),
            in_specs=[pl.BlockSpec((1, TQ, w), lambda i, b: (b, tile0 + i, 0)),
                      pl.BlockSpec((1, n_keys, w), lambda i, b: (b, 0, 0)),
                      pl.BlockSpec((1, w, n_keys), lambda i, b: (b, 0, 0)),
                      pl.BlockSpec((1, BF16_ROWS, w), lambda i, b: (b, 0, 0)),
                      _const_spec2(bias_master.shape), pl.BlockSpec(memory_space=pl.ANY)],
            out_specs=pl.BlockSpec((1, TQ, w), lambda i, b: (b, tile0 + i, 0)),
            out_shape=out_shape,
            input_output_aliases={5: 0},
            compiler_params=_cparams("arbitrary", "arbitrary"),
            name=f"moba_mixer_{g}",
        )(cq, ck, cv_t, kmean, bias_master, out)

    return _grouped_tiles(call_group, out_shape)


def _layer_norm(y, g_ref, b_ref):
    mu = jnp.mean(y, axis=-1, keepdims=True)
    yc = y - mu
    var = jnp.mean(yc * yc, axis=-1, keepdims=True)
    return yc * lax.rsqrt(var + LN_EPS) * g_ref[...] + b_ref[...]


def _merge_kernel(x_ref, oa_ref, ob_ref, oc_ref, wg_ref, wa_ref, wb_ref, wc_ref, wo_ref, g_ref, b_ref, y_ref):
    x = x_ref[...]
    xb = x.astype(BF)
    merged = None
    o_b = jnp.concatenate([ob_ref[0, lt] for lt in range(ob_ref.shape[1])], axis=1).astype(BF)
    for n, (o, w_ref) in enumerate(((oa_ref[...], wa_ref), (o_b, wb_ref), (oc_ref[...], wc_ref))):
        gate = jax.nn.sigmoid(_dot(xb, wg_ref[:, n * D_MODEL:(n + 1) * D_MODEL]))
        term = gate * _dot(o, w_ref[...])
        merged = term if merged is None else merged + term
    y = ALPHA * x + _dot(merged.astype(BF), wo_ref[...])
    y_ref[...] = _layer_norm(y, g_ref, b_ref)


def _const_spec(shape):
    return pl.BlockSpec(shape, lambda i: (0,) * len(shape), pipeline_mode=pl.Buffered(1))


def _merge(x2d, oa, ob, oc, wg, wa, wb, wc, wo, ln_g, ln_b, tm):
    n = x2d.shape[0]
    tiles_per_seq = ob.shape[2] // tm
    rows = lambda w: pl.BlockSpec((tm, w), lambda i: (i, 0))
    ob_spec = pl.BlockSpec((1, ob.shape[1], tm, LANES), lambda i: (i // tiles_per_seq, 0, i % tiles_per_seq, 0))
    return pl.pallas_call(
        _merge_kernel,
        grid=(n // tm,),
        in_specs=[rows(D_MODEL), rows(oa.shape[1]), ob_spec, rows(oc.shape[1]),
                  _const_spec(wg.shape), _const_spec(wa.shape), _const_spec(wb.shape), _const_spec(wc.shape),
                  _const_spec(wo.shape), _const_spec(ln_g.shape), _const_spec(ln_b.shape)],
        out_specs=rows(D_MODEL),
        out_shape=jax.ShapeDtypeStruct((n, D_MODEL), F32),
        compiler_params=_cparams("parallel"),
        name="merge_out_ln",
    )(x2d, oa, ob, oc, wg, wa, wb, wc, wo, ln_g, ln_b)


_FF_CHUNK = 1024


def _ffn_kernel(x_ref, p_ref, wu_ref, wd_ref, wpg_ref, wp_ref, g_ref, b_ref, y_ref):
    x = x_ref[...]
    xb = x.astype(BF)
    y = ALPHA * x + jax.nn.sigmoid(_dot(xb, wpg_ref[...])) * _dot(p_ref[...].astype(BF), wp_ref[...])
    for c in range(D_FF // _FF_CHUNK):
        cs = slice(c * _FF_CHUNK, (c + 1) * _FF_CHUNK)
        u = jnp.maximum(_dot(xb, wu_ref[:, cs]), 0.0)
        y = y + _dot((u * u).astype(BF), wd_ref[cs, :])
    y_ref[...] = _layer_norm(y, g_ref, b_ref)


def _ffn(x2d, p2d, wu, wd, wpg, wp, ln_g, ln_b, tm):
    n = x2d.shape[0]
    rows = lambda w: pl.BlockSpec((tm, w), lambda i: (i, 0))
    return pl.pallas_call(
        _ffn_kernel,
        grid=(n // tm,),
        in_specs=[rows(D_MODEL), rows(PLE_DIM), _const_spec(wu.shape), _const_spec(wd.shape),
                  _const_spec(wpg.shape), _const_spec(wp.shape), _const_spec(ln_g.shape), _const_spec(ln_b.shape)],
        out_specs=rows(D_MODEL),
        out_shape=jax.ShapeDtypeStruct((n, D_MODEL), F32),
        compiler_params=_cparams("parallel"),
        name="ffn_ple_ln",
    )(x2d, p2d, wu, wd, wpg, wp, ln_g, ln_b)


def kernel(x, p, w_in, w_gate, w_br_a, w_br_b, w_br_c, w_out, ln1_g, ln1_b,
           w_up, w_down, w_ple_gate, w_ple, ln2_g, ln2_b, rel_bias):
    bsz, s_len, d_model = x.shape
    assert d_model == D_MODEL and s_len == MAX_DISTANCE, (x.shape,)
    n_tok = bsz * s_len
    tm = 512

    b_head0 = A_HEADS
    c_head0 = A_HEADS + B_GROUPS * B_SLOTS
    bias_a = _causal_bias_master(rel_bias, 0, A_HEADS, s_len, F32)
    bias_b = _band_bias_tiles(rel_bias, b_head0)
    bias_c = _causal_bias_master(rel_bias, c_head0, C_HEADS, s_len, BF)

    x2d = x.reshape(n_tok, D_MODEL)
    for i in range(DEPTH):
        pr = _project(x2d, *_pack_w_in(w_in[i]), tm, s_len)
        seq = lambda name: pr[name].reshape(bsz, s_len, -1)
        o_a = _dsa(seq("aq"), seq("iq"), pr["iwT"], seq("akk"), seq("aii"), pr["avT"], bias_a)
        o_b = _dilated((pr["bq0"], pr["bq1"], pr["bq2"]), pr["bk"], pr["bv"], bias_b)
        o_c = _moba(seq("cq"), seq("ck"), pr["cvT"], bias_c)
        flat = lambda a: a.reshape(n_tok, -1)
        row = lambda a: a.reshape(1, D_MODEL)
        x2d = _merge(x2d, flat(o_a), o_b, flat(o_c), w_gate[i].astype(BF), w_br_a[i].astype(BF),
                     w_br_b[i].astype(BF), w_br_c[i].astype(BF), w_out[i].astype(BF),
                     row(ln1_g[i]), row(ln1_b[i]), tm)
        x2d = _ffn(x2d, p[i].reshape(n_tok, PLE_DIM), w_up[i].astype(BF), w_down[i].astype(BF),
                   w_ple_gate[i].astype(BF), w_ple[i].astype(BF), row(ln2_g[i]), row(ln2_b[i]), tm)
    return x2d.reshape(bsz, s_len, D_MODEL)
```

```python
import functools
import math

import numpy as np
import jax
import jax.numpy as jnp
from jax import lax
from jax.experimental import pallas as pl
from jax.experimental.pallas import tpu as pltpu

D_MODEL = 1024
HEAD_DIM = 64
A_HEADS = 6
IDX_HEADS = 8
A_TOPK_MAX = 256
B_SLOTS = 4
B_PATTERNS = ((128, 1), (512, 4), (2048, 16))
B_GROUPS = 3
C_HEADS = 6
C_BLOCK = 256
C_TOPK = 3
N_BUCKETS = 32
MAX_DISTANCE = 2048
D_FF = 4 * D_MODEL
PLE_DIM = 256
DEPTH = 2
ALPHA = (2 * DEPTH) ** 0.25
LN_EPS = 1e-5
NEG = -1e30
QK_SCALE = HEAD_DIM ** -0.5

LANES = 128
SUBLANES = 8
BF16_ROWS = 16
VMEM_LIMIT = 56 * 1024 * 1024

TQ = 256
BIAS_TQ = LANES
TILES_PER_GROUP = C_BLOCK // TQ
BAND = 128
BF = jnp.bfloat16
F32 = jnp.float32
HALF_BITS = 16
HALF_MIN = -2 ** (HALF_BITS - 1)
FEW_TIES = 8.0

_NT = (((1,), (1,)), ((), ()))


def _dot(a, b):
    return jnp.dot(a, b, preferred_element_type=F32)


def _dot_nt(a, b):
    return lax.dot_general(a, b, _NT, preferred_element_type=F32)


def _cparams(*sem):
    return pltpu.CompilerParams(dimension_semantics=sem, vmem_limit_bytes=VMEM_LIMIT)


def _bucket_starts():
    d = np.arange(0, MAX_DISTANCE + 1)
    max_exact = N_BUCKETS // 2
    nf = np.maximum(d, 1).astype(np.float32)
    large = max_exact + (np.log(nf / np.float32(max_exact)) / np.float32(math.log(MAX_DISTANCE / max_exact))
                         * np.float32(N_BUCKETS - max_exact)).astype(np.int32)
    bucket = np.where(d < max_exact, d, np.minimum(large, N_BUCKETS - 1))
    return [int(np.argmax(bucket >= b)) if np.any(bucket >= b) else None for b in range(N_BUCKETS)]


_BUCKET_START = _bucket_starts()


def _bias_from_distance(dist, tab_ref, col):
    val = jnp.full(dist.shape, tab_ref[0, col], F32)
    for b in range(1, N_BUCKETS):
        if _BUCKET_START[b] is not None:
            val = jnp.where(dist >= _BUCKET_START[b], tab_ref[b, col], val)
    return jnp.where(dist < 0, NEG, val)


def _causal_bias_kernel(tab_ref, o_ref, *, head0):
    h = pl.program_id(0)
    n_rows, tq = o_ref.shape[1], o_ref.shape[2]
    u = lax.broadcasted_iota(jnp.int32, (n_rows, tq), 0)
    t = lax.broadcasted_iota(jnp.int32, (n_rows, tq), 1)
    o_ref[0] = _bias_from_distance(t + (n_rows - tq) // 2 - u, tab_ref, head0 + h).astype(o_ref.dtype)


def _causal_bias_master(rel_bias, head0, n_heads, s_len, dtype):
    n_rows = 2 * s_len - BIAS_TQ
    return pl.pallas_call(
        functools.partial(_causal_bias_kernel, head0=head0),
        grid=(n_heads,),
        in_specs=[pl.BlockSpec(memory_space=pltpu.SMEM)],
        out_specs=pl.BlockSpec((1, n_rows, BIAS_TQ), lambda h: (h, 0, 0)),
        out_shape=jax.ShapeDtypeStruct((n_heads, n_rows, BIAS_TQ), dtype),
        compiler_params=_cparams("arbitrary"),
        name="causal_bias_master",
    )(rel_bias)


def _bias_row0(t0, s_len):
    return pl.multiple_of(s_len - BIAS_TQ - t0, BIAS_TQ)


def _causal_bias(bias_ref, h, t0, first_key, n_keys, tq, s_len):
    parts = [bias_ref[h, pl.ds(_bias_row0(t0 + c * BIAS_TQ, s_len) + first_key, n_keys), :]
             for c in range(tq // BIAS_TQ)]
    return parts[0] if len(parts) == 1 else jnp.concatenate(parts, axis=1)


def _band_bias_kernel(tab_ref, o_ref, *, head0):
    g = pl.program_id(0)
    variant = pl.program_id(1)
    pair = pl.program_id(2)
    row = lax.broadcasted_iota(jnp.int32, (BAND, 2 * BAND), 0)
    col = lax.broadcasted_iota(jnp.int32, (BAND, 2 * BAND), 1)
    j = row + BAND - col
    no_prev = (variant == 1) & (col < BAND)
    for gi, (_, dil) in enumerate(B_PATTERNS):
        @pl.when(g == gi)
        def _(dil=dil):
            for half in range(2):
                bias = _bias_from_distance(j * dil, tab_ref, head0 + g * B_SLOTS + 2 * pair + half)
                bias = jnp.where((j > BAND) | no_prev, NEG, bias)
                o_ref[0, 0, 0, :, half * 2 * BAND:(half + 1) * 2 * BAND] = bias


def _band_bias_tiles(rel_bias, head0):
    return pl.pallas_call(
        functools.partial(_band_bias_kernel, head0=head0),
        grid=(B_GROUPS, 2, B_SLOTS // 2),
        in_specs=[pl.BlockSpec(memory_space=pltpu.SMEM)],
        out_specs=pl.BlockSpec((1, 1, 1, BAND, 4 * BAND), lambda g, v, p: (g, v, p, 0, 0)),
        out_shape=jax.ShapeDtypeStruct((B_GROUPS, 2, B_SLOTS // 2, BAND, 4 * BAND), F32),
        compiler_params=_cparams("arbitrary", "arbitrary", "arbitrary"),
        name="band_bias_tiles",
    )(rel_bias)


_PROJ_OUTS = (
    ("aq", A_HEADS * HEAD_DIM, QK_SCALE),
    ("akk", 2 * HEAD_DIM, 1.0),
    ("aii", 2 * HEAD_DIM, 1.0),
    ("iq", IDX_HEADS * HEAD_DIM, QK_SCALE),
    ("bq0", B_SLOTS * HEAD_DIM, QK_SCALE),
    ("bq1", B_SLOTS * HEAD_DIM, QK_SCALE),
    ("bq2", B_SLOTS * HEAD_DIM, QK_SCALE),
    ("bk", B_SLOTS * HEAD_DIM, 1.0),
    ("bv", B_SLOTS * HEAD_DIM, 1.0),
    ("cq", C_HEADS * HEAD_DIM, QK_SCALE),
    ("ck", C_HEADS * HEAD_DIM, 1.0),
)
_PROJ_WIDTH = sum(w for _, w, _ in _PROJ_OUTS)
_PROJ_LANE_TILED = ("bq0", "bq1", "bq2", "bk", "bv")
_PROJ_OUTS_T = (
    ("avT", HEAD_DIM, BF),
    ("cvT", C_HEADS * HEAD_DIM, BF),
    ("iwT", BF16_ROWS, F32),
)
_PROJ_ROWS_T = sum(r for _, r, _ in _PROJ_OUTS_T)


def _pack_w_in(w):
    widths = (384, 64, 64, 512, 64, 8, 768, 256, 256, 384, 384, 384)
    offs = np.concatenate([[0], np.cumsum(widths)])
    aq, ak, av, iq, ik, iw, bq, bk, bv, cq, ck, cv = (w[:, offs[n]:offs[n + 1]] for n in range(12))
    cols = jnp.concatenate([aq, ak, ak, ik, ik, iq, bq, bk, bv, cq, ck], axis=1)
    iw_pad = jnp.concatenate([iw, jnp.zeros((w.shape[0], BF16_ROWS - IDX_HEADS), w.dtype)], axis=1)
    rows = jnp.concatenate([av, cv, iw_pad], axis=1).T
    return cols.astype(BF), rows.astype(BF)


def _proj_kernel(x_ref, w_ref, wt_ref, *o_refs):
    xb = x_ref[...].astype(BF)
    off = 0
    for o_ref, (_, width, scale) in zip(o_refs, _PROJ_OUTS):
        res = _dot(xb, w_ref[:, off:off + width])
        if scale != 1.0:
            res = res * scale
        if len(o_ref.shape) == 4:
            for lt in range(width // LANES):
                o_ref[0, lt] = res[:, lt * LANES:(lt + 1) * LANES]
        else:
            o_ref[...] = res.astype(o_ref.dtype)
        off += width
    off = 0
    for o_ref, (_, rows, _) in zip(o_refs[len(_PROJ_OUTS):], _PROJ_OUTS_T):
        o_ref[0] = _dot_nt(wt_ref[off:off + rows, :], xb).astype(o_ref.dtype)
        off += rows


def _project(x2d, w_cols, w_rows, tm, s_len):
    n = x2d.shape[0]
    tiles_per_seq = s_len // tm
    out_shape, out_specs = [], []
    for name, w, _ in _PROJ_OUTS:
        if name in _PROJ_LANE_TILED:
            out_shape.append(jax.ShapeDtypeStruct((n // s_len, w // LANES, s_len, LANES), F32))
            out_specs.append(pl.BlockSpec((1, w // LANES, tm, LANES),
                                          lambda i: (i // tiles_per_seq, 0, i % tiles_per_seq, 0)))
        else:
            out_shape.append(jax.ShapeDtypeStruct((n, w), BF))
            out_specs.append(pl.BlockSpec((tm, w), lambda i: (i, 0)))
    out_shape += [jax.ShapeDtypeStruct((n // s_len, r, s_len), dt) for _, r, dt in _PROJ_OUTS_T]
    out_specs += [pl.BlockSpec((1, r, tm), lambda i: (i // tiles_per_seq, 0, i % tiles_per_seq))
                  for _, r, _ in _PROJ_OUTS_T]
    outs = pl.pallas_call(
        _proj_kernel,
        grid=(n // tm,),
        in_specs=[pl.BlockSpec((tm, D_MODEL), lambda i: (i, 0)),
                  pl.BlockSpec((D_MODEL, _PROJ_WIDTH), lambda i: (0, 0)),
                  pl.BlockSpec((_PROJ_ROWS_T, D_MODEL), lambda i: (0, 0))],
        out_specs=out_specs,
        out_shape=out_shape,
        compiler_params=_cparams("parallel"),
        name="in_proj",
    )(x2d, w_cols, w_rows)
    names = [name for name, _, _ in _PROJ_OUTS] + [name for name, _, _ in _PROJ_OUTS_T]
    return dict(zip(names, outs))


def _head_pair_rhs(pair):
    lane = lax.broadcasted_iota(jnp.int32, pair.shape, 1)
    zero = jnp.zeros_like(pair)
    return jnp.concatenate([jnp.where(lane < HEAD_DIM, pair, zero), jnp.where(lane < HEAD_DIM, zero, pair)], axis=0)


ROW_CHUNK = 8 * SUBLANES


def _max_rows(x):
    chunk = ROW_CHUNK * (4 // x.dtype.itemsize)
    acc = x[:chunk]
    for c in range(1, x.shape[0] // chunk):
        acc = jnp.maximum(acc, x[c * chunk:(c + 1) * chunk])
    return jnp.max(acc.astype(F32), axis=0, keepdims=True)


def _softmax_pv(logits, v_t_ones):
    m = _max_rows(logits)
    e = jnp.exp(logits - m.astype(logits.dtype)).astype(BF)
    o = _dot(v_t_ones, e)
    return o[:HEAD_DIM] / o[HEAD_DIM:HEAD_DIM + 1]


def _with_ones_rows(v_t):
    return jnp.concatenate([v_t, jnp.ones((BF16_ROWS, v_t.shape[1]), v_t.dtype)], axis=0)


def _const_spec2(shape):
    return pl.BlockSpec(shape, lambda i, b: (0,) * len(shape), pipeline_mode=pl.Buffered(1))


def _grouped_tiles(call_group, out_shape):
    s_len = out_shape.shape[1]
    out = jnp.zeros(out_shape.shape, out_shape.dtype)
    for g in range(s_len // (TILES_PER_GROUP * TQ)):
        out = call_group(g, (g + 1) * TILES_PER_GROUP * TQ, out)
    return out


def _dsa_kernel(q_ref, iq_ref, iw_ref, kk_ref, ii_ref, vt_ref, bias_ref, _, o_ref, key_ref, half_ref, mask_ref, *,
                topk, tile0, s_len):
    i = tile0 + pl.program_id(0)
    n_keys, tq = key_ref.shape
    pos = lax.broadcasted_iota(jnp.int32, (n_keys, tq), 0)
    qry = i * tq + lax.broadcasted_iota(jnp.int32, (n_keys, tq), 1)

    if n_keys > topk:
        iq = iq_ref[0]
        iw = iw_ref[0]
        ii = ii_ref[0]
        index = jnp.zeros((n_keys, tq), F32)
        for p in range(IDX_HEADS // 2):
            sc = _dot_nt(ii, _head_pair_rhs(iq[:, p * LANES:(p + 1) * LANES]))
            index = index + iw[2 * p:2 * p + 1, :] * jnp.maximum(sc[:, :tq], 0.0)
            index = index + iw[2 * p + 1:2 * p + 2, :] * jnp.maximum(sc[:, tq:], 0.0)
        index = jnp.where(pos <= qry, index + 0.0, -jnp.inf)
        bits = pltpu.bitcast(index, jnp.int32)
        key_ref[...] = jnp.where(bits < 0, bits ^ jnp.int32(0x7FFFFFFF), bits)

        chunk_pos = lax.broadcasted_iota(jnp.int32, (ROW_CHUNK, tq), 0)

        def count(pred):
            acc = None
            for c in range(n_keys // ROW_CHUNK):
                hit = pred(key_ref[c * ROW_CHUNK:(c + 1) * ROW_CHUNK, :], chunk_pos + c * ROW_CHUNK)
                ones = jnp.where(hit, 1.0, 0.0)
                acc = ones if acc is None else acc + ones
            return jnp.sum(acc, axis=0, keepdims=True)

        def count_half(pred):
            acc = None
            for c in range(n_keys // (2 * ROW_CHUNK)):
                hit = pred(half_ref[c * 2 * ROW_CHUNK:(c + 1) * 2 * ROW_CHUNK, :])
                ones = jnp.where(hit, jnp.int16(1), jnp.int16(0))
                acc = ones if acc is None else acc + ones
            return jnp.sum(acc.astype(F32), axis=0, keepdims=True)

        def search_half(need, count_at_min):
            c0 = count_half(lambda h: h >= 0)
            t0 = jnp.where(c0 >= need, 0, HALF_MIN).astype(jnp.int32)
            n0 = jnp.where(c0 >= need, c0, count_at_min)

            def step(it, carry):
                t, n = carry
                cand = t | (jnp.int32(1) << (HALF_BITS - 2 - it))
                cand16 = cand.astype(jnp.int16)
                c = count_half(lambda h: h >= cand16)
                ok = c >= need
                return jnp.where(ok, cand, t), jnp.where(ok, c, n)

            return lax.fori_loop(0, HALF_BITS - 1, step, (t0, n0))

        k_f = float(topk)
        half_ref[...] = (key_ref[...] >> HALF_BITS).astype(jnp.int16)
        t_hi, cnt_ge_hi = search_half(k_f, float(n_keys))
        t_hi16 = t_hi.astype(jnp.int16)
        cnt_gt_hi = count_half(lambda h: h > t_hi16)
        key = key_ref[...]
        low = (key & (2 ** HALF_BITS - 1)) + HALF_MIN
        half_ref[...] = jnp.where((key >> HALF_BITS) == t_hi, low, HALF_MIN).astype(jnp.int16)
        t_lo, cnt_lo = search_half(k_f - cnt_gt_hi, cnt_ge_hi - cnt_gt_hi)
        thr = t_hi * 2 ** HALF_BITS + (t_lo - HALF_MIN)
        cnt_ge = cnt_gt_hi + cnt_lo

        surplus = cnt_ge - k_f
        max_surplus = jnp.max(surplus)

        def drop_highest_ties():
            def drop_round(r, cutoff):
                top = None
                for c in range(n_keys // ROW_CHUNK):
                    k = key_ref[c * ROW_CHUNK:(c + 1) * ROW_CHUNK, :]
                    kpos = chunk_pos + c * ROW_CHUNK
                    tie_pos = jnp.where((k == thr) & (kpos < cutoff), kpos, -1)
                    top = tie_pos if top is None else jnp.maximum(top, tie_pos)
                highest = jnp.max(top, axis=0, keepdims=True)
                return jnp.where(surplus > r.astype(F32), highest, cutoff)

            return lax.fori_loop(0, max_surplus.astype(jnp.int32), drop_round, jnp.full((1, tq), n_keys, jnp.int32))

        def search_cutoff():
            remaining = k_f - count(lambda k, _: k > thr)
            n_bits = (n_keys - 1).bit_length()

            def tie_step(it, last):
                cand = last | (jnp.int32(1) << (n_bits - 1 - it))
                c = count(lambda k, kpos: (k == thr) & (kpos < cand))
                return jnp.where(c < remaining, cand, last)

            return lax.fori_loop(0, n_bits, tie_step, jnp.zeros((1, tq), jnp.int32)) + 1

        def write_mask(cutoff):
            key = key_ref[...]
            mask_ref[...] = jnp.where((key > thr) | ((key == thr) & (pos < cutoff)), 0.0, NEG)

        @pl.when(max_surplus == 0.0)
        def _():
            mask_ref[...] = jnp.where(key_ref[...] >= thr, 0.0, NEG)

        @pl.when((max_surplus > 0.0) & (max_surplus <= FEW_TIES))
        def _():
            write_mask(drop_highest_ties())

        @pl.when(max_surplus > FEW_TIES)
        def _():
            write_mask(search_cutoff())
    else:
        mask_ref[...] = jnp.zeros_like(mask_ref)

    q = q_ref[0]
    kk = kk_ref[0]
    v_t = _with_ones_rows(vt_ref[0])
    outs = []
    for p in range(A_HEADS // 2):
        logits = _dot_nt(kk, _head_pair_rhs(q[:, p * LANES:(p + 1) * LANES]))
        for half in range(2):
            bias = _causal_bias(bias_ref, 2 * p + half, i * tq, 0, n_keys, tq, s_len)
            lg = logits[:, half * tq:(half + 1) * tq] + bias + mask_ref[...]
            outs.append(_softmax_pv(lg, v_t))
    o_ref[0] = jnp.concatenate(outs, axis=0).T.astype(o_ref.dtype)


def _dsa(aq, iq, iw_t, akk, aii, av_t, bias_master):
    bsz, s_len, _ = aq.shape
    topk = min(A_TOPK_MAX, s_len // 4)
    out_shape = jax.ShapeDtypeStruct(aq.shape, BF)

    def call_group(g, n_keys, out):
        tile0 = g * TILES_PER_GROUP
        q_spec = lambda w: pl.BlockSpec((1, TQ, w), lambda i, b: (b, tile0 + i, 0))
        keys = lambda w: pl.BlockSpec((1, n_keys, w), lambda i, b: (b, 0, 0))
        return pl.pallas_call(
            functools.partial(_dsa_kernel, topk=topk, tile0=tile0, s_len=s_len),
            grid=(TILES_PER_GROUP, bsz),
            in_specs=[q_spec(A_HEADS * HEAD_DIM), q_spec(IDX_HEADS * HEAD_DIM),
                      pl.BlockSpec((1, BF16_ROWS, TQ), lambda i, b: (b, 0, tile0 + i)),
                      keys(2 * HEAD_DIM), keys(2 * HEAD_DIM),
                      pl.BlockSpec((1, HEAD_DIM, n_keys), lambda i, b: (b, 0, 0)),
                      _const_spec2(bias_master.shape), pl.BlockSpec(memory_space=pl.ANY)],
            out_specs=q_spec(A_HEADS * HEAD_DIM),
            out_shape=out_shape,
            input_output_aliases={7: 0},
            scratch_shapes=[pltpu.VMEM((n_keys, TQ), jnp.int32), pltpu.VMEM((n_keys, TQ), jnp.int16),
                            pltpu.VMEM((n_keys, TQ), F32)],
            compiler_params=_cparams("arbitrary", "arbitrary"),
            name=f"dsa_mixer_{g}",
        )(aq, iq, iw_t, akk, aii, av_t, bias_master, out)

    return _grouped_tiles(call_group, out_shape)


_PAIR_W = 2 * HEAD_DIM
_N_PAIRS = B_SLOTS // 2
_N_STATE = 2 * _N_PAIRS


def _class_rows(r, t, dil):
    start = r + t * BAND * dil
    return pl.ds(start, BAND, stride=dil) if dil > 1 else pl.ds(start, BAND)


def _pair_ones(n_keys):
    row = lax.broadcasted_iota(jnp.int32, (2 * n_keys, _PAIR_W), 0)
    lane = lax.broadcasted_iota(jnp.int32, (2 * n_keys, _PAIR_W), 1)
    return jnp.where((row < n_keys) == (lane < HEAD_DIM), 1.0, 0.0).astype(BF)


def _band_pairs(units, ones_bd):
    n_keys = units[0][1].shape[0]
    lane = lax.broadcasted_iota(jnp.int32, (BAND, _PAIR_W), 1)
    logits = [_dot_nt(q, _head_pair_rhs(k)) + bias for q, k, _, bias in units]
    maxes = [(jnp.max(lg[:, :n_keys], axis=1, keepdims=True), jnp.max(lg[:, n_keys:], axis=1, keepdims=True))
             for lg in logits]
    probs = [jnp.concatenate([jnp.exp(lg[:, :n_keys] - ma), jnp.exp(lg[:, n_keys:] - mb)], axis=1).astype(BF)
             for lg, (ma, mb) in zip(logits, maxes)]
    results = []
    for (_, _, v, _), e, (ma, mb) in zip(units, probs, maxes):
        acc = _dot(e, _head_pair_rhs(v))
        den = _dot(e, ones_bd)
        results.append((acc / den, jnp.where(lane < HEAD_DIM, ma, mb) + jnp.log(den)))
    return results


def _band_state_kernel(q_ref, k_ref, v_ref, kp_ref, vp_ref, bias_ref, st_ref, *, dil):
    c = pl.program_id(1)
    n_tiles = q_ref.shape[2] // (BAND * dil)
    first_variant = jnp.where(c == 0, 1, 0)
    load = lambda ref, p, rows: ref[0, p, rows, :].astype(BF)
    units, where = [], []
    for r in range(dil):
        for t in range(n_tiles):
            rows = _class_rows(r, t, dil)
            for p in range(_N_PAIRS):
                if t == 0:
                    prev = _class_rows(r, 0, dil)
                    k = jnp.concatenate([load(kp_ref, p, prev), load(k_ref, p, rows)], axis=0)
                    v = jnp.concatenate([load(vp_ref, p, prev), load(v_ref, p, rows)], axis=0)
                    bias = bias_ref[0, first_variant, p]
                else:
                    prev = _class_rows(r, t - 1, dil)
                    k = jnp.concatenate([load(k_ref, p, prev), load(k_ref, p, rows)], axis=0)
                    v = jnp.concatenate([load(v_ref, p, prev), load(v_ref, p, rows)], axis=0)
                    bias = bias_ref[0, 0, p]
                units.append((load(q_ref, p, rows), k, v, bias))
                where.append((rows, p))
    for (rows, p), (out, lse) in zip(where, _band_pairs(units, _pair_ones(2 * BAND))):
        st_ref[0, p, rows, :] = out
        st_ref[0, _N_PAIRS + p, rows, :] = lse


def _band_state(q, k, v, bias_tiles, g, dil):
    bsz, n_lt, s_len, _ = q.shape
    rows = 4 * BAND
    prev_rows = BAND * dil
    per = rows // prev_rows
    main = pl.BlockSpec((1, n_lt, rows, LANES), lambda b, c: (b, 0, c, 0))
    prev = pl.BlockSpec((1, n_lt, prev_rows, LANES), lambda b, c: (b, 0, jnp.maximum(c * per - 1, 0), 0))
    return pl.pallas_call(
        functools.partial(_band_state_kernel, dil=dil),
        grid=(bsz, s_len // rows),
        in_specs=[main, main, main, prev, prev,
                  pl.BlockSpec((1, 2, _N_PAIRS, BAND, 4 * BAND), lambda b, c: (g, 0, 0, 0, 0))],
        out_specs=pl.BlockSpec((1, _N_STATE, rows, LANES), lambda b, c: (b, 0, c, 0)),
        out_shape=jax.ShapeDtypeStruct((bsz, _N_STATE, s_len, LANES), F32),
        compiler_params=_cparams("parallel", "arbitrary"),
        name=f"band_state_{g}",
    )(q, k, v, k, v, bias_tiles)


_DEINTERLEAVE = 4


def _band_merge_kernel(q_ref, k_ref, v_ref, st1_ref, st2_ref, bias_ref, o_ref, in_scr, out_scr, *, dil):
    inner = dil // _DEINTERLEAVE
    n_stage = q_ref.shape[2] // _DEINTERLEAVE
    sources = [(q_ref, p) for p in range(_N_PAIRS)] + [(k_ref, p) for p in range(_N_PAIRS)] \
        + [(v_ref, p) for p in range(_N_PAIRS)] + [(st1_ref, j) for j in range(_N_STATE)] \
        + [(st2_ref, j) for j in range(_N_STATE)]
    q0, k0, v0, s1, s2 = 0, _N_PAIRS, 2 * _N_PAIRS, 3 * _N_PAIRS, 3 * _N_PAIRS + _N_STATE
    own_bias = [jnp.concatenate([bias_ref[0, 0, p, :, BAND:2 * BAND], bias_ref[0, 0, p, :, 3 * BAND:]], axis=1)
                for p in range(_N_PAIRS)]
    ones_bd = _pair_ones(BAND)
    for r_outer in range(_DEINTERLEAVE):
        for n, (ref, idx) in enumerate(sources):
            in_scr[n] = ref[0, idx, pl.ds(r_outer, n_stage, stride=_DEINTERLEAVE), :]
        rows_of = [pl.ds(r_inner, BAND, stride=inner) for r_inner in range(inner)]
        load = lambda n, rows: in_scr[n, rows, :].astype(BF)
        units, where = [], []
        for rows in rows_of:
            for p in range(_N_PAIRS):
                units.append((load(q0 + p, rows), load(k0 + p, rows), load(v0 + p, rows), own_bias[p]))
                where.append((rows, p))
        for (rows, p), (out3, lse3) in zip(where, _band_pairs(units, ones_bd)):
            lse1, lse2 = in_scr[s1 + _N_PAIRS + p, rows, :], in_scr[s2 + _N_PAIRS + p, rows, :]
            top = jnp.maximum(jnp.maximum(lse1, lse2), lse3)
            w1, w2, w3 = jnp.exp(lse1 - top), jnp.exp(lse2 - top), jnp.exp(lse3 - top)
            num = w1 * in_scr[s1 + p, rows, :] + w2 * in_scr[s2 + p, rows, :] + w3 * out3
            out_scr[p, rows, :] = num / (w1 + w2 + w3)
        for p in range(_N_PAIRS):
            o_ref[0, p, pl.ds(r_outer, n_stage, stride=_DEINTERLEAVE), :] = out_scr[p]


def _band_merge(q, k, v, st1, st2, bias_tiles, g, dil):
    bsz, n_lt, s_len, _ = q.shape
    assert s_len == BAND * dil and dil % _DEINTERLEAVE == 0
    full = lambda n: pl.BlockSpec((1, n, s_len, LANES), lambda b: (b, 0, 0, 0))
    n_stage = s_len // _DEINTERLEAVE
    return pl.pallas_call(
        functools.partial(_band_merge_kernel, dil=dil),
        grid=(bsz,),
        in_specs=[full(n_lt), full(n_lt), full(n_lt), full(_N_STATE), full(_N_STATE),
                  pl.BlockSpec((1, 2, _N_PAIRS, BAND, 4 * BAND), lambda b: (g, 0, 0, 0, 0))],
        out_specs=full(n_lt),
        out_shape=jax.ShapeDtypeStruct((bsz, n_lt, s_len, LANES), F32),
        scratch_shapes=[pltpu.VMEM((3 * _N_PAIRS + 2 * _N_STATE, n_stage, LANES), F32),
                        pltpu.VMEM((_N_PAIRS, n_stage, LANES), F32)],
        compiler_params=_cparams("parallel"),
        name="band_merge",
    )(q, k, v, st1, st2, bias_tiles)


def _dilated(bq, bk, bv, bias_tiles):
    (_, d0), (_, d1), (_, d2) = B_PATTERNS
    st0 = _band_state(bq[0], bk, bv, bias_tiles, 0, d0)
    st1 = _band_state(bq[1], bk, bv, bias_tiles, 1, d1)
    return _band_merge(bq[2], bk, bv, st0, st1, bias_tiles, 2, d2)


def _block_mean_kernel(k_ref, o_ref):
    n_blk = k_ref.shape[1] // C_BLOCK
    o_ref[...] = jnp.zeros_like(o_ref)
    for j in range(n_blk):
        blk = k_ref[0, j * C_BLOCK:(j + 1) * C_BLOCK, :].astype(F32)
        o_ref[0, j:j + 1, :] = (jnp.sum(blk, axis=0, keepdims=True) * (1.0 / C_BLOCK)).astype(o_ref.dtype)


def _block_means(ck):
    bsz, s_len, w = ck.shape
    assert s_len // C_BLOCK <= BF16_ROWS
    return pl.pallas_call(
        _block_mean_kernel,
        grid=(bsz,),
        in_specs=[pl.BlockSpec((1, s_len, w), lambda b: (b, 0, 0))],
        out_specs=pl.BlockSpec((1, BF16_ROWS, w), lambda b: (b, 0, 0)),
        out_shape=jax.ShapeDtypeStruct((bsz, BF16_ROWS, w), BF),
        compiler_params=_cparams("parallel"),
        name="moba_block_means",
    )(ck)


def _moba_kernel(q_ref, k_ref, vt_ref, kmean_ref, bias_ref, _, o_ref, *, cur, s_len):
    tq = q_ref.shape[1]
    t0 = (cur * TILES_PER_GROUP + pl.program_id(0)) * tq
    q = q_ref[0]
    kmean = kmean_ref[0]
    blk = lax.broadcasted_iota(jnp.int32, (BF16_ROWS, tq), 0)
    outs = []
    for p in range(C_HEADS // 2):
        ps = slice(p * LANES, (p + 1) * LANES)
        rhs = _head_pair_rhs(q[:, ps])
        logits = _dot_nt(k_ref[0, :, ps], rhs).astype(BF)
        gates = _dot_nt(kmean[:, ps], rhs)
        for half in range(2):
            h = 2 * p + half
            hq = slice(half * tq, (half + 1) * tq)
            if cur > C_TOPK:
                gate = jnp.where(blk < cur, gates[:, hq], -jnp.inf)
                off = jnp.where(blk == cur, 0.0, NEG)
                for j in range(cur):
                    gj = gate[j:j + 1, :]
                    beats = (gate > gj) | ((gate == gj) & (blk < j))
                    rank = jnp.sum(jnp.where(beats, 1.0, 0.0), axis=0, keepdims=True)
                    off = jnp.where((blk == j) & (rank < float(C_TOPK)), 0.0, off)
            pieces = []
            for j in range(cur + 1):
                ks = slice(j * C_BLOCK, (j + 1) * C_BLOCK)
                piece = logits[ks, hq] + _causal_bias(bias_ref, h, t0, j * C_BLOCK, C_BLOCK, tq, s_len)
                if cur > C_TOPK:
                    piece = piece + off[j:j + 1, :].astype(BF)
                pieces.append(piece)
            lg = jnp.concatenate(pieces, axis=0)
            outs.append(_softmax_pv(lg, _with_ones_rows(vt_ref[0, h * HEAD_DIM:(h + 1) * HEAD_DIM, :])))
    o_ref[0] = jnp.concatenate(outs, axis=0).T.astype(o_ref.dtype)


def _moba(cq, ck, cv_t, bias_master):
    bsz, s_len, w = cq.shape
    assert TILES_PER_GROUP * TQ == C_BLOCK
    kmean = _block_means(ck)
    out_shape = jax.ShapeDtypeStruct(cq.shape, BF)

    def call_group(g, n_keys, out):
        tile0 = g * TILES_PER_GROUP
        return pl.pallas_call(
            functools.partial(_moba_kernel, cur=g, s_len=s_len),
            grid=(TILES_PER_GROUP, bsz),
            in_specs=[pl.BlockSpec((1, TQ, w), lambda i, b: (b, tile0 + i, 0)),
                      pl.BlockSpec((1, n_keys, w), lambda i, b: (b, 0, 0)),
                      pl.BlockSpec((1, w, n_keys), lambda i, b: (b, 0, 0)),
                      pl.BlockSpec((1, BF16_ROWS, w), lambda i, b: (b, 0, 0)),
                      _const_spec2(bias_master.shape), pl.BlockSpec(memory_space=pl.ANY)],
            out_specs=pl.BlockSpec((1, TQ, w), lambda i, b: (b, tile0 + i, 0)),
            out_shape=out_shape,
            input_output_aliases={5: 0},
            compiler_params=_cparams("arbitrary", "arbitrary"),
            name=f"moba_mixer_{g}",
        )(cq, ck, cv_t, kmean, bias_master, out)

    return _grouped_tiles(call_group, out_shape)


def _layer_norm(y, g_ref, b_ref):
    mu = jnp.mean(y, axis=-1, keepdims=True)
    yc = y - mu
    var = jnp.mean(yc * yc, axis=-1, keepdims=True)
    return yc * lax.rsqrt(var + LN_EPS) * g_ref[...] + b_ref[...]


def _merge_kernel(x_ref, oa_ref, ob_ref, oc_ref, wg_ref, wa_ref, wb_ref, wc_ref, wo_ref, g_ref, b_ref, y_ref):
    x = x_ref[...]
    xb = x.astype(BF)
    merged = None
    o_b = jnp.concatenate([ob_ref[0, lt] for lt in range(ob_ref.shape[1])], axis=1).astype(BF)
    for n, (o, w_ref) in enumerate(((oa_ref[...], wa_ref), (o_b, wb_ref), (oc_ref[...], wc_ref))):
        gate = jax.nn.sigmoid(_dot(xb, wg_ref[:, n * D_MODEL:(n + 1) * D_MODEL]))
        term = gate * _dot(o, w_ref[...])
        merged = term if merged is None else merged + term
    y = ALPHA * x + _dot(merged.astype(BF), wo_ref[...])
    y_ref[...] = _layer_norm(y, g_ref, b_ref)


def _const_spec(shape):
    return pl.BlockSpec(shape, lambda i: (0,) * len(shape), pipeline_mode=pl.Buffered(1))


def _merge(x2d, oa, ob, oc, wg, wa, wb, wc, wo, ln_g, ln_b, tm):
    n = x2d.shape[0]
    tiles_per_seq = ob.shape[2] // tm
    rows = lambda w: pl.BlockSpec((tm, w), lambda i: (i, 0))
    ob_spec = pl.BlockSpec((1, ob.shape[1], tm, LANES), lambda i: (i // tiles_per_seq, 0, i % tiles_per_seq, 0))
    return pl.pallas_call(
        _merge_kernel,
        grid=(n // tm,),
        in_specs=[rows(D_MODEL), rows(oa.shape[1]), ob_spec, rows(oc.shape[1]),
                  _const_spec(wg.shape), _const_spec(wa.shape), _const_spec(wb.shape), _const_spec(wc.shape),
                  _const_spec(wo.shape), _const_spec(ln_g.shape), _const_spec(ln_b.shape)],
        out_specs=rows(D_MODEL),
        out_shape=jax.ShapeDtypeStruct((n, D_MODEL), F32),
        compiler_params=_cparams("parallel"),
        name="merge_out_ln",
    )(x2d, oa, ob, oc, wg, wa, wb, wc, wo, ln_g, ln_b)


_FF_CHUNK = 1024


def _ffn_kernel(x_ref, p_ref, wu_ref, wd_ref, wpg_ref, wp_ref, g_ref, b_ref, y_ref):
    x = x_ref[...]
    xb = x.astype(BF)
    y = ALPHA * x + jax.nn.sigmoid(_dot(xb, wpg_ref[...])) * _dot(p_ref[...].astype(BF), wp_ref[...])
    for c in range(D_FF // _FF_CHUNK):
        cs = slice(c * _FF_CHUNK, (c + 1) * _FF_CHUNK)
        u = jnp.maximum(_dot(xb, wu_ref[:, cs]), 0.0)
        y = y + _dot((u * u).astype(BF), wd_ref[cs, :])
    y_ref[...] = _layer_norm(y, g_ref, b_ref)


def _ffn(x2d, p2d, wu, wd, wpg, wp, ln_g, ln_b, tm):
    n = x2d.shape[0]
    rows = lambda w: pl.BlockSpec((tm, w), lambda i: (i, 0))
    return pl.pallas_call(
        _ffn_kernel,
        grid=(n // tm,),
        in_specs=[rows(D_MODEL), rows(PLE_DIM), _const_spec(wu.shape), _const_spec(wd.shape),
                  _const_spec(wpg.shape), _const_spec(wp.shape), _const_spec(ln_g.shape), _const_spec(ln_b.shape)],
        out_specs=rows(D_MODEL),
        out_shape=jax.ShapeDtypeStruct((n, D_MODEL), F32),
        compiler_params=_cparams("parallel"),
        name="ffn_ple_ln",
    )(x2d, p2d, wu, wd, wpg, wp, ln_g, ln_b)


def kernel(x, p, w_in, w_gate, w_br_a, w_br_b, w_br_c, w_out, ln1_g, ln1_b,
           w_up, w_down, w_ple_gate, w_ple, ln2_g, ln2_b, rel_bias):
    bsz, s_len, d_model = x.shape
    assert d_model == D_MODEL and s_len == MAX_DISTANCE, (x.shape,)
    n_tok = bsz * s_len
    tm = 512

    b_head0 = A_HEADS
    c_head0 = A_HEADS + B_GROUPS * B_SLOTS
    bias_a = _causal_bias_master(rel_bias, 0, A_HEADS, s_len, F32)
    bias_b = _band_bias_tiles(rel_bias, b_head0)
    bias_c = _causal_bias_master(rel_bias, c_head0, C_HEADS, s_len, BF)

    x2d = x.reshape(n_tok, D_MODEL)
    for i in range(DEPTH):
        pr = _project(x2d, *_pack_w_in(w_in[i]), tm, s_len)
        seq = lambda name: pr[name].reshape(bsz, s_len, -1)
        o_a = _dsa(seq("aq"), seq("iq"), pr["iwT"], seq("akk"), seq("aii"), pr["avT"], bias_a)
        o_b = _dilated((pr["bq0"], pr["bq1"], pr["bq2"]), pr["bk"], pr["bv"], bias_b)
        o_c = _moba(seq("cq"), seq("ck"), pr["cvT"], bias_c)
        flat = lambda a: a.reshape(n_tok, -1)
        row = lambda a: a.reshape(1, D_MODEL)
        x2d = _merge(x2d, flat(o_a), o_b, flat(o_c), w_gate[i].astype(BF), w_br_a[i].astype(BF),
                     w_br_b[i].astype(BF), w_br_c[i].astype(BF), w_out[i].astype(BF),
                     row(ln1_g[i]), row(ln1_b[i]), tm)
        x2d = _ffn(x2d, p[i].reshape(n_tok, PLE_DIM), w_up[i].astype(BF), w_down[i].astype(BF),
                   w_ple_gate[i].astype(BF), w_ple[i].astype(BF), row(ln2_g[i]), row(ln2_b[i]), tm)
    return x2d.reshape(bsz, s_len, D_MODEL)
```

```python
import functools
import math

import numpy as np
import jax
import jax.numpy as jnp
from jax import lax
from jax.experimental import pallas as pl
from jax.experimental.pallas import tpu as pltpu

D_MODEL = 1024
HEAD_DIM = 64
A_HEADS = 6
IDX_HEADS = 8
A_TOPK_MAX = 256
B_SLOTS = 4
B_PATTERNS = ((128, 1), (512, 4), (2048, 16))
B_GROUPS = 3
C_HEADS = 6
C_BLOCK = 256
C_TOPK = 3
N_BUCKETS = 32
MAX_DISTANCE = 2048
D_FF = 4 * D_MODEL
PLE_DIM = 256
DEPTH = 2
ALPHA = (2 * DEPTH) ** 0.25
LN_EPS = 1e-5
NEG = -1e30
QK_SCALE = HEAD_DIM ** -0.5

LANES = 128
SUBLANES = 8
BF16_ROWS = 16
VMEM_LIMIT = 56 * 1024 * 1024

TQ = 256
BIAS_TQ = LANES
TILES_PER_GROUP = C_BLOCK // TQ
BAND = 128
BF = jnp.bfloat16
F32 = jnp.float32
HALF_BITS = 16
HALF_MIN = -2 ** (HALF_BITS - 1)
FEW_TIES = 8.0

_NT = (((1,), (1,)), ((), ()))


def _dot(a, b):
    return jnp.dot(a, b, preferred_element_type=F32)


def _dot_nt(a, b):
    return lax.dot_general(a, b, _NT, preferred_element_type=F32)


def _cparams(*sem):
    return pltpu.CompilerParams(dimension_semantics=sem, vmem_limit_bytes=VMEM_LIMIT)


def _bucket_starts():
    d = np.arange(0, MAX_DISTANCE + 1)
    max_exact = N_BUCKETS // 2
    nf = np.maximum(d, 1).astype(np.float32)
    large = max_exact + (np.log(nf / np.float32(max_exact)) / np.float32(math.log(MAX_DISTANCE / max_exact))
                         * np.float32(N_BUCKETS - max_exact)).astype(np.int32)
    bucket = np.where(d < max_exact, d, np.minimum(large, N_BUCKETS - 1))
    return [int(np.argmax(bucket >= b)) if np.any(bucket >= b) else None for b in range(N_BUCKETS)]


_BUCKET_START = _bucket_starts()


def _bias_from_distance(dist, tab_ref, col):
    val = jnp.full(dist.shape, tab_ref[0, col], F32)
    for b in range(1, N_BUCKETS):
        if _BUCKET_START[b] is not None:
            val = jnp.where(dist >= _BUCKET_START[b], tab_ref[b, col], val)
    return jnp.where(dist < 0, NEG, val)


def _causal_bias_kernel(tab_ref, o_ref, *, head0):
    h = pl.program_id(0)
    n_rows, tq = o_ref.shape[1], o_ref.shape[2]
    u = lax.broadcasted_iota(jnp.int32, (n_rows, tq), 0)
    t = lax.broadcasted_iota(jnp.int32, (n_rows, tq), 1)
    o_ref[0] = _bias_from_distance(t + (n_rows - tq) // 2 - u, tab_ref, head0 + h).astype(o_ref.dtype)


def _causal_bias_master(rel_bias, head0, n_heads, s_len, dtype):
    n_rows = 2 * s_len - BIAS_TQ
    return pl.pallas_call(
        functools.partial(_causal_bias_kernel, head0=head0),
        grid=(n_heads,),
        in_specs=[pl.BlockSpec(memory_space=pltpu.SMEM)],
        out_specs=pl.BlockSpec((1, n_rows, BIAS_TQ), lambda h: (h, 0, 0)),
        out_shape=jax.ShapeDtypeStruct((n_heads, n_rows, BIAS_TQ), dtype),
        compiler_params=_cparams("arbitrary"),
        name="causal_bias_master",
    )(rel_bias)


def _bias_row0(t0, s_len):
    return pl.multiple_of(s_len - BIAS_TQ - t0, BIAS_TQ)


def _causal_bias(bias_ref, h, t0, first_key, n_keys, tq, s_len):
    parts = [bias_ref[h, pl.ds(_bias_row0(t0 + c * BIAS_TQ, s_len) + first_key, n_keys), :]
             for c in range(tq // BIAS_TQ)]
    return parts[0] if len(parts) == 1 else jnp.concatenate(parts, axis=1)


def _band_bias_kernel(tab_ref, o_ref, *, head0):
    g = pl.program_id(0)
    variant = pl.program_id(1)
    pair = pl.program_id(2)
    row = lax.broadcasted_iota(jnp.int32, (BAND, 2 * BAND), 0)
    col = lax.broadcasted_iota(jnp.int32, (BAND, 2 * BAND), 1)
    j = row + BAND - col
    no_prev = (variant == 1) & (col < BAND)
    for gi, (_, dil) in enumerate(B_PATTERNS):
        @pl.when(g == gi)
        def _(dil=dil):
            for half in range(2):
                bias = _bias_from_distance(j * dil, tab_ref, head0 + g * B_SLOTS + 2 * pair + half)
                bias = jnp.where((j > BAND) | no_prev, NEG, bias)
                o_ref[0, 0, 0, :, half * 2 * BAND:(half + 1) * 2 * BAND] = bias


def _band_bias_tiles(rel_bias, head0):
    return pl.pallas_call(
        functools.partial(_band_bias_kernel, head0=head0),
        grid=(B_GROUPS, 2, B_SLOTS // 2),
        in_specs=[pl.BlockSpec(memory_space=pltpu.SMEM)],
        out_specs=pl.BlockSpec((1, 1, 1, BAND, 4 * BAND), lambda g, v, p: (g, v, p, 0, 0)),
        out_shape=jax.ShapeDtypeStruct((B_GROUPS, 2, B_SLOTS // 2, BAND, 4 * BAND), F32),
        compiler_params=_cparams("arbitrary", "arbitrary", "arbitrary"),
        name="band_bias_tiles",
    )(rel_bias)


_PROJ_OUTS = (
    ("aq", A_HEADS * HEAD_DIM, QK_SCALE),
    ("akk", 2 * HEAD_DIM, 1.0),
    ("aii", 2 * HEAD_DIM, 1.0),
    ("iq", IDX_HEADS * HEAD_DIM, QK_SCALE),
    ("bq0", B_SLOTS * HEAD_DIM, QK_SCALE),
    ("bq1", B_SLOTS * HEAD_DIM, QK_SCALE),
    ("bq2", B_SLOTS * HEAD_DIM, QK_SCALE),
    ("bk", B_SLOTS * HEAD_DIM, 1.0),
    ("bv", B_SLOTS * HEAD_DIM, 1.0),
    ("cq", C_HEADS * HEAD_DIM, QK_SCALE),
    ("ck", C_HEADS * HEAD_DIM, 1.0),
)
_PROJ_WIDTH = sum(w for _, w, _ in _PROJ_OUTS)
_PROJ_LANE_TILED = ("bq0", "bq1", "bq2", "bk", "bv")
_PROJ_OUTS_T = (
    ("avT", HEAD_DIM, BF),
    ("cvT", C_HEADS * HEAD_DIM, BF),
    ("iwT", BF16_ROWS, F32),
)
_PROJ_ROWS_T = sum(r for _, r, _ in _PROJ_OUTS_T)


def _pack_w_in(w):
    widths = (384, 64, 64, 512, 64, 8, 768, 256, 256, 384, 384, 384)
    offs = np.concatenate([[0], np.cumsum(widths)])
    aq, ak, av, iq, ik, iw, bq, bk, bv, cq, ck, cv = (w[:, offs[n]:offs[n + 1]] for n in range(12))
    cols = jnp.concatenate([aq, ak, ak, ik, ik, iq, bq, bk, bv, cq, ck], axis=1)
    iw_pad = jnp.concatenate([iw, jnp.zeros((w.shape[0], BF16_ROWS - IDX_HEADS), w.dtype)], axis=1)
    rows = jnp.concatenate([av, cv, iw_pad], axis=1).T
    return cols.astype(BF), rows.astype(BF)


def _proj_kernel(x_ref, w_ref, wt_ref, *o_refs):
    xb = x_ref[...].astype(BF)
    off = 0
    for o_ref, (_, width, scale) in zip(o_refs, _PROJ_OUTS):
        res = _dot(xb, w_ref[:, off:off + width])
        if scale != 1.0:
            res = res * scale
        if len(o_ref.shape) == 4:
            for lt in range(width // LANES):
                o_ref[0, lt] = res[:, lt * LANES:(lt + 1) * LANES]
        else:
            o_ref[...] = res.astype(o_ref.dtype)
        off += width
    off = 0
    for o_ref, (_, rows, _) in zip(o_refs[len(_PROJ_OUTS):], _PROJ_OUTS_T):
        o_ref[0] = _dot_nt(wt_ref[off:off + rows, :], xb).astype(o_ref.dtype)
        off += rows


def _project(x2d, w_cols, w_rows, tm, s_len):
    n = x2d.shape[0]
    tiles_per_seq = s_len // tm
    out_shape, out_specs = [], []
    for name, w, _ in _PROJ_OUTS:
        if name in _PROJ_LANE_TILED:
            out_shape.append(jax.ShapeDtypeStruct((n // s_len, w // LANES, s_len, LANES), F32))
            out_specs.append(pl.BlockSpec((1, w // LANES, tm, LANES),
                                          lambda i: (i // tiles_per_seq, 0, i % tiles_per_seq, 0)))
        else:
            out_shape.append(jax.ShapeDtypeStruct((n, w), BF))
            out_specs.append(pl.BlockSpec((tm, w), lambda i: (i, 0)))
    out_shape += [jax.ShapeDtypeStruct((n // s_len, r, s_len), dt) for _, r, dt in _PROJ_OUTS_T]
    out_specs += [pl.BlockSpec((1, r, tm), lambda i: (i // tiles_per_seq, 0, i % tiles_per_seq))
                  for _, r, _ in _PROJ_OUTS_T]
    outs = pl.pallas_call(
        _proj_kernel,
        grid=(n // tm,),
        in_specs=[pl.BlockSpec((tm, D_MODEL), lambda i: (i, 0)),
                  pl.BlockSpec((D_MODEL, _PROJ_WIDTH), lambda i: (0, 0)),
                  pl.BlockSpec((_PROJ_ROWS_T, D_MODEL), lambda i: (0, 0))],
        out_specs=out_specs,
        out_shape=out_shape,
        compiler_params=_cparams("parallel"),
        name="in_proj",
    )(x2d, w_cols, w_rows)
    names = [name for name, _, _ in _PROJ_OUTS] + [name for name, _, _ in _PROJ_OUTS_T]
    return dict(zip(names, outs))


def _head_pair_rhs(pair):
    lane = lax.broadcasted_iota(jnp.int32, pair.shape, 1)
    zero = jnp.zeros_like(pair)
    return jnp.concatenate([jnp.where(lane < HEAD_DIM, pair, zero), jnp.where(lane < HEAD_DIM, zero, pair)], axis=0)


ROW_CHUNK = 8 * SUBLANES


def _max_rows(x):
    chunk = ROW_CHUNK * (4 // x.dtype.itemsize)
    acc = x[:chunk]
    for c in range(1, x.shape[0] // chunk):
        acc = jnp.maximum(acc, x[c * chunk:(c + 1) * chunk])
    return jnp.max(acc.astype(F32), axis=0, keepdims=True)


def _softmax_pv(logits, v_t_ones):
    m = _max_rows(logits)
    e = jnp.exp(logits - m.astype(logits.dtype)).astype(BF)
    o = _dot(v_t_ones, e)
    return o[:HEAD_DIM] / o[HEAD_DIM:HEAD_DIM + 1]


def _with_ones_rows(v_t):
    return jnp.concatenate([v_t, jnp.ones((BF16_ROWS, v_t.shape[1]), v_t.dtype)], axis=0)


def _const_spec2(shape):
    return pl.BlockSpec(shape, lambda i, b: (0,) * len(shape), pipeline_mode=pl.Buffered(1))


BATCH_PER_STEP = 2


def _batch_per_step(bsz):
    return BATCH_PER_STEP if bsz % BATCH_PER_STEP == 0 else 1


def _one_batch(ref, bb):
    return ref.at[pl.ds(bb, 1)]


def _grouped_tiles(call_group, out_shape):
    s_len = out_shape.shape[1]
    out = jnp.zeros(out_shape.shape, out_shape.dtype)
    for g in range(s_len // (TILES_PER_GROUP * TQ)):
        out = call_group(g, (g + 1) * TILES_PER_GROUP * TQ, out)
    return out


def _dsa_kernel(q_ref, iq_ref, iw_ref, kk_ref, ii_ref, vt_ref, bias_ref, _, o_ref, key_ref, half_ref, mask_ref, **static):
    for bb in range(q_ref.shape[0]):
        _dsa_tile(*(_one_batch(r, bb) for r in (q_ref, iq_ref, iw_ref, kk_ref, ii_ref, vt_ref)), bias_ref,
                  _one_batch(o_ref, bb), key_ref, half_ref, mask_ref, **static)


def _dsa_tile(q_ref, iq_ref, iw_ref, kk_ref, ii_ref, vt_ref, bias_ref, o_ref, key_ref, half_ref, mask_ref, *,
              topk, tile0, s_len):
    i = tile0 + pl.program_id(0)
    n_keys, tq = key_ref.shape
    pos = lax.broadcasted_iota(jnp.int32, (n_keys, tq), 0)
    qry = i * tq + lax.broadcasted_iota(jnp.int32, (n_keys, tq), 1)

    if n_keys > topk:
        iq = iq_ref[0]
        iw = iw_ref[0]
        ii = ii_ref[0]
        index = jnp.zeros((n_keys, tq), F32)
        for p in range(IDX_HEADS // 2):
            sc = _dot_nt(ii, _head_pair_rhs(iq[:, p * LANES:(p + 1) * LANES]))
            index = index + iw[2 * p:2 * p + 1, :] * jnp.maximum(sc[:, :tq], 0.0)
            index = index + iw[2 * p + 1:2 * p + 2, :] * jnp.maximum(sc[:, tq:], 0.0)
        index = jnp.where(pos <= qry, index + 0.0, -jnp.inf)
        bits = pltpu.bitcast(index, jnp.int32)
        key_ref[...] = jnp.where(bits < 0, bits ^ jnp.int32(0x7FFFFFFF), bits)

        chunk_pos = lax.broadcasted_iota(jnp.int32, (ROW_CHUNK, tq), 0)

        def count(pred):
            acc = None
            for c in range(n_keys // ROW_CHUNK):
                hit = pred(key_ref[c * ROW_CHUNK:(c + 1) * ROW_CHUNK, :], chunk_pos + c * ROW_CHUNK)
                ones = jnp.where(hit, 1.0, 0.0)
                acc = ones if acc is None else acc + ones
            return jnp.sum(acc, axis=0, keepdims=True)

        def count_half(pred):
            acc = None
            for c in range(n_keys // (2 * ROW_CHUNK)):
                hit = pred(half_ref[c * 2 * ROW_CHUNK:(c + 1) * 2 * ROW_CHUNK, :])
                ones = jnp.where(hit, jnp.int16(1), jnp.int16(0))
                acc = ones if acc is None else acc + ones
            return jnp.sum(acc.astype(F32), axis=0, keepdims=True)

        def search_half(need, count_at_min):
            c0 = count_half(lambda h: h >= 0)
            t0 = jnp.where(c0 >= need, 0, HALF_MIN).astype(jnp.int32)
            n0 = jnp.where(c0 >= need, c0, count_at_min)

            def step(it, carry):
                t, n = carry
                cand = t | (jnp.int32(1) << (HALF_BITS - 2 - it))
                cand16 = cand.astype(jnp.int16)
                c = count_half(lambda h: h >= cand16)
                ok = c >= need
                return jnp.where(ok, cand, t), jnp.where(ok, c, n)

            return lax.fori_loop(0, HALF_BITS - 1, step, (t0, n0))

        k_f = float(topk)
        half_ref[...] = (key_ref[...] >> HALF_BITS).astype(jnp.int16)
        t_hi, cnt_ge_hi = search_half(k_f, float(n_keys))
        t_hi16 = t_hi.astype(jnp.int16)
        cnt_gt_hi = count_half(lambda h: h > t_hi16)
        key = key_ref[...]
        low = (key & (2 ** HALF_BITS - 1)) + HALF_MIN
        half_ref[...] = jnp.where((key >> HALF_BITS) == t_hi, low, HALF_MIN).astype(jnp.int16)
        t_lo, cnt_lo = search_half(k_f - cnt_gt_hi, cnt_ge_hi - cnt_gt_hi)
        thr = t_hi * 2 ** HALF_BITS + (t_lo - HALF_MIN)
        cnt_ge = cnt_gt_hi + cnt_lo

        surplus = cnt_ge - k_f
        max_surplus = jnp.max(surplus)

        def drop_highest_ties():
            half_ref[...] = jnp.where(key_ref[...] == thr, pos, -1).astype(jnp.int16)

            def drop_round(r, cutoff):
                cutoff16 = cutoff.astype(jnp.int16)
                top = None
                for c in range(n_keys // (2 * ROW_CHUNK)):
                    tie_pos = half_ref[c * 2 * ROW_CHUNK:(c + 1) * 2 * ROW_CHUNK, :]
                    below = jnp.where(tie_pos < cutoff16, tie_pos, jnp.int16(-1))
                    top = below if top is None else jnp.where(below > top, below, top)
                highest = jnp.max(top.astype(jnp.int32), axis=0, keepdims=True)
                return jnp.where(surplus > r.astype(F32), highest, cutoff)

            return lax.fori_loop(0, max_surplus.astype(jnp.int32), drop_round, jnp.full((1, tq), n_keys, jnp.int32))

        def search_cutoff():
            remaining = k_f - count(lambda k, _: k > thr)
            n_bits = (n_keys - 1).bit_length()

            def tie_step(it, last):
                cand = last | (jnp.int32(1) << (n_bits - 1 - it))
                c = count(lambda k, kpos: (k == thr) & (kpos < cand))
                return jnp.where(c < remaining, cand, last)

            return lax.fori_loop(0, n_bits, tie_step, jnp.zeros((1, tq), jnp.int32)) + 1

        def write_mask(cutoff):
            key = key_ref[...]
            mask_ref[...] = jnp.where((key > thr) | ((key == thr) & (pos < cutoff)), 0.0, NEG)

        @pl.when(max_surplus == 0.0)
        def _():
            mask_ref[...] = jnp.where(key_ref[...] >= thr, 0.0, NEG)

        @pl.when((max_surplus > 0.0) & (max_surplus <= FEW_TIES))
        def _():
            write_mask(drop_highest_ties())

        @pl.when(max_surplus > FEW_TIES)
        def _():
            write_mask(search_cutoff())
    else:
        mask_ref[...] = jnp.zeros_like(mask_ref)

    q = q_ref[0]
    kk = kk_ref[0]
    v_t = _with_ones_rows(vt_ref[0])
    outs = []
    for p in range(A_HEADS // 2):
        logits = _dot_nt(kk, _head_pair_rhs(q[:, p * LANES:(p + 1) * LANES]))
        for half in range(2):
            bias = _causal_bias(bias_ref, 2 * p + half, i * tq, 0, n_keys, tq, s_len)
            lg = logits[:, half * tq:(half + 1) * tq] + bias + mask_ref[...]
            outs.append(_softmax_pv(lg, v_t))
    o_ref[0] = jnp.concatenate(outs, axis=0).T.astype(o_ref.dtype)


def _dsa(aq, iq, iw_t, akk, aii, av_t, bias_master):
    bsz, s_len, _ = aq.shape
    topk = min(A_TOPK_MAX, s_len // 4)
    out_shape = jax.ShapeDtypeStruct(aq.shape, BF)

    bps = _batch_per_step(bsz)

    def call_group(g, n_keys, out):
        tile0 = g * TILES_PER_GROUP
        q_spec = lambda w: pl.BlockSpec((bps, TQ, w), lambda i, b: (b, tile0 + i, 0))
        keys = lambda w: pl.BlockSpec((bps, n_keys, w), lambda i, b: (b, 0, 0))
        return pl.pallas_call(
            functools.partial(_dsa_kernel, topk=topk, tile0=tile0, s_len=s_len),
            grid=(TILES_PER_GROUP, bsz // bps),
            in_specs=[q_spec(A_HEADS * HEAD_DIM), q_spec(IDX_HEADS * HEAD_DIM),
                      pl.BlockSpec((bps, BF16_ROWS, TQ), lambda i, b: (b, 0, tile0 + i)),
                      keys(2 * HEAD_DIM), keys(2 * HEAD_DIM),
                      pl.BlockSpec((bps, HEAD_DIM, n_keys), lambda i, b: (b, 0, 0)),
                      _const_spec2(bias_master.shape), pl.BlockSpec(memory_space=pl.ANY)],
            out_specs=q_spec(A_HEADS * HEAD_DIM),
            out_shape=out_shape,
            input_output_aliases={7: 0},
            scratch_shapes=[pltpu.VMEM((n_keys, TQ), jnp.int32), pltpu.VMEM((n_keys, TQ), jnp.int16),
                            pltpu.VMEM((n_keys, TQ), F32)],
            compiler_params=_cparams("arbitrary", "arbitrary"),
            name=f"dsa_mixer_{g}",
        )(aq, iq, iw_t, akk, aii, av_t, bias_master, out)

    return _grouped_tiles(call_group, out_shape)


_PAIR_W = 2 * HEAD_DIM
_N_PAIRS = B_SLOTS // 2
_N_STATE = 2 * _N_PAIRS


def _class_rows(r, t, dil):
    start = r + t * BAND * dil
    return pl.ds(start, BAND, stride=dil) if dil > 1 else pl.ds(start, BAND)


def _pair_ones(n_keys):
    row = lax.broadcasted_iota(jnp.int32, (2 * n_keys, _PAIR_W), 0)
    lane = lax.broadcasted_iota(jnp.int32, (2 * n_keys, _PAIR_W), 1)
    return jnp.where((row < n_keys) == (lane < HEAD_DIM), 1.0, 0.0).astype(BF)


def _band_pairs(units, ones_bd):
    n_keys = units[0][1].shape[0]
    lane = lax.broadcasted_iota(jnp.int32, (BAND, _PAIR_W), 1)
    logits = [_dot_nt(q, _head_pair_rhs(k)) + bias for q, k, _, bias in units]
    maxes = [(jnp.max(lg[:, :n_keys], axis=1, keepdims=True), jnp.max(lg[:, n_keys:], axis=1, keepdims=True))
             for lg in logits]
    probs = [jnp.concatenate([jnp.exp(lg[:, :n_keys] - ma), jnp.exp(lg[:, n_keys:] - mb)], axis=1).astype(BF)
             for lg, (ma, mb) in zip(logits, maxes)]
    results = []
    for (_, _, v, _), e, (ma, mb) in zip(units, probs, maxes):
        acc = _dot(e, _head_pair_rhs(v))
        den = _dot(e, ones_bd)
        results.append((acc / den, jnp.where(lane < HEAD_DIM, ma, mb) + jnp.log(den)))
    return results


def _band_state_kernel(q_ref, k_ref, v_ref, kp_ref, vp_ref, bias_ref, st_ref, *, dil):
    c = pl.program_id(1)
    n_tiles = q_ref.shape[2] // (BAND * dil)
    first_variant = jnp.where(c == 0, 1, 0)
    load = lambda ref, p, rows: ref[0, p, rows, :].astype(BF)
    units, where = [], []
    for r in range(dil):
        for t in range(n_tiles):
            rows = _class_rows(r, t, dil)
            for p in range(_N_PAIRS):
                if t == 0:
                    prev = _class_rows(r, 0, dil)
                    k = jnp.concatenate([load(kp_ref, p, prev), load(k_ref, p, rows)], axis=0)
                    v = jnp.concatenate([load(vp_ref, p, prev), load(v_ref, p, rows)], axis=0)
                    bias = bias_ref[0, first_variant, p]
                else:
                    prev = _class_rows(r, t - 1, dil)
                    k = jnp.concatenate([load(k_ref, p, prev), load(k_ref, p, rows)], axis=0)
                    v = jnp.concatenate([load(v_ref, p, prev), load(v_ref, p, rows)], axis=0)
                    bias = bias_ref[0, 0, p]
                units.append((load(q_ref, p, rows), k, v, bias))
                where.append((rows, p))
    for (rows, p), (out, lse) in zip(where, _band_pairs(units, _pair_ones(2 * BAND))):
        st_ref[0, p, rows, :] = out
        st_ref[0, _N_PAIRS + p, rows, :] = lse


def _band_state(q, k, v, bias_tiles, g, dil):
    bsz, n_lt, s_len, _ = q.shape
    rows = 8 * BAND
    prev_rows = BAND * dil
    per = rows // prev_rows
    main = pl.BlockSpec((1, n_lt, rows, LANES), lambda b, c: (b, 0, c, 0))
    prev = pl.BlockSpec((1, n_lt, prev_rows, LANES), lambda b, c: (b, 0, jnp.maximum(c * per - 1, 0), 0))
    return pl.pallas_call(
        functools.partial(_band_state_kernel, dil=dil),
        grid=(bsz, s_len // rows),
        in_specs=[main, main, main, prev, prev,
                  pl.BlockSpec((1, 2, _N_PAIRS, BAND, 4 * BAND), lambda b, c: (g, 0, 0, 0, 0))],
        out_specs=pl.BlockSpec((1, _N_STATE, rows, LANES), lambda b, c: (b, 0, c, 0)),
        out_shape=jax.ShapeDtypeStruct((bsz, _N_STATE, s_len, LANES), F32),
        compiler_params=_cparams("parallel", "arbitrary"),
        name=f"band_state_{g}",
    )(q, k, v, k, v, bias_tiles)


_DEINTERLEAVE = 4


def _band_merge_kernel(q_ref, k_ref, v_ref, st1_ref, st2_ref, bias_ref, o_ref, in_scr, out_scr, *, dil):
    inner = dil // _DEINTERLEAVE
    n_stage = q_ref.shape[2] // _DEINTERLEAVE
    sources = [(q_ref, p) for p in range(_N_PAIRS)] + [(k_ref, p) for p in range(_N_PAIRS)] \
        + [(v_ref, p) for p in range(_N_PAIRS)] + [(st1_ref, j) for j in range(_N_STATE)] \
        + [(st2_ref, j) for j in range(_N_STATE)]
    q0, k0, v0, s1, s2 = 0, _N_PAIRS, 2 * _N_PAIRS, 3 * _N_PAIRS, 3 * _N_PAIRS + _N_STATE
    own_bias = [jnp.concatenate([bias_ref[0, 0, p, :, BAND:2 * BAND], bias_ref[0, 0, p, :, 3 * BAND:]], axis=1)
                for p in range(_N_PAIRS)]
    ones_bd = _pair_ones(BAND)
    for r_outer in range(_DEINTERLEAVE):
        for n, (ref, idx) in enumerate(sources):
            in_scr[n] = ref[0, idx, pl.ds(r_outer, n_stage, stride=_DEINTERLEAVE), :]
        rows_of = [pl.ds(r_inner, BAND, stride=inner) for r_inner in range(inner)]
        load = lambda n, rows: in_scr[n, rows, :].astype(BF)
        units, where = [], []
        for rows in rows_of:
            for p in range(_N_PAIRS):
                units.append((load(q0 + p, rows), load(k0 + p, rows), load(v0 + p, rows), own_bias[p]))
                where.append((rows, p))
        for (rows, p), (out3, lse3) in zip(where, _band_pairs(units, ones_bd)):
            lse1, lse2 = in_scr[s1 + _N_PAIRS + p, rows, :], in_scr[s2 + _N_PAIRS + p, rows, :]
            top = jnp.maximum(jnp.maximum(lse1, lse2), lse3)
            w1, w2, w3 = jnp.exp(lse1 - top), jnp.exp(lse2 - top), jnp.exp(lse3 - top)
            num = w1 * in_scr[s1 + p, rows, :] + w2 * in_scr[s2 + p, rows, :] + w3 * out3
            out_scr[p, rows, :] = num / (w1 + w2 + w3)
        for p in range(_N_PAIRS):
            o_ref[0, p, pl.ds(r_outer, n_stage, stride=_DEINTERLEAVE), :] = out_scr[p]


def _band_merge(q, k, v, st1, st2, bias_tiles, g, dil):
    bsz, n_lt, s_len, _ = q.shape
    assert s_len == BAND * dil and dil % _DEINTERLEAVE == 0
    full = lambda n: pl.BlockSpec((1, n, s_len, LANES), lambda b: (b, 0, 0, 0))
    n_stage = s_len // _DEINTERLEAVE
    return pl.pallas_call(
        functools.partial(_band_merge_kernel, dil=dil),
        grid=(bsz,),
        in_specs=[full(n_lt), full(n_lt), full(n_lt), full(_N_STATE), full(_N_STATE),
                  pl.BlockSpec((1, 2, _N_PAIRS, BAND, 4 * BAND), lambda b: (g, 0, 0, 0, 0))],
        out_specs=full(n_lt),
        out_shape=jax.ShapeDtypeStruct((bsz, n_lt, s_len, LANES), F32),
        scratch_shapes=[pltpu.VMEM((3 * _N_PAIRS + 2 * _N_STATE, n_stage, LANES), F32),
                        pltpu.VMEM((_N_PAIRS, n_stage, LANES), F32)],
        compiler_params=_cparams("parallel"),
        name="band_merge",
    )(q, k, v, st1, st2, bias_tiles)


def _dilated(bq, bk, bv, bias_tiles):
    (_, d0), (_, d1), (_, d2) = B_PATTERNS
    st0 = _band_state(bq[0], bk, bv, bias_tiles, 0, d0)
    st1 = _band_state(bq[1], bk, bv, bias_tiles, 1, d1)
    return _band_merge(bq[2], bk, bv, st0, st1, bias_tiles, 2, d2)


def _block_mean_kernel(k_ref, o_ref):
    n_blk = k_ref.shape[1] // C_BLOCK
    o_ref[...] = jnp.zeros_like(o_ref)
    for j in range(n_blk):
        blk = k_ref[0, j * C_BLOCK:(j + 1) * C_BLOCK, :].astype(F32)
        o_ref[0, j:j + 1, :] = (jnp.sum(blk, axis=0, keepdims=True) * (1.0 / C_BLOCK)).astype(o_ref.dtype)


def _block_means(ck):
    bsz, s_len, w = ck.shape
    assert s_len // C_BLOCK <= BF16_ROWS
    return pl.pallas_call(
        _block_mean_kernel,
        grid=(bsz,),
        in_specs=[pl.BlockSpec((1, s_len, w), lambda b: (b, 0, 0))],
        out_specs=pl.BlockSpec((1, BF16_ROWS, w), lambda b: (b, 0, 0)),
        out_shape=jax.ShapeDtypeStruct((bsz, BF16_ROWS, w), BF),
        compiler_params=_cparams("parallel"),
        name="moba_block_means",
    )(ck)


def _moba_kernel(q_ref, k_ref, vt_ref, kmean_ref, bias_ref, _, o_ref, **static):
    for bb in range(q_ref.shape[0]):
        _moba_tile(*(_one_batch(r, bb) for r in (q_ref, k_ref, vt_ref, kmean_ref)), bias_ref,
                   _one_batch(o_ref, bb), **static)


def _moba_tile(q_ref, k_ref, vt_ref, kmean_ref, bias_ref, o_ref, *, cur, s_len):
    tq = q_ref.shape[1]
    t0 = (cur * TILES_PER_GROUP + pl.program_id(0)) * tq
    q = q_ref[0]
    kmean = kmean_ref[0]
    blk = lax.broadcasted_iota(jnp.int32, (BF16_ROWS, tq), 0)
    outs = []
    for p in range(C_HEADS // 2):
        ps = slice(p * LANES, (p + 1) * LANES)
        rhs = _head_pair_rhs(q[:, ps])
        logits = _dot_nt(k_ref[0, :, ps], rhs).astype(BF)
        gates = _dot_nt(kmean[:, ps], rhs)
        for half in range(2):
            h = 2 * p + half
            hq = slice(half * tq, (half + 1) * tq)
            if cur > C_TOPK:
                gate = jnp.where(blk < cur, gates[:, hq], -jnp.inf)
                off = jnp.where(blk == cur, 0.0, NEG)
                for j in range(cur):
                    gj = gate[j:j + 1, :]
                    beats = (gate > gj) | ((gate == gj) & (blk < j))
                    rank = jnp.sum(jnp.where(beats, 1.0, 0.0), axis=0, keepdims=True)
                    off = jnp.where((blk == j) & (rank < float(C_TOPK)), 0.0, off)
            pieces = []
            for j in range(cur + 1):
                ks = slice(j * C_BLOCK, (j + 1) * C_BLOCK)
                piece = logits[ks, hq] + _causal_bias(bias_ref, h, t0, j * C_BLOCK, C_BLOCK, tq, s_len)
                if cur > C_TOPK:
                    piece = piece + off[j:j + 1, :].astype(BF)
                pieces.append(piece)
            lg = jnp.concatenate(pieces, axis=0)
            outs.append(_softmax_pv(lg, _with_ones_rows(vt_ref[0, h * HEAD_DIM:(h + 1) * HEAD_DIM, :])))
    o_ref[0] = jnp.concatenate(outs, axis=0).T.astype(o_ref.dtype)


def _moba(cq, ck, cv_t, bias_master):
    bsz, s_len, w = cq.shape
    assert TILES_PER_GROUP * TQ == C_BLOCK
    kmean = _block_means(ck)
    out_shape = jax.ShapeDtypeStruct(cq.shape, BF)

    bps = _batch_per_step(bsz)

    def call_group(g, n_keys, out):
        tile0 = g * TILES_PER_GROUP
        return pl.pallas_call(
            functools.partial(_moba_kernel, cur=g, s_len=s_len),
            grid=(TILES_PER_GROUP, bsz // bps),
            in_specs=[pl.BlockSpec((bps, TQ, w), lambda i, b: (b, tile0 + i, 0)),
                      pl.BlockSpec((bps, n_keys, w), lambda i, b: (b, 0, 0)),
                      pl.BlockSpec((bps, w, n_keys), lambda i, b: (b, 0, 0)),
                      pl.BlockSpec((bps, BF16_ROWS, w), lambda i, b: (b, 0, 0)),
                      _const_spec2(bias_master.shape), pl.BlockSpec(memory_space=pl.ANY)],
            out_specs=pl.BlockSpec((bps, TQ, w), lambda i, b: (b, tile0 + i, 0)),
            out_shape=out_shape,
            input_output_aliases={5: 0},
            compiler_params=_cparams("arbitrary", "arbitrary"),
            name=f"moba_mixer_{g}",
        )(cq, ck, cv_t, kmean, bias_master, out)

    return _grouped_tiles(call_group, out_shape)


def _layer_norm(y, g_ref, b_ref):
    mu = jnp.mean(y, axis=-1, keepdims=True)
    yc = y - mu
    var = jnp.mean(yc * yc, axis=-1, keepdims=True)
    return yc * lax.rsqrt(var + LN_EPS) * g_ref[...] + b_ref[...]


def _merge_kernel(x_ref, oa_ref, ob_ref, oc_ref, wg_ref, wa_ref, wb_ref, wc_ref, wo_ref, g_ref, b_ref, y_ref):
    x = x_ref[...]
    xb = x.astype(BF)
    merged = None
    o_b = jnp.concatenate([ob_ref[0, lt] for lt in range(ob_ref.shape[1])], axis=1).astype(BF)
    for n, (o, w_ref) in enumerate(((oa_ref[...], wa_ref), (o_b, wb_ref), (oc_ref[...], wc_ref))):
        gate = jax.nn.sigmoid(_dot(xb, wg_ref[:, n * D_MODEL:(n + 1) * D_MODEL]))
        term = gate * _dot(o, w_ref[...])
        merged = term if merged is None else merged + term
    y = ALPHA * x + _dot(merged.astype(BF), wo_ref[...])
    y_ref[...] = _layer_norm(y, g_ref, b_ref)


def _const_spec(shape):
    return pl.BlockSpec(shape, lambda i: (0,) * len(shape), pipeline_mode=pl.Buffered(1))


def _merge(x2d, oa, ob, oc, wg, wa, wb, wc, wo, ln_g, ln_b, tm):
    n = x2d.shape[0]
    tiles_per_seq = ob.shape[2] // tm
    rows = lambda w: pl.BlockSpec((tm, w), lambda i: (i, 0))
    ob_spec = pl.BlockSpec((1, ob.shape[1], tm, LANES), lambda i: (i // tiles_per_seq, 0, i % tiles_per_seq, 0))
    return pl.pallas_call(
        _merge_kernel,
        grid=(n // tm,),
        in_specs=[rows(D_MODEL), rows(oa.shape[1]), ob_spec, rows(oc.shape[1]),
                  _const_spec(wg.shape), _const_spec(wa.shape), _const_spec(wb.shape), _const_spec(wc.shape),
                  _const_spec(wo.shape), _const_spec(ln_g.shape), _const_spec(ln_b.shape)],
        out_specs=rows(D_MODEL),
        out_shape=jax.ShapeDtypeStruct((n, D_MODEL), F32),
        compiler_params=_cparams("parallel"),
        name="merge_out_ln",
    )(x2d, oa, ob, oc, wg, wa, wb, wc, wo, ln_g, ln_b)


_FF_CHUNK = 1024


def _ffn_kernel(x_ref, p_ref, wu_ref, wd_ref, wpg_ref, wp_ref, g_ref, b_ref, y_ref):
    x = x_ref[...]
    xb = x.astype(BF)
    y = ALPHA * x + jax.nn.sigmoid(_dot(xb, wpg_ref[...])) * _dot(p_ref[...].astype(BF), wp_ref[...])
    for c in range(D_FF // _FF_CHUNK):
        cs = slice(c * _FF_CHUNK, (c + 1) * _FF_CHUNK)
        u = jnp.maximum(_dot(xb, wu_ref[:, cs]), 0.0)
        y = y + _dot((u * u).astype(BF), wd_ref[cs, :])
    y_ref[...] = _layer_norm(y, g_ref, b_ref)


def _ffn(x2d, p2d, wu, wd, wpg, wp, ln_g, ln_b, tm):
    n = x2d.shape[0]
    rows = lambda w: pl.BlockSpec((tm, w), lambda i: (i, 0))
    return pl.pallas_call(
        _ffn_kernel,
        grid=(n // tm,),
        in_specs=[rows(D_MODEL), rows(PLE_DIM), _const_spec(wu.shape), _const_spec(wd.shape),
                  _const_spec(wpg.shape), _const_spec(wp.shape), _const_spec(ln_g.shape), _const_spec(ln_b.shape)],
        out_specs=rows(D_MODEL),
        out_shape=jax.ShapeDtypeStruct((n, D_MODEL), F32),
        compiler_params=_cparams("parallel"),
        name="ffn_ple_ln",
    )(x2d, p2d, wu, wd, wpg, wp, ln_g, ln_b)


def kernel(x, p, w_in, w_gate, w_br_a, w_br_b, w_br_c, w_out, ln1_g, ln1_b,
           w_up, w_down, w_ple_gate, w_ple, ln2_g, ln2_b, rel_bias):
    bsz, s_len, d_model = x.shape
    assert d_model == D_MODEL and s_len == MAX_DISTANCE, (x.shape,)
    n_tok = bsz * s_len
    tm = 1024

    b_head0 = A_HEADS
    c_head0 = A_HEADS + B_GROUPS * B_SLOTS
    bias_a = _causal_bias_master(rel_bias, 0, A_HEADS, s_len, F32)
    bias_b = _band_bias_tiles(rel_bias, b_head0)
    bias_c = _causal_bias_master(rel_bias, c_head0, C_HEADS, s_len, BF)

    x2d = x.reshape(n_tok, D_MODEL)
    for i in range(DEPTH):
        pr = _project(x2d, *_pack_w_in(w_in[i]), tm, s_len)
        seq = lambda name: pr[name].reshape(bsz, s_len, -1)
        o_a = _dsa(seq("aq"), seq("iq"), pr["iwT"], seq("akk"), seq("aii"), pr["avT"], bias_a)
        o_b = _dilated((pr["bq0"], pr["bq1"], pr["bq2"]), pr["bk"], pr["bv"], bias_b)
        o_c = _moba(seq("cq"), seq("ck"), pr["cvT"], bias_c)
        flat = lambda a: a.reshape(n_tok, -1)
        row = lambda a: a.reshape(1, D_MODEL)
        x2d = _merge(x2d, flat(o_a), o_b, flat(o_c), w_gate[i].astype(BF), w_br_a[i].astype(BF),
                     w_br_b[i].astype(BF), w_br_c[i].astype(BF), w_out[i].astype(BF),
                     row(ln1_g[i]), row(ln1_b[i]), tm)
        x2d = _ffn(x2d, p[i].reshape(n_tok, PLE_DIM), w_up[i].astype(BF), w_down[i].astype(BF),
                   w_ple_gate[i].astype(BF), w_ple[i].astype(BF), row(ln2_g[i]), row(ln2_b[i]), tm)
    return x2d.reshape(bsz, s_len, D_MODEL)
```

```python
import functools
import math

import numpy as np
import jax
import jax.numpy as jnp
from jax import lax
from jax.experimental import pallas as pl
from jax.experimental.pallas import tpu as pltpu

D_MODEL = 1024
HEAD_DIM = 64
A_HEADS = 6
IDX_HEADS = 8
A_TOPK_MAX = 256
B_SLOTS = 4
B_PATTERNS = ((128, 1), (512, 4), (2048, 16))
B_GROUPS = 3
C_HEADS = 6
C_BLOCK = 256
C_TOPK = 3
N_BUCKETS = 32
MAX_DISTANCE = 2048
D_FF = 4 * D_MODEL
PLE_DIM = 256
DEPTH = 2
ALPHA = (2 * DEPTH) ** 0.25
LN_EPS = 1e-5
NEG = -1e30
QK_SCALE = HEAD_DIM ** -0.5

LANES = 128
SUBLANES = 8
BF16_ROWS = 16
VMEM_LIMIT = 56 * 1024 * 1024

TQ = 256
BIAS_TQ = LANES
TILES_PER_GROUP = C_BLOCK // TQ
BAND = 128
BF = jnp.bfloat16
F32 = jnp.float32
HALF_BITS = 16
HALF_MIN = -2 ** (HALF_BITS - 1)
FEW_TIES = 8.0

_NT = (((1,), (1,)), ((), ()))


def _dot(a, b):
    return jnp.dot(a, b, preferred_element_type=F32)


def _dot_nt(a, b):
    return lax.dot_general(a, b, _NT, preferred_element_type=F32)


def _cparams(*sem):
    return pltpu.CompilerParams(dimension_semantics=sem, vmem_limit_bytes=VMEM_LIMIT)


def _bucket_starts():
    d = np.arange(0, MAX_DISTANCE + 1)
    max_exact = N_BUCKETS // 2
    nf = np.maximum(d, 1).astype(np.float32)
    large = max_exact + (np.log(nf / np.float32(max_exact)) / np.float32(math.log(MAX_DISTANCE / max_exact))
                         * np.float32(N_BUCKETS - max_exact)).astype(np.int32)
    bucket = np.where(d < max_exact, d, np.minimum(large, N_BUCKETS - 1))
    return [int(np.argmax(bucket >= b)) if np.any(bucket >= b) else None for b in range(N_BUCKETS)]


_BUCKET_START = _bucket_starts()


def _bias_from_distance(dist, tab_ref, col):
    val = jnp.full(dist.shape, tab_ref[0, col], F32)
    for b in range(1, N_BUCKETS):
        if _BUCKET_START[b] is not None:
            val = jnp.where(dist >= _BUCKET_START[b], tab_ref[b, col], val)
    return jnp.where(dist < 0, NEG, val)


def _causal_bias_kernel(tab_ref, o_ref, *, head0):
    h = pl.program_id(0)
    n_rows, tq = o_ref.shape[1], o_ref.shape[2]
    u = lax.broadcasted_iota(jnp.int32, (n_rows, tq), 0)
    t = lax.broadcasted_iota(jnp.int32, (n_rows, tq), 1)
    o_ref[0] = _bias_from_distance(t + (n_rows - tq) // 2 - u, tab_ref, head0 + h).astype(o_ref.dtype)


def _causal_bias_master(rel_bias, head0, n_heads, s_len, dtype):
    n_rows = 2 * s_len - BIAS_TQ
    return pl.pallas_call(
        functools.partial(_causal_bias_kernel, head0=head0),
        grid=(n_heads,),
        in_specs=[pl.BlockSpec(memory_space=pltpu.SMEM)],
        out_specs=pl.BlockSpec((1, n_rows, BIAS_TQ), lambda h: (h, 0, 0)),
        out_shape=jax.ShapeDtypeStruct((n_heads, n_rows, BIAS_TQ), dtype),
        compiler_params=_cparams("arbitrary"),
        name="causal_bias_master",
    )(rel_bias)


def _bias_row0(t0, s_len):
    return pl.multiple_of(s_len - BIAS_TQ - t0, BIAS_TQ)


def _causal_bias(bias_ref, h, t0, first_key, n_keys, tq, s_len):
    parts = [bias_ref[h, pl.ds(_bias_row0(t0 + c * BIAS_TQ, s_len) + first_key, n_keys), :]
             for c in range(tq // BIAS_TQ)]
    return parts[0] if len(parts) == 1 else jnp.concatenate(parts, axis=1)


def _band_bias_kernel(tab_ref, o_ref, *, head0):
    g = pl.program_id(0)
    variant = pl.program_id(1)
    pair = pl.program_id(2)
    row = lax.broadcasted_iota(jnp.int32, (BAND, 2 * BAND), 0)
    col = lax.broadcasted_iota(jnp.int32, (BAND, 2 * BAND), 1)
    j = row + BAND - col
    no_prev = (variant == 1) & (col < BAND)
    for gi, (_, dil) in enumerate(B_PATTERNS):
        @pl.when(g == gi)
        def _(dil=dil):
            for half in range(2):
                bias = _bias_from_distance(j * dil, tab_ref, head0 + g * B_SLOTS + 2 * pair + half)
                bias = jnp.where((j > BAND) | no_prev, NEG, bias)
                o_ref[0, 0, 0, :, half * 2 * BAND:(half + 1) * 2 * BAND] = bias


def _band_bias_tiles(rel_bias, head0):
    return pl.pallas_call(
        functools.partial(_band_bias_kernel, head0=head0),
        grid=(B_GROUPS, 2, B_SLOTS // 2),
        in_specs=[pl.BlockSpec(memory_space=pltpu.SMEM)],
        out_specs=pl.BlockSpec((1, 1, 1, BAND, 4 * BAND), lambda g, v, p: (g, v, p, 0, 0)),
        out_shape=jax.ShapeDtypeStruct((B_GROUPS, 2, B_SLOTS // 2, BAND, 4 * BAND), F32),
        compiler_params=_cparams("arbitrary", "arbitrary", "arbitrary"),
        name="band_bias_tiles",
    )(rel_bias)


_PROJ_OUTS = (
    ("aq", A_HEADS * HEAD_DIM, QK_SCALE),
    ("akk", 2 * HEAD_DIM, 1.0),
    ("aii", 2 * HEAD_DIM, 1.0),
    ("iq", IDX_HEADS * HEAD_DIM, QK_SCALE),
    ("bq0", B_SLOTS * HEAD_DIM, QK_SCALE),
    ("bq1", B_SLOTS * HEAD_DIM, QK_SCALE),
    ("bq2", B_SLOTS * HEAD_DIM, QK_SCALE),
    ("bk", B_SLOTS * HEAD_DIM, 1.0),
    ("bv", B_SLOTS * HEAD_DIM, 1.0),
    ("cq", C_HEADS * HEAD_DIM, QK_SCALE),
    ("ck", C_HEADS * HEAD_DIM, 1.0),
)
_PROJ_WIDTH = sum(w for _, w, _ in _PROJ_OUTS)
_PROJ_LANE_TILED = ("bq0", "bq1", "bq2", "bk", "bv")
_PROJ_OUTS_T = (
    ("avT", HEAD_DIM, BF),
    ("cvT", C_HEADS * HEAD_DIM, BF),
    ("iwT", BF16_ROWS, F32),
)
_PROJ_ROWS_T = sum(r for _, r, _ in _PROJ_OUTS_T)


def _pack_w_in(w):
    widths = (384, 64, 64, 512, 64, 8, 768, 256, 256, 384, 384, 384)
    offs = np.concatenate([[0], np.cumsum(widths)])
    aq, ak, av, iq, ik, iw, bq, bk, bv, cq, ck, cv = (w[:, offs[n]:offs[n + 1]] for n in range(12))
    cols = jnp.concatenate([aq, ak, ak, ik, ik, iq, bq, bk, bv, cq, ck], axis=1)
    iw_pad = jnp.concatenate([iw, jnp.zeros((w.shape[0], BF16_ROWS - IDX_HEADS), w.dtype)], axis=1)
    rows = jnp.concatenate([av, cv, iw_pad], axis=1).T
    return cols.astype(BF), rows.astype(BF)


def _proj_kernel(x_ref, w_ref, wt_ref, *o_refs):
    xb = x_ref[...].astype(BF)
    off = 0
    for o_ref, (_, width, scale) in zip(o_refs, _PROJ_OUTS):
        res = _dot(xb, w_ref[:, off:off + width])
        if scale != 1.0:
            res = res * scale
        if len(o_ref.shape) == 4:
            for lt in range(width // LANES):
                o_ref[0, lt] = res[:, lt * LANES:(lt + 1) * LANES]
        else:
            o_ref[...] = res.astype(o_ref.dtype)
        off += width
    off = 0
    for o_ref, (_, rows, _) in zip(o_refs[len(_PROJ_OUTS):], _PROJ_OUTS_T):
        o_ref[0] = _dot_nt(wt_ref[off:off + rows, :], xb).astype(o_ref.dtype)
        off += rows


def _project(x2d, w_cols, w_rows, tm, s_len):
    n = x2d.shape[0]
    tiles_per_seq = s_len // tm
    out_shape, out_specs = [], []
    for name, w, _ in _PROJ_OUTS:
        if name in _PROJ_LANE_TILED:
            out_shape.append(jax.ShapeDtypeStruct((n // s_len, w // LANES, s_len, LANES), F32))
            out_specs.append(pl.BlockSpec((1, w // LANES, tm, LANES),
                                          lambda i: (i // tiles_per_seq, 0, i % tiles_per_seq, 0)))
        else:
            out_shape.append(jax.ShapeDtypeStruct((n, w), BF))
            out_specs.append(pl.BlockSpec((tm, w), lambda i: (i, 0)))
    out_shape += [jax.ShapeDtypeStruct((n // s_len, r, s_len), dt) for _, r, dt in _PROJ_OUTS_T]
    out_specs += [pl.BlockSpec((1, r, tm), lambda i: (i // tiles_per_seq, 0, i % tiles_per_seq))
                  for _, r, _ in _PROJ_OUTS_T]
    outs = pl.pallas_call(
        _proj_kernel,
        grid=(n // tm,),
        in_specs=[pl.BlockSpec((tm, D_MODEL), lambda i: (i, 0)),
                  pl.BlockSpec((D_MODEL, _PROJ_WIDTH), lambda i: (0, 0)),
                  pl.BlockSpec((_PROJ_ROWS_T, D_MODEL), lambda i: (0, 0))],
        out_specs=out_specs,
        out_shape=out_shape,
        compiler_params=_cparams("parallel"),
        name="in_proj",
    )(x2d, w_cols, w_rows)
    names = [name for name, _, _ in _PROJ_OUTS] + [name for name, _, _ in _PROJ_OUTS_T]
    return dict(zip(names, outs))


def _head_pair_rhs(pair):
    lane = lax.broadcasted_iota(jnp.int32, pair.shape, 1)
    zero = jnp.zeros_like(pair)
    return jnp.concatenate([jnp.where(lane < HEAD_DIM, pair, zero), jnp.where(lane < HEAD_DIM, zero, pair)], axis=0)


ROW_CHUNK = 8 * SUBLANES


def _max_rows(x):
    chunk = ROW_CHUNK * (4 // x.dtype.itemsize)
    acc = x[:chunk]
    for c in range(1, x.shape[0] // chunk):
        acc = jnp.maximum(acc, x[c * chunk:(c + 1) * chunk])
    return jnp.max(acc.astype(F32), axis=0, keepdims=True)


def _softmax_pv(logits, v_t_ones):
    m = _max_rows(logits)
    e = jnp.exp(logits - m.astype(logits.dtype)).astype(BF)
    o = _dot(v_t_ones, e)
    return o[:HEAD_DIM] / o[HEAD_DIM:HEAD_DIM + 1]


def _with_ones_rows(v_t):
    return jnp.concatenate([v_t, jnp.ones((BF16_ROWS, v_t.shape[1]), v_t.dtype)], axis=0)


def _const_spec2(shape):
    return pl.BlockSpec(shape, lambda i, b: (0,) * len(shape), pipeline_mode=pl.Buffered(1))


BATCH_PER_STEP = 4


def _batch_per_step(bsz):
    return BATCH_PER_STEP if bsz % BATCH_PER_STEP == 0 else 1


def _one_batch(ref, bb):
    return ref.at[pl.ds(bb, 1)]


def _grouped_tiles(call_group, out_shape):
    s_len = out_shape.shape[1]
    out = jnp.zeros(out_shape.shape, out_shape.dtype)
    for g in range(s_len // (TILES_PER_GROUP * TQ)):
        out = call_group(g, (g + 1) * TILES_PER_GROUP * TQ, out)
    return out


def _dsa_kernel(q_ref, iq_ref, iw_ref, kk_ref, ii_ref, vt_ref, bias_ref, _, o_ref, key_ref, half_ref, mask_ref, **static):
    for bb in range(q_ref.shape[0]):
        _dsa_tile(*(_one_batch(r, bb) for r in (q_ref, iq_ref, iw_ref, kk_ref, ii_ref, vt_ref)), bias_ref,
                  _one_batch(o_ref, bb), key_ref, half_ref, mask_ref, **static)


def _dsa_tile(q_ref, iq_ref, iw_ref, kk_ref, ii_ref, vt_ref, bias_ref, o_ref, key_ref, half_ref, mask_ref, *,
              topk, tile0, s_len):
    i = tile0 + pl.program_id(0)
    n_keys, tq = key_ref.shape
    pos = lax.broadcasted_iota(jnp.int32, (n_keys, tq), 0)
    qry = i * tq + lax.broadcasted_iota(jnp.int32, (n_keys, tq), 1)

    if n_keys > topk:
        iq = iq_ref[0]
        iw = iw_ref[0]
        ii = ii_ref[0]
        index = jnp.zeros((n_keys, tq), F32)
        for p in range(IDX_HEADS // 2):
            sc = _dot_nt(ii, _head_pair_rhs(iq[:, p * LANES:(p + 1) * LANES]))
            index = index + iw[2 * p:2 * p + 1, :] * jnp.maximum(sc[:, :tq], 0.0)
            index = index + iw[2 * p + 1:2 * p + 2, :] * jnp.maximum(sc[:, tq:], 0.0)
        index = jnp.where(pos <= qry, index + 0.0, -jnp.inf)
        bits = pltpu.bitcast(index, jnp.int32)
        key_ref[...] = jnp.where(bits < 0, bits ^ jnp.int32(0x7FFFFFFF), bits)

        chunk_pos = lax.broadcasted_iota(jnp.int32, (ROW_CHUNK, tq), 0)

        def count(pred):
            acc = None
            for c in range(n_keys // ROW_CHUNK):
                hit = pred(key_ref[c * ROW_CHUNK:(c + 1) * ROW_CHUNK, :], chunk_pos + c * ROW_CHUNK)
                ones = jnp.where(hit, 1.0, 0.0)
                acc = ones if acc is None else acc + ones
            return jnp.sum(acc, axis=0, keepdims=True)

        def count_half(pred):
            acc = None
            for c in range(n_keys // (2 * ROW_CHUNK)):
                hit = pred(half_ref[c * 2 * ROW_CHUNK:(c + 1) * 2 * ROW_CHUNK, :])
                ones = jnp.where(hit, jnp.int16(1), jnp.int16(0))
                acc = ones if acc is None else acc + ones
            return jnp.sum(acc.astype(F32), axis=0, keepdims=True)

        def search_half(need, count_at_min):
            c0 = count_half(lambda h: h >= 0)
            t0 = jnp.where(c0 >= need, 0, HALF_MIN).astype(jnp.int32)
            n0 = jnp.where(c0 >= need, c0, count_at_min)

            def step(it, carry):
                t, n = carry
                cand = t | (jnp.int32(1) << (HALF_BITS - 2 - it))
                cand16 = cand.astype(jnp.int16)
                c = count_half(lambda h: h >= cand16)
                ok = c >= need
                return jnp.where(ok, cand, t), jnp.where(ok, c, n)

            return lax.fori_loop(0, HALF_BITS - 1, step, (t0, n0))

        k_f = float(topk)
        half_ref[...] = (key_ref[...] >> HALF_BITS).astype(jnp.int16)
        t_hi, cnt_ge_hi = search_half(k_f, float(n_keys))
        t_hi16 = t_hi.astype(jnp.int16)
        cnt_gt_hi = count_half(lambda h: h > t_hi16)
        key = key_ref[...]
        low = (key & (2 ** HALF_BITS - 1)) + HALF_MIN
        half_ref[...] = jnp.where((key >> HALF_BITS) == t_hi, low, HALF_MIN).astype(jnp.int16)
        t_lo, cnt_lo = search_half(k_f - cnt_gt_hi, cnt_ge_hi - cnt_gt_hi)
        thr = t_hi * 2 ** HALF_BITS + (t_lo - HALF_MIN)
        cnt_ge = cnt_gt_hi + cnt_lo

        surplus = cnt_ge - k_f
        max_surplus = jnp.max(surplus)

        def drop_highest_ties():
            half_ref[...] = jnp.where(key_ref[...] == thr, pos, -1).astype(jnp.int16)

            def drop_round(r, cutoff):
                cutoff16 = cutoff.astype(jnp.int16)
                top = None
                for c in range(n_keys // (2 * ROW_CHUNK)):
                    tie_pos = half_ref[c * 2 * ROW_CHUNK:(c + 1) * 2 * ROW_CHUNK, :]
                    below = jnp.where(tie_pos < cutoff16, tie_pos, jnp.int16(-1))
                    top = below if top is None else jnp.where(below > top, below, top)
                highest = jnp.max(top.astype(jnp.int32), axis=0, keepdims=True)
                return jnp.where(surplus > r.astype(F32), highest, cutoff)

            return lax.fori_loop(0, max_surplus.astype(jnp.int32), drop_round, jnp.full((1, tq), n_keys, jnp.int32))

        def search_cutoff():
            remaining = k_f - count(lambda k, _: k > thr)
            n_bits = (n_keys - 1).bit_length()

            def tie_step(it, last):
                cand = last | (jnp.int32(1) << (n_bits - 1 - it))
                c = count(lambda k, kpos: (k == thr) & (kpos < cand))
                return jnp.where(c < remaining, cand, last)

            return lax.fori_loop(0, n_bits, tie_step, jnp.zeros((1, tq), jnp.int32)) + 1

        def write_mask(cutoff):
            key = key_ref[...]
            mask_ref[...] = jnp.where((key > thr) | ((key == thr) & (pos < cutoff)), 0.0, NEG)

        @pl.when(max_surplus == 0.0)
        def _():
            mask_ref[...] = jnp.where(key_ref[...] >= thr, 0.0, NEG)

        @pl.when((max_surplus > 0.0) & (max_surplus <= FEW_TIES))
        def _():
            write_mask(drop_highest_ties())

        @pl.when(max_surplus > FEW_TIES)
        def _():
            write_mask(search_cutoff())
    else:
        mask_ref[...] = jnp.zeros_like(mask_ref)

    q = q_ref[0]
    kk = kk_ref[0]
    v_t = _with_ones_rows(vt_ref[0])
    outs = []
    for p in range(A_HEADS // 2):
        logits = _dot_nt(kk, _head_pair_rhs(q[:, p * LANES:(p + 1) * LANES]))
        for half in range(2):
            bias = _causal_bias(bias_ref, 2 * p + half, i * tq, 0, n_keys, tq, s_len)
            lg = logits[:, half * tq:(half + 1) * tq] + bias + mask_ref[...]
            outs.append(_softmax_pv(lg, v_t))
    o_ref[0] = jnp.concatenate(outs, axis=0).T.astype(o_ref.dtype)


def _dsa(aq, iq, iw_t, akk, aii, av_t, bias_master):
    bsz, s_len, _ = aq.shape
    topk = min(A_TOPK_MAX, s_len // 4)
    out_shape = jax.ShapeDtypeStruct(aq.shape, BF)

    bps = _batch_per_step(bsz)

    def call_group(g, n_keys, out):
        tile0 = g * TILES_PER_GROUP
        q_spec = lambda w: pl.BlockSpec((bps, TQ, w), lambda i, b: (b, tile0 + i, 0))
        keys = lambda w: pl.BlockSpec((bps, n_keys, w), lambda i, b: (b, 0, 0))
        return pl.pallas_call(
            functools.partial(_dsa_kernel, topk=topk, tile0=tile0, s_len=s_len),
            grid=(TILES_PER_GROUP, bsz // bps),
            in_specs=[q_spec(A_HEADS * HEAD_DIM), q_spec(IDX_HEADS * HEAD_DIM),
                      pl.BlockSpec((bps, BF16_ROWS, TQ), lambda i, b: (b, 0, tile0 + i)),
                      keys(2 * HEAD_DIM), keys(2 * HEAD_DIM),
                      pl.BlockSpec((bps, HEAD_DIM, n_keys), lambda i, b: (b, 0, 0)),
                      _const_spec2(bias_master.shape), pl.BlockSpec(memory_space=pl.ANY)],
            out_specs=q_spec(A_HEADS * HEAD_DIM),
            out_shape=out_shape,
            input_output_aliases={7: 0},
            scratch_shapes=[pltpu.VMEM((n_keys, TQ), jnp.int32), pltpu.VMEM((n_keys, TQ), jnp.int16),
                            pltpu.VMEM((n_keys, TQ), F32)],
            compiler_params=_cparams("arbitrary", "arbitrary"),
            name=f"dsa_mixer_{g}",
        )(aq, iq, iw_t, akk, aii, av_t, bias_master, out)

    return _grouped_tiles(call_group, out_shape)


_PAIR_W = 2 * HEAD_DIM
_N_PAIRS = B_SLOTS // 2
_N_STATE = 2 * _N_PAIRS


def _class_rows(r, t, dil):
    start = r + t * BAND * dil
    return pl.ds(start, BAND, stride=dil) if dil > 1 else pl.ds(start, BAND)


def _pair_ones(n_keys):
    row = lax.broadcasted_iota(jnp.int32, (2 * n_keys, _PAIR_W), 0)
    lane = lax.broadcasted_iota(jnp.int32, (2 * n_keys, _PAIR_W), 1)
    return jnp.where((row < n_keys) == (lane < HEAD_DIM), 1.0, 0.0).astype(BF)


def _band_pairs(units, ones_bd):
    n_keys = units[0][1].shape[0]
    lane = lax.broadcasted_iota(jnp.int32, (BAND, _PAIR_W), 1)
    logits = [_dot_nt(q, _head_pair_rhs(k)) + bias for q, k, _, bias in units]
    maxes = [(jnp.max(lg[:, :n_keys], axis=1, keepdims=True), jnp.max(lg[:, n_keys:], axis=1, keepdims=True))
             for lg in logits]
    probs = [jnp.concatenate([jnp.exp(lg[:, :n_keys] - ma), jnp.exp(lg[:, n_keys:] - mb)], axis=1).astype(BF)
             for lg, (ma, mb) in zip(logits, maxes)]
    results = []
    for (_, _, v, _), e, (ma, mb) in zip(units, probs, maxes):
        acc = _dot(e, _head_pair_rhs(v))
        den = _dot(e, ones_bd)
        results.append((acc / den, jnp.where(lane < HEAD_DIM, ma, mb) + jnp.log(den)))
    return results


def _band_state_kernel(q_ref, k_ref, v_ref, kp_ref, vp_ref, bias_ref, st_ref, *, dil):
    c = pl.program_id(1)
    n_tiles = q_ref.shape[2] // (BAND * dil)
    first_variant = jnp.where(c == 0, 1, 0)
    load = lambda ref, p, rows: ref[0, p, rows, :].astype(BF)
    units, where = [], []
    for r in range(dil):
        for t in range(n_tiles):
            rows = _class_rows(r, t, dil)
            for p in range(_N_PAIRS):
                if t == 0:
                    prev = _class_rows(r, 0, dil)
                    k = jnp.concatenate([load(kp_ref, p, prev), load(k_ref, p, rows)], axis=0)
                    v = jnp.concatenate([load(vp_ref, p, prev), load(v_ref, p, rows)], axis=0)
                    bias = bias_ref[0, first_variant, p]
                else:
                    prev = _class_rows(r, t - 1, dil)
                    k = jnp.concatenate([load(k_ref, p, prev), load(k_ref, p, rows)], axis=0)
                    v = jnp.concatenate([load(v_ref, p, prev), load(v_ref, p, rows)], axis=0)
                    bias = bias_ref[0, 0, p]
                units.append((load(q_ref, p, rows), k, v, bias))
                where.append((rows, p))
    for (rows, p), (out, lse) in zip(where, _band_pairs(units, _pair_ones(2 * BAND))):
        st_ref[0, p, rows, :] = out
        st_ref[0, _N_PAIRS + p, rows, :] = lse


def _band_state(q, k, v, bias_tiles, g, dil):
    bsz, n_lt, s_len, _ = q.shape
    rows = 16 * BAND
    prev_rows = BAND * dil
    per = rows // prev_rows
    main = pl.BlockSpec((1, n_lt, rows, LANES), lambda b, c: (b, 0, c, 0))
    prev = pl.BlockSpec((1, n_lt, prev_rows, LANES), lambda b, c: (b, 0, jnp.maximum(c * per - 1, 0), 0))
    return pl.pallas_call(
        functools.partial(_band_state_kernel, dil=dil),
        grid=(bsz, s_len // rows),
        in_specs=[main, main, main, prev, prev,
                  pl.BlockSpec((1, 2, _N_PAIRS, BAND, 4 * BAND), lambda b, c: (g, 0, 0, 0, 0))],
        out_specs=pl.BlockSpec((1, _N_STATE, rows, LANES), lambda b, c: (b, 0, c, 0)),
        out_shape=jax.ShapeDtypeStruct((bsz, _N_STATE, s_len, LANES), F32),
        compiler_params=_cparams("parallel", "arbitrary"),
        name=f"band_state_{g}",
    )(q, k, v, k, v, bias_tiles)


_DEINTERLEAVE = 4


def _band_merge_kernel(q_ref, k_ref, v_ref, st1_ref, st2_ref, bias_ref, o_ref, in_scr, out_scr, *, dil):
    inner = dil // _DEINTERLEAVE
    n_stage = q_ref.shape[2] // _DEINTERLEAVE
    sources = [(q_ref, p) for p in range(_N_PAIRS)] + [(k_ref, p) for p in range(_N_PAIRS)] \
        + [(v_ref, p) for p in range(_N_PAIRS)] + [(st1_ref, j) for j in range(_N_STATE)] \
        + [(st2_ref, j) for j in range(_N_STATE)]
    q0, k0, v0, s1, s2 = 0, _N_PAIRS, 2 * _N_PAIRS, 3 * _N_PAIRS, 3 * _N_PAIRS + _N_STATE
    own_bias = [jnp.concatenate([bias_ref[0, 0, p, :, BAND:2 * BAND], bias_ref[0, 0, p, :, 3 * BAND:]], axis=1)
                for p in range(_N_PAIRS)]
    ones_bd = _pair_ones(BAND)
    for r_outer in range(_DEINTERLEAVE):
        for n, (ref, idx) in enumerate(sources):
            in_scr[n] = ref[0, idx, pl.ds(r_outer, n_stage, stride=_DEINTERLEAVE), :]
        rows_of = [pl.ds(r_inner, BAND, stride=inner) for r_inner in range(inner)]
        load = lambda n, rows: in_scr[n, rows, :].astype(BF)
        units, where = [], []
        for rows in rows_of:
            for p in range(_N_PAIRS):
                units.append((load(q0 + p, rows), load(k0 + p, rows), load(v0 + p, rows), own_bias[p]))
                where.append((rows, p))
        for (rows, p), (out3, lse3) in zip(where, _band_pairs(units, ones_bd)):
            lse1, lse2 = in_scr[s1 + _N_PAIRS + p, rows, :], in_scr[s2 + _N_PAIRS + p, rows, :]
            top = jnp.maximum(jnp.maximum(lse1, lse2), lse3)
            w1, w2, w3 = jnp.exp(lse1 - top), jnp.exp(lse2 - top), jnp.exp(lse3 - top)
            num = w1 * in_scr[s1 + p, rows, :] + w2 * in_scr[s2 + p, rows, :] + w3 * out3
            out_scr[p, rows, :] = num / (w1 + w2 + w3)
        for p in range(_N_PAIRS):
            o_ref[0, p, pl.ds(r_outer, n_stage, stride=_DEINTERLEAVE), :] = out_scr[p]


def _band_merge(q, k, v, st1, st2, bias_tiles, g, dil):
    bsz, n_lt, s_len, _ = q.shape
    assert s_len == BAND * dil and dil % _DEINTERLEAVE == 0
    full = lambda n: pl.BlockSpec((1, n, s_len, LANES), lambda b: (b, 0, 0, 0))
    n_stage = s_len // _DEINTERLEAVE
    return pl.pallas_call(
        functools.partial(_band_merge_kernel, dil=dil),
        grid=(bsz,),
        in_specs=[full(n_lt), full(n_lt), full(n_lt), full(_N_STATE), full(_N_STATE),
                  pl.BlockSpec((1, 2, _N_PAIRS, BAND, 4 * BAND), lambda b: (g, 0, 0, 0, 0))],
        out_specs=full(n_lt),
        out_shape=jax.ShapeDtypeStruct((bsz, n_lt, s_len, LANES), F32),
        scratch_shapes=[pltpu.VMEM((3 * _N_PAIRS + 2 * _N_STATE, n_stage, LANES), F32),
                        pltpu.VMEM((_N_PAIRS, n_stage, LANES), F32)],
        compiler_params=_cparams("parallel"),
        name="band_merge",
    )(q, k, v, st1, st2, bias_tiles)


def _dilated(bq, bk, bv, bias_tiles):
    (_, d0), (_, d1), (_, d2) = B_PATTERNS
    st0 = _band_state(bq[0], bk, bv, bias_tiles, 0, d0)
    st1 = _band_state(bq[1], bk, bv, bias_tiles, 1, d1)
    return _band_merge(bq[2], bk, bv, st0, st1, bias_tiles, 2, d2)


def _block_mean_kernel(k_ref, o_ref):
    n_blk = k_ref.shape[1] // C_BLOCK
    o_ref[...] = jnp.zeros_like(o_ref)
    for j in range(n_blk):
        blk = k_ref[0, j * C_BLOCK:(j + 1) * C_BLOCK, :].astype(F32)
        o_ref[0, j:j + 1, :] = (jnp.sum(blk, axis=0, keepdims=True) * (1.0 / C_BLOCK)).astype(o_ref.dtype)


def _block_means(ck):
    bsz, s_len, w = ck.shape
    assert s_len // C_BLOCK <= BF16_ROWS
    return pl.pallas_call(
        _block_mean_kernel,
        grid=(bsz,),
        in_specs=[pl.BlockSpec((1, s_len, w), lambda b: (b, 0, 0))],
        out_specs=pl.BlockSpec((1, BF16_ROWS, w), lambda b: (b, 0, 0)),
        out_shape=jax.ShapeDtypeStruct((bsz, BF16_ROWS, w), BF),
        compiler_params=_cparams("parallel"),
        name="moba_block_means",
    )(ck)


def _moba_kernel(q_ref, k_ref, vt_ref, kmean_ref, bias_ref, _, o_ref, **static):
    for bb in range(q_ref.shape[0]):
        _moba_tile(*(_one_batch(r, bb) for r in (q_ref, k_ref, vt_ref, kmean_ref)), bias_ref,
                   _one_batch(o_ref, bb), **static)


def _moba_tile(q_ref, k_ref, vt_ref, kmean_ref, bias_ref, o_ref, *, cur, s_len):
    tq = q_ref.shape[1]
    t0 = (cur * TILES_PER_GROUP + pl.program_id(0)) * tq
    q = q_ref[0]
    kmean = kmean_ref[0]
    blk = lax.broadcasted_iota(jnp.int32, (BF16_ROWS, tq), 0)
    outs = []
    for p in range(C_HEADS // 2):
        ps = slice(p * LANES, (p + 1) * LANES)
        rhs = _head_pair_rhs(q[:, ps])
        logits = _dot_nt(k_ref[0, :, ps], rhs).astype(BF)
        gates = _dot_nt(kmean[:, ps], rhs)
        for half in range(2):
            h = 2 * p + half
            hq = slice(half * tq, (half + 1) * tq)
            if cur > C_TOPK:
                gate = jnp.where(blk < cur, gates[:, hq], -jnp.inf)
                off = jnp.where(blk == cur, 0.0, NEG)
                for j in range(cur):
                    gj = gate[j:j + 1, :]
                    beats = (gate > gj) | ((gate == gj) & (blk < j))
                    rank = jnp.sum(jnp.where(beats, 1.0, 0.0), axis=0, keepdims=True)
                    off = jnp.where((blk == j) & (rank < float(C_TOPK)), 0.0, off)
            pieces = []
            for j in range(cur + 1):
                ks = slice(j * C_BLOCK, (j + 1) * C_BLOCK)
                piece = logits[ks, hq] + _causal_bias(bias_ref, h, t0, j * C_BLOCK, C_BLOCK, tq, s_len)
                if cur > C_TOPK:
                    piece = piece + off[j:j + 1, :].astype(BF)
                pieces.append(piece)
            lg = jnp.concatenate(pieces, axis=0)
            outs.append(_softmax_pv(lg, _with_ones_rows(vt_ref[0, h * HEAD_DIM:(h + 1) * HEAD_DIM, :])))
    o_ref[0] = jnp.concatenate(outs, axis=0).T.astype(o_ref.dtype)


def _moba(cq, ck, cv_t, bias_master):
    bsz, s_len, w = cq.shape
    assert TILES_PER_GROUP * TQ == C_BLOCK
    kmean = _block_means(ck)
    out_shape = jax.ShapeDtypeStruct(cq.shape, BF)

    bps = _batch_per_step(bsz)

    def call_group(g, n_keys, out):
        tile0 = g * TILES_PER_GROUP
        return pl.pallas_call(
            functools.partial(_moba_kernel, cur=g, s_len=s_len),
            grid=(TILES_PER_GROUP, bsz // bps),
            in_specs=[pl.BlockSpec((bps, TQ, w), lambda i, b: (b, tile0 + i, 0)),
                      pl.BlockSpec((bps, n_keys, w), lambda i, b: (b, 0, 0)),
                      pl.BlockSpec((bps, w, n_keys), lambda i, b: (b, 0, 0)),
                      pl.BlockSpec((bps, BF16_ROWS, w), lambda i, b: (b, 0, 0)),
                      _const_spec2(bias_master.shape), pl.BlockSpec(memory_space=pl.ANY)],
            out_specs=pl.BlockSpec((bps, TQ, w), lambda i, b: (b, tile0 + i, 0)),
            out_shape=out_shape,
            input_output_aliases={5: 0},
            compiler_params=_cparams("arbitrary", "arbitrary"),
            name=f"moba_mixer_{g}",
        )(cq, ck, cv_t, kmean, bias_master, out)

    return _grouped_tiles(call_group, out_shape)


def _layer_norm(y, g_ref, b_ref):
    mu = jnp.mean(y, axis=-1, keepdims=True)
    yc = y - mu
    var = jnp.mean(yc * yc, axis=-1, keepdims=True)
    return yc * lax.rsqrt(var + LN_EPS) * g_ref[...] + b_ref[...]


def _merge_kernel(x_ref, oa_ref, ob_ref, oc_ref, wg_ref, wa_ref, wb_ref, wc_ref, wo_ref, g_ref, b_ref, y_ref):
    x = x_ref[...]
    xb = x.astype(BF)
    merged = None
    o_b = jnp.concatenate([ob_ref[0, lt] for lt in range(ob_ref.shape[1])], axis=1).astype(BF)
    for n, (o, w_ref) in enumerate(((oa_ref[...], wa_ref), (o_b, wb_ref), (oc_ref[...], wc_ref))):
        gate = jax.nn.sigmoid(_dot(xb, wg_ref[:, n * D_MODEL:(n + 1) * D_MODEL]))
        term = gate * _dot(o, w_ref[...])
        merged = term if merged is None else merged + term
    y = ALPHA * x + _dot(merged.astype(BF), wo_ref[...])
    y_ref[...] = _layer_norm(y, g_ref, b_ref)


def _const_spec(shape):
    return pl.BlockSpec(shape, lambda i: (0,) * len(shape), pipeline_mode=pl.Buffered(1))


def _merge(x2d, oa, ob, oc, wg, wa, wb, wc, wo, ln_g, ln_b, tm):
    n = x2d.shape[0]
    tiles_per_seq = ob.shape[2] // tm
    rows = lambda w: pl.BlockSpec((tm, w), lambda i: (i, 0))
    ob_spec = pl.BlockSpec((1, ob.shape[1], tm, LANES), lambda i: (i // tiles_per_seq, 0, i % tiles_per_seq, 0))
    return pl.pallas_call(
        _merge_kernel,
        grid=(n // tm,),
        in_specs=[rows(D_MODEL), rows(oa.shape[1]), ob_spec, rows(oc.shape[1]),
                  _const_spec(wg.shape), _const_spec(wa.shape), _const_spec(wb.shape), _const_spec(wc.shape),
                  _const_spec(wo.shape), _const_spec(ln_g.shape), _const_spec(ln_b.shape)],
        out_specs=rows(D_MODEL),
        out_shape=jax.ShapeDtypeStruct((n, D_MODEL), F32),
        compiler_params=_cparams("parallel"),
        name="merge_out_ln",
    )(x2d, oa, ob, oc, wg, wa, wb, wc, wo, ln_g, ln_b)


_FF_CHUNK = 1024


def _ffn_kernel(x_ref, p_ref, wu_ref, wd_ref, wpg_ref, wp_ref, g_ref, b_ref, y_ref):
    x = x_ref[...]
    xb = x.astype(BF)
    y = ALPHA * x + jax.nn.sigmoid(_dot(xb, wpg_ref[...])) * _dot(p_ref[...].astype(BF), wp_ref[...])
    for c in range(D_FF // _FF_CHUNK):
        cs = slice(c * _FF_CHUNK, (c + 1) * _FF_CHUNK)
        u = jnp.maximum(_dot(xb, wu_ref[:, cs]), 0.0)
        y = y + _dot((u * u).astype(BF), wd_ref[cs, :])
    y_ref[...] = _layer_norm(y, g_ref, b_ref)


def _ffn(x2d, p2d, wu, wd, wpg, wp, ln_g, ln_b, tm):
    n = x2d.shape[0]
    rows = lambda w: pl.BlockSpec((tm, w), lambda i: (i, 0))
    return pl.pallas_call(
        _ffn_kernel,
        grid=(n // tm,),
        in_specs=[rows(D_MODEL), rows(PLE_DIM), _const_spec(wu.shape), _const_spec(wd.shape),
                  _const_spec(wpg.shape), _const_spec(wp.shape), _const_spec(ln_g.shape), _const_spec(ln_b.shape)],
        out_specs=rows(D_MODEL),
        out_shape=jax.ShapeDtypeStruct((n, D_MODEL), F32),
        compiler_params=_cparams("parallel"),
        name="ffn_ple_ln",
    )(x2d, p2d, wu, wd, wpg, wp, ln_g, ln_b)


def kernel(x, p, w_in, w_gate, w_br_a, w_br_b, w_br_c, w_out, ln1_g, ln1_b,
           w_up, w_down, w_ple_gate, w_ple, ln2_g, ln2_b, rel_bias):
    bsz, s_len, d_model = x.shape
    assert d_model == D_MODEL and s_len == MAX_DISTANCE, (x.shape,)
    n_tok = bsz * s_len
    tm = 1024

    b_head0 = A_HEADS
    c_head0 = A_HEADS + B_GROUPS * B_SLOTS
    bias_a = _causal_bias_master(rel_bias, 0, A_HEADS, s_len, F32)
    bias_b = _band_bias_tiles(rel_bias, b_head0)
    bias_c = _causal_bias_master(rel_bias, c_head0, C_HEADS, s_len, BF)

    x2d = x.reshape(n_tok, D_MODEL)
    for i in range(DEPTH):
        pr = _project(x2d, *_pack_w_in(w_in[i]), tm, s_len)
        seq = lambda name: pr[name].reshape(bsz, s_len, -1)
        o_a = _dsa(seq("aq"), seq("iq"), pr["iwT"], seq("akk"), seq("aii"), pr["avT"], bias_a)
        o_b = _dilated((pr["bq0"], pr["bq1"], pr["bq2"]), pr["bk"], pr["bv"], bias_b)
        o_c = _moba(seq("cq"), seq("ck"), pr["cvT"], bias_c)
        flat = lambda a: a.reshape(n_tok, -1)
        row = lambda a: a.reshape(1, D_MODEL)
        x2d = _merge(x2d, flat(o_a), o_b, flat(o_c), w_gate[i].astype(BF), w_br_a[i].astype(BF),
                     w_br_b[i].astype(BF), w_br_c[i].astype(BF), w_out[i].astype(BF),
                     row(ln1_g[i]), row(ln1_b[i]), tm)
        x2d = _ffn(x2d, p[i].reshape(n_tok, PLE_DIM), w_up[i].astype(BF), w_down[i].astype(BF),
                   w_ple_gate[i].astype(BF), w_ple[i].astype(BF), row(ln2_g[i]), row(ln2_b[i]), tm)
    return x2d.reshape(bsz, s_len, D_MODEL)
```

```python
import functools
import math

import numpy as np
import jax
import jax.numpy as jnp
from jax import lax
from jax.experimental import pallas as pl
from jax.experimental.pallas import tpu as pltpu

D_MODEL = 1024
HEAD_DIM = 64
A_HEADS = 6
IDX_HEADS = 8
A_TOPK_MAX = 256
B_SLOTS = 4
B_PATTERNS = ((128, 1), (512, 4), (2048, 16))
B_GROUPS = 3
C_HEADS = 6
C_BLOCK = 256
C_TOPK = 3
N_BUCKETS = 32
MAX_DISTANCE = 2048
D_FF = 4 * D_MODEL
PLE_DIM = 256
DEPTH = 2
ALPHA = (2 * DEPTH) ** 0.25
LN_EPS = 1e-5
NEG = -1e30
QK_SCALE = HEAD_DIM ** -0.5

LANES = 128
SUBLANES = 8
BF16_ROWS = 16
VMEM_LIMIT = 56 * 1024 * 1024

TQ = 256
BIAS_TQ = LANES
TILES_PER_GROUP = C_BLOCK // TQ
BAND = 128
BF = jnp.bfloat16
F32 = jnp.float32
HALF_BITS = 16
HALF_MIN = -2 ** (HALF_BITS - 1)
FEW_TIES = 8.0

_NT = (((1,), (1,)), ((), ()))


def _dot(a, b):
    return jnp.dot(a, b, preferred_element_type=F32)


def _dot_nt(a, b):
    return lax.dot_general(a, b, _NT, preferred_element_type=F32)


def _cparams(*sem):
    return pltpu.CompilerParams(dimension_semantics=sem, vmem_limit_bytes=VMEM_LIMIT)


def _bucket_starts():
    d = np.arange(0, MAX_DISTANCE + 1)
    max_exact = N_BUCKETS // 2
    nf = np.maximum(d, 1).astype(np.float32)
    large = max_exact + (np.log(nf / np.float32(max_exact)) / np.float32(math.log(MAX_DISTANCE / max_exact))
                         * np.float32(N_BUCKETS - max_exact)).astype(np.int32)
    bucket = np.where(d < max_exact, d, np.minimum(large, N_BUCKETS - 1))
    return [int(np.argmax(bucket >= b)) if np.any(bucket >= b) else None for b in range(N_BUCKETS)]


_BUCKET_START = _bucket_starts()


def _bias_from_distance(dist, tab_ref, col):
    val = jnp.full(dist.shape, tab_ref[0, col], F32)
    for b in range(1, N_BUCKETS):
        if _BUCKET_START[b] is not None:
            val = jnp.where(dist >= _BUCKET_START[b], tab_ref[b, col], val)
    return jnp.where(dist < 0, NEG, val)


def _causal_bias_kernel(tab_ref, o_ref, *, head0):
    h = pl.program_id(0)
    n_rows, tq = o_ref.shape[1], o_ref.shape[2]
    u = lax.broadcasted_iota(jnp.int32, (n_rows, tq), 0)
    t = lax.broadcasted_iota(jnp.int32, (n_rows, tq), 1)
    o_ref[0] = _bias_from_distance(t + (n_rows - tq) // 2 - u, tab_ref, head0 + h).astype(o_ref.dtype)


def _causal_bias_master(rel_bias, head0, n_heads, s_len, dtype):
    n_rows = 2 * s_len - BIAS_TQ
    return pl.pallas_call(
        functools.partial(_causal_bias_kernel, head0=head0),
        grid=(n_heads,),
        in_specs=[pl.BlockSpec(memory_space=pltpu.SMEM)],
        out_specs=pl.BlockSpec((1, n_rows, BIAS_TQ), lambda h: (h, 0, 0)),
        out_shape=jax.ShapeDtypeStruct((n_heads, n_rows, BIAS_TQ), dtype),
        compiler_params=_cparams("arbitrary"),
        name="causal_bias_master",
    )(rel_bias)


def _bias_row0(t0, s_len):
    return pl.multiple_of(s_len - BIAS_TQ - t0, BIAS_TQ)


def _causal_bias(bias_ref, h, t0, first_key, n_keys, tq, s_len):
    parts = [bias_ref[h, pl.ds(_bias_row0(t0 + c * BIAS_TQ, s_len) + first_key, n_keys), :]
             for c in range(tq // BIAS_TQ)]
    return parts[0] if len(parts) == 1 else jnp.concatenate(parts, axis=1)


def _band_bias_kernel(tab_ref, o_ref, *, head0):
    g = pl.program_id(0)
    variant = pl.program_id(1)
    pair = pl.program_id(2)
    row = lax.broadcasted_iota(jnp.int32, (BAND, 2 * BAND), 0)
    col = lax.broadcasted_iota(jnp.int32, (BAND, 2 * BAND), 1)
    j = row + BAND - col
    no_prev = (variant == 1) & (col < BAND)
    for gi, (_, dil) in enumerate(B_PATTERNS):
        @pl.when(g == gi)
        def _(dil=dil):
            for half in range(2):
                bias = _bias_from_distance(j * dil, tab_ref, head0 + g * B_SLOTS + 2 * pair + half)
                bias = jnp.where((j > BAND) | no_prev, NEG, bias)
                o_ref[0, 0, 0, :, half * 2 * BAND:(half + 1) * 2 * BAND] = bias


def _band_bias_tiles(rel_bias, head0):
    return pl.pallas_call(
        functools.partial(_band_bias_kernel, head0=head0),
        grid=(B_GROUPS, 2, B_SLOTS // 2),
        in_specs=[pl.BlockSpec(memory_space=pltpu.SMEM)],
        out_specs=pl.BlockSpec((1, 1, 1, BAND, 4 * BAND), lambda g, v, p: (g, v, p, 0, 0)),
        out_shape=jax.ShapeDtypeStruct((B_GROUPS, 2, B_SLOTS // 2, BAND, 4 * BAND), F32),
        compiler_params=_cparams("arbitrary", "arbitrary", "arbitrary"),
        name="band_bias_tiles",
    )(rel_bias)


_PROJ_OUTS = (
    ("aq", A_HEADS * HEAD_DIM, QK_SCALE),
    ("akk", 2 * HEAD_DIM, 1.0),
    ("aii", 2 * HEAD_DIM, 1.0),
    ("iq", IDX_HEADS * HEAD_DIM, QK_SCALE),
    ("bq0", B_SLOTS * HEAD_DIM, QK_SCALE),
    ("bq1", B_SLOTS * HEAD_DIM, QK_SCALE),
    ("bq2", B_SLOTS * HEAD_DIM, QK_SCALE),
    ("bk", B_SLOTS * HEAD_DIM, 1.0),
    ("bv", B_SLOTS * HEAD_DIM, 1.0),
    ("cq", C_HEADS * HEAD_DIM, QK_SCALE),
    ("ck", C_HEADS * HEAD_DIM, 1.0),
)
_PROJ_WIDTH = sum(w for _, w, _ in _PROJ_OUTS)
_PROJ_LANE_TILED = ("bq0", "bq1", "bq2", "bk", "bv")
_PROJ_OUTS_T = (
    ("avT", HEAD_DIM, BF),
    ("cvT", C_HEADS * HEAD_DIM, BF),
    ("iwT", BF16_ROWS, F32),
)
_PROJ_ROWS_T = sum(r for _, r, _ in _PROJ_OUTS_T)


def _pack_w_in(w):
    widths = (384, 64, 64, 512, 64, 8, 768, 256, 256, 384, 384, 384)
    offs = np.concatenate([[0], np.cumsum(widths)])
    aq, ak, av, iq, ik, iw, bq, bk, bv, cq, ck, cv = (w[:, offs[n]:offs[n + 1]] for n in range(12))
    cols = jnp.concatenate([aq, ak, ak, ik, ik, iq, bq, bk, bv, cq, ck], axis=1)
    iw_pad = jnp.concatenate([iw, jnp.zeros((w.shape[0], BF16_ROWS - IDX_HEADS), w.dtype)], axis=1)
    rows = jnp.concatenate([av, cv, iw_pad], axis=1).T
    return cols.astype(BF), rows.astype(BF)


def _proj_kernel(x_ref, w_ref, wt_ref, *o_refs):
    xb = x_ref[...].astype(BF)
    off = 0
    for o_ref, (_, width, scale) in zip(o_refs, _PROJ_OUTS):
        res = _dot(xb, w_ref[:, off:off + width])
        if scale != 1.0:
            res = res * scale
        if len(o_ref.shape) == 4:
            for lt in range(width // LANES):
                o_ref[0, lt] = res[:, lt * LANES:(lt + 1) * LANES]
        else:
            o_ref[...] = res.astype(o_ref.dtype)
        off += width
    off = 0
    for o_ref, (_, rows, _) in zip(o_refs[len(_PROJ_OUTS):], _PROJ_OUTS_T):
        o_ref[0] = _dot_nt(wt_ref[off:off + rows, :], xb).astype(o_ref.dtype)
        off += rows


def _project(x2d, w_cols, w_rows, tm, s_len):
    n = x2d.shape[0]
    tiles_per_seq = s_len // tm
    out_shape, out_specs = [], []
    for name, w, _ in _PROJ_OUTS:
        if name in _PROJ_LANE_TILED:
            out_shape.append(jax.ShapeDtypeStruct((n // s_len, w // LANES, s_len, LANES), F32))
            out_specs.append(pl.BlockSpec((1, w // LANES, tm, LANES),
                                          lambda i: (i // tiles_per_seq, 0, i % tiles_per_seq, 0)))
        else:
            out_shape.append(jax.ShapeDtypeStruct((n, w), BF))
            out_specs.append(pl.BlockSpec((tm, w), lambda i: (i, 0)))
    out_shape += [jax.ShapeDtypeStruct((n // s_len, r, s_len), dt) for _, r, dt in _PROJ_OUTS_T]
    out_specs += [pl.BlockSpec((1, r, tm), lambda i: (i // tiles_per_seq, 0, i % tiles_per_seq))
                  for _, r, _ in _PROJ_OUTS_T]
    outs = pl.pallas_call(
        _proj_kernel,
        grid=(n // tm,),
        in_specs=[pl.BlockSpec((tm, D_MODEL), lambda i: (i, 0)),
                  pl.BlockSpec((D_MODEL, _PROJ_WIDTH), lambda i: (0, 0)),
                  pl.BlockSpec((_PROJ_ROWS_T, D_MODEL), lambda i: (0, 0))],
        out_specs=out_specs,
        out_shape=out_shape,
        compiler_params=_cparams("parallel"),
        name="in_proj",
    )(x2d, w_cols, w_rows)
    names = [name for name, _, _ in _PROJ_OUTS] + [name for name, _, _ in _PROJ_OUTS_T]
    return dict(zip(names, outs))


def _head_pair_rhs(pair):
    lane = lax.broadcasted_iota(jnp.int32, pair.shape, 1)
    zero = jnp.zeros_like(pair)
    return jnp.concatenate([jnp.where(lane < HEAD_DIM, pair, zero), jnp.where(lane < HEAD_DIM, zero, pair)], axis=0)


ROW_CHUNK = 8 * SUBLANES


def _max_rows(x):
    chunk = ROW_CHUNK * (4 // x.dtype.itemsize)
    acc = x[:chunk]
    for c in range(1, x.shape[0] // chunk):
        acc = jnp.maximum(acc, x[c * chunk:(c + 1) * chunk])
    return jnp.max(acc.astype(F32), axis=0, keepdims=True)


def _softmax_pv(logits, v_t_ones):
    m = _max_rows(logits)
    e = jnp.exp(logits - m.astype(logits.dtype)).astype(BF)
    o = _dot(v_t_ones, e)
    return o[:HEAD_DIM] / o[HEAD_DIM:HEAD_DIM + 1]


def _with_ones_rows(v_t):
    return jnp.concatenate([v_t, jnp.ones((BF16_ROWS, v_t.shape[1]), v_t.dtype)], axis=0)


def _const_spec2(shape):
    return pl.BlockSpec(shape, lambda i, b: (0,) * len(shape), pipeline_mode=pl.Buffered(1))


BATCH_PER_STEP = 2


def _batch_per_step(bsz):
    return BATCH_PER_STEP if bsz % BATCH_PER_STEP == 0 else 1


def _one_batch(ref, bb):
    return ref.at[pl.ds(bb, 1)]


def _grouped_tiles(call_group, out_shape):
    s_len = out_shape.shape[1]
    out = jnp.zeros(out_shape.shape, out_shape.dtype)
    for g in range(s_len // (TILES_PER_GROUP * TQ)):
        out = call_group(g, (g + 1) * TILES_PER_GROUP * TQ, out)
    return out


def _dsa_kernel(q_ref, iq_ref, iw_ref, kk_ref, ii_ref, vt_ref, bias_ref, _, o_ref, key_ref, half_ref, mask_ref, **static):
    for bb in range(q_ref.shape[0]):
        _dsa_tile(*(_one_batch(r, bb) for r in (q_ref, iq_ref, iw_ref, kk_ref, ii_ref, vt_ref)), bias_ref,
                  _one_batch(o_ref, bb), key_ref, half_ref, mask_ref, **static)


def _dsa_tile(q_ref, iq_ref, iw_ref, kk_ref, ii_ref, vt_ref, bias_ref, o_ref, key_ref, half_ref, mask_ref, *,
              topk, tile0, s_len):
    i = tile0 + pl.program_id(0)
    n_keys, tq = key_ref.shape
    pos = lax.broadcasted_iota(jnp.int32, (n_keys, tq), 0)
    qry = i * tq + lax.broadcasted_iota(jnp.int32, (n_keys, tq), 1)

    if n_keys > topk:
        iq = iq_ref[0]
        iw = iw_ref[0]
        ii = ii_ref[0]
        index = jnp.zeros((n_keys, tq), F32)
        for p in range(IDX_HEADS // 2):
            sc = _dot_nt(ii, _head_pair_rhs(iq[:, p * LANES:(p + 1) * LANES]))
            index = index + iw[2 * p:2 * p + 1, :] * jnp.maximum(sc[:, :tq], 0.0)
            index = index + iw[2 * p + 1:2 * p + 2, :] * jnp.maximum(sc[:, tq:], 0.0)
        index = jnp.where(pos <= qry, index + 0.0, -jnp.inf)
        bits = pltpu.bitcast(index, jnp.int32)
        key_ref[...] = jnp.where(bits < 0, bits ^ jnp.int32(0x7FFFFFFF), bits)

        chunk_pos = lax.broadcasted_iota(jnp.int32, (ROW_CHUNK, tq), 0)

        def count(pred):
            acc = None
            for c in range(n_keys // ROW_CHUNK):
                hit = pred(key_ref[c * ROW_CHUNK:(c + 1) * ROW_CHUNK, :], chunk_pos + c * ROW_CHUNK)
                ones = jnp.where(hit, 1.0, 0.0)
                acc = ones if acc is None else acc + ones
            return jnp.sum(acc, axis=0, keepdims=True)

        def count_half(pred):
            acc = None
            for c in range(n_keys // (2 * ROW_CHUNK)):
                hit = pred(half_ref[c * 2 * ROW_CHUNK:(c + 1) * 2 * ROW_CHUNK, :])
                ones = jnp.where(hit, jnp.int16(1), jnp.int16(0))
                acc = ones if acc is None else acc + ones
            return jnp.sum(acc.astype(F32), axis=0, keepdims=True)

        def search_half(need, count_at_min):
            c0 = count_half(lambda h: h >= 0)
            t0 = jnp.where(c0 >= need, 0, HALF_MIN).astype(jnp.int32)
            n0 = jnp.where(c0 >= need, c0, count_at_min)

            def step(it, carry):
                t, n = carry
                cand = t | (jnp.int32(1) << (HALF_BITS - 2 - it))
                cand16 = cand.astype(jnp.int16)
                c = count_half(lambda h: h >= cand16)
                ok = c >= need
                return jnp.where(ok, cand, t), jnp.where(ok, c, n)

            return lax.fori_loop(0, HALF_BITS - 1, step, (t0, n0))

        k_f = float(topk)
        half_ref[...] = (key_ref[...] >> HALF_BITS).astype(jnp.int16)
        t_hi, cnt_ge_hi = search_half(k_f, float(n_keys))
        t_hi16 = t_hi.astype(jnp.int16)
        cnt_gt_hi = count_half(lambda h: h > t_hi16)
        key = key_ref[...]
        low = (key & (2 ** HALF_BITS - 1)) + HALF_MIN
        half_ref[...] = jnp.where((key >> HALF_BITS) == t_hi, low, HALF_MIN).astype(jnp.int16)
        t_lo, cnt_lo = search_half(k_f - cnt_gt_hi, cnt_ge_hi - cnt_gt_hi)
        thr = t_hi * 2 ** HALF_BITS + (t_lo - HALF_MIN)
        cnt_ge = cnt_gt_hi + cnt_lo

        surplus = cnt_ge - k_f
        max_surplus = jnp.max(surplus)

        def drop_highest_ties():
            half_ref[...] = jnp.where(key_ref[...] == thr, pos, -1).astype(jnp.int16)

            def drop_round(r, cutoff):
                cutoff16 = cutoff.astype(jnp.int16)
                top = None
                for c in range(n_keys // (2 * ROW_CHUNK)):
                    tie_pos = half_ref[c * 2 * ROW_CHUNK:(c + 1) * 2 * ROW_CHUNK, :]
                    below = jnp.where(tie_pos < cutoff16, tie_pos, jnp.int16(-1))
                    top = below if top is None else jnp.where(below > top, below, top)
                highest = jnp.max(top.astype(jnp.int32), axis=0, keepdims=True)
                return jnp.where(surplus > r.astype(F32), highest, cutoff)

            return lax.fori_loop(0, max_surplus.astype(jnp.int32), drop_round, jnp.full((1, tq), n_keys, jnp.int32))

        def search_cutoff():
            remaining = k_f - count(lambda k, _: k > thr)
            n_bits = (n_keys - 1).bit_length()

            def tie_step(it, last):
                cand = last | (jnp.int32(1) << (n_bits - 1 - it))
                c = count(lambda k, kpos: (k == thr) & (kpos < cand))
                return jnp.where(c < remaining, cand, last)

            return lax.fori_loop(0, n_bits, tie_step, jnp.zeros((1, tq), jnp.int32)) + 1

        def write_mask(cutoff):
            key = key_ref[...]
            mask_ref[...] = jnp.where((key > thr) | ((key == thr) & (pos < cutoff)), 0.0, NEG)

        @pl.when(max_surplus == 0.0)
        def _():
            mask_ref[...] = jnp.where(key_ref[...] >= thr, 0.0, NEG)

        @pl.when((max_surplus > 0.0) & (max_surplus <= FEW_TIES))
        def _():
            write_mask(drop_highest_ties())

        @pl.when(max_surplus > FEW_TIES)
        def _():
            write_mask(search_cutoff())
    else:
        mask_ref[...] = jnp.zeros_like(mask_ref)

    q = q_ref[0]
    kk = kk_ref[0]
    v_t = _with_ones_rows(vt_ref[0])
    outs = []
    for p in range(A_HEADS // 2):
        logits = _dot_nt(kk, _head_pair_rhs(q[:, p * LANES:(p + 1) * LANES]))
        for half in range(2):
            bias = _causal_bias(bias_ref, 2 * p + half, i * tq, 0, n_keys, tq, s_len)
            lg = logits[:, half * tq:(half + 1) * tq] + bias + mask_ref[...]
            outs.append(_softmax_pv(lg, v_t))
    o_ref[0] = jnp.concatenate(outs, axis=0).T.astype(o_ref.dtype)


def _dsa(aq, iq, iw_t, akk, aii, av_t, bias_master):
    bsz, s_len, _ = aq.shape
    topk = min(A_TOPK_MAX, s_len // 4)
    out_shape = jax.ShapeDtypeStruct(aq.shape, BF)

    bps = _batch_per_step(bsz)

    def call_group(g, n_keys, out):
        tile0 = g * TILES_PER_GROUP
        q_spec = lambda w: pl.BlockSpec((bps, TQ, w), lambda i, b: (b, tile0 + i, 0))
        keys = lambda w: pl.BlockSpec((bps, n_keys, w), lambda i, b: (b, 0, 0))
        return pl.pallas_call(
            functools.partial(_dsa_kernel, topk=topk, tile0=tile0, s_len=s_len),
            grid=(TILES_PER_GROUP, bsz // bps),
            in_specs=[q_spec(A_HEADS * HEAD_DIM), q_spec(IDX_HEADS * HEAD_DIM),
                      pl.BlockSpec((bps, BF16_ROWS, TQ), lambda i, b: (b, 0, tile0 + i)),
                      keys(2 * HEAD_DIM), keys(2 * HEAD_DIM),
                      pl.BlockSpec((bps, HEAD_DIM, n_keys), lambda i, b: (b, 0, 0)),
                      _const_spec2(bias_master.shape), pl.BlockSpec(memory_space=pl.ANY)],
            out_specs=q_spec(A_HEADS * HEAD_DIM),
            out_shape=out_shape,
            input_output_aliases={7: 0},
            scratch_shapes=[pltpu.VMEM((n_keys, TQ), jnp.int32), pltpu.VMEM((n_keys, TQ), jnp.int16),
                            pltpu.VMEM((n_keys, TQ), F32)],
            compiler_params=_cparams("arbitrary", "arbitrary"),
            name=f"dsa_mixer_{g}",
        )(aq, iq, iw_t, akk, aii, av_t, bias_master, out)

    return _grouped_tiles(call_group, out_shape)


_PAIR_W = 2 * HEAD_DIM
_N_PAIRS = B_SLOTS // 2
_N_STATE = 2 * _N_PAIRS


def _class_rows(r, t, dil):
    start = r + t * BAND * dil
    return pl.ds(start, BAND, stride=dil) if dil > 1 else pl.ds(start, BAND)


def _pair_ones(n_keys):
    row = lax.broadcasted_iota(jnp.int32, (2 * n_keys, _PAIR_W), 0)
    lane = lax.broadcasted_iota(jnp.int32, (2 * n_keys, _PAIR_W), 1)
    return jnp.where((row < n_keys) == (lane < HEAD_DIM), 1.0, 0.0).astype(BF)


def _band_pairs(units, ones_bd):
    n_keys = units[0][1].shape[0]
    lane = lax.broadcasted_iota(jnp.int32, (BAND, _PAIR_W), 1)
    logits = [_dot_nt(q, _head_pair_rhs(k)) + bias for q, k, _, bias in units]
    maxes = [(jnp.max(lg[:, :n_keys], axis=1, keepdims=True), jnp.max(lg[:, n_keys:], axis=1, keepdims=True))
             for lg in logits]
    probs = [jnp.concatenate([jnp.exp(lg[:, :n_keys] - ma), jnp.exp(lg[:, n_keys:] - mb)], axis=1).astype(BF)
             for lg, (ma, mb) in zip(logits, maxes)]
    results = []
    for (_, _, v, _), e, (ma, mb) in zip(units, probs, maxes):
        acc = _dot(e, _head_pair_rhs(v))
        den = _dot(e, ones_bd)
        results.append((acc / den, jnp.where(lane < HEAD_DIM, ma, mb) + jnp.log(den)))
    return results


def _band_state_kernel(q_ref, k_ref, v_ref, kp_ref, vp_ref, bias_ref, st_ref, *, dil):
    c = pl.program_id(1)
    n_tiles = q_ref.shape[2] // (BAND * dil)
    first_variant = jnp.where(c == 0, 1, 0)
    load = lambda ref, p, rows: ref[0, p, rows, :].astype(BF)
    units, where = [], []
    for r in range(dil):
        for t in range(n_tiles):
            rows = _class_rows(r, t, dil)
            for p in range(_N_PAIRS):
                if t == 0:
                    prev = _class_rows(r, 0, dil)
                    k = jnp.concatenate([load(kp_ref, p, prev), load(k_ref, p, rows)], axis=0)
                    v = jnp.concatenate([load(vp_ref, p, prev), load(v_ref, p, rows)], axis=0)
                    bias = bias_ref[0, first_variant, p]
                else:
                    prev = _class_rows(r, t - 1, dil)
                    k = jnp.concatenate([load(k_ref, p, prev), load(k_ref, p, rows)], axis=0)
                    v = jnp.concatenate([load(v_ref, p, prev), load(v_ref, p, rows)], axis=0)
                    bias = bias_ref[0, 0, p]
                units.append((load(q_ref, p, rows), k, v, bias))
                where.append((rows, p))
    for (rows, p), (out, lse) in zip(where, _band_pairs(units, _pair_ones(2 * BAND))):
        st_ref[0, p, rows, :] = out
        st_ref[0, _N_PAIRS + p, rows, :] = lse


def _band_state(q, k, v, bias_tiles, g, dil):
    bsz, n_lt, s_len, _ = q.shape
    rows = 16 * BAND
    prev_rows = BAND * dil
    per = rows // prev_rows
    main = pl.BlockSpec((1, n_lt, rows, LANES), lambda b, c: (b, 0, c, 0))
    prev = pl.BlockSpec((1, n_lt, prev_rows, LANES), lambda b, c: (b, 0, jnp.maximum(c * per - 1, 0), 0))
    return pl.pallas_call(
        functools.partial(_band_state_kernel, dil=dil),
        grid=(bsz, s_len // rows),
        in_specs=[main, main, main, prev, prev,
                  pl.BlockSpec((1, 2, _N_PAIRS, BAND, 4 * BAND), lambda b, c: (g, 0, 0, 0, 0))],
        out_specs=pl.BlockSpec((1, _N_STATE, rows, LANES), lambda b, c: (b, 0, c, 0)),
        out_shape=jax.ShapeDtypeStruct((bsz, _N_STATE, s_len, LANES), F32),
        compiler_params=_cparams("parallel", "arbitrary"),
        name=f"band_state_{g}",
    )(q, k, v, k, v, bias_tiles)


_DEINTERLEAVE = 4


def _band_merge_kernel(q_ref, k_ref, v_ref, st1_ref, st2_ref, bias_ref, o_ref, in_scr, out_scr, *, dil):
    inner = dil // _DEINTERLEAVE
    n_stage = q_ref.shape[2] // _DEINTERLEAVE
    sources = [(q_ref, p) for p in range(_N_PAIRS)] + [(k_ref, p) for p in range(_N_PAIRS)] \
        + [(v_ref, p) for p in range(_N_PAIRS)] + [(st1_ref, j) for j in range(_N_STATE)] \
        + [(st2_ref, j) for j in range(_N_STATE)]
    q0, k0, v0, s1, s2 = 0, _N_PAIRS, 2 * _N_PAIRS, 3 * _N_PAIRS, 3 * _N_PAIRS + _N_STATE
    own_bias = [jnp.concatenate([bias_ref[0, 0, p, :, BAND:2 * BAND], bias_ref[0, 0, p, :, 3 * BAND:]], axis=1)
                for p in range(_N_PAIRS)]
    ones_bd = _pair_ones(BAND)
    for r_outer in range(_DEINTERLEAVE):
        for n, (ref, idx) in enumerate(sources):
            in_scr[n] = ref[0, idx, pl.ds(r_outer, n_stage, stride=_DEINTERLEAVE), :]
        rows_of = [pl.ds(r_inner, BAND, stride=inner) for r_inner in range(inner)]
        load = lambda n, rows: in_scr[n, rows, :].astype(BF)
        units, where = [], []
        for rows in rows_of:
            for p in range(_N_PAIRS):
                units.append((load(q0 + p, rows), load(k0 + p, rows), load(v0 + p, rows), own_bias[p]))
                where.append((rows, p))
        for (rows, p), (out3, lse3) in zip(where, _band_pairs(units, ones_bd)):
            lse1, lse2 = in_scr[s1 + _N_PAIRS + p, rows, :], in_scr[s2 + _N_PAIRS + p, rows, :]
            top = jnp.maximum(jnp.maximum(lse1, lse2), lse3)
            w1, w2, w3 = jnp.exp(lse1 - top), jnp.exp(lse2 - top), jnp.exp(lse3 - top)
            num = w1 * in_scr[s1 + p, rows, :] + w2 * in_scr[s2 + p, rows, :] + w3 * out3
            out_scr[p, rows, :] = num / (w1 + w2 + w3)
        for p in range(_N_PAIRS):
            o_ref[0, p, pl.ds(r_outer, n_stage, stride=_DEINTERLEAVE), :] = out_scr[p]


def _band_merge(q, k, v, st1, st2, bias_tiles, g, dil):
    bsz, n_lt, s_len, _ = q.shape
    assert s_len == BAND * dil and dil % _DEINTERLEAVE == 0
    full = lambda n: pl.BlockSpec((1, n, s_len, LANES), lambda b: (b, 0, 0, 0))
    n_stage = s_len // _DEINTERLEAVE
    return pl.pallas_call(
        functools.partial(_band_merge_kernel, dil=dil),
        grid=(bsz,),
        in_specs=[full(n_lt), full(n_lt), full(n_lt), full(_N_STATE), full(_N_STATE),
                  pl.BlockSpec((1, 2, _N_PAIRS, BAND, 4 * BAND), lambda b: (g, 0, 0, 0, 0))],
        out_specs=full(n_lt),
        out_shape=jax.ShapeDtypeStruct((bsz, n_lt, s_len, LANES), F32),
        scratch_shapes=[pltpu.VMEM((3 * _N_PAIRS + 2 * _N_STATE, n_stage, LANES), F32),
                        pltpu.VMEM((_N_PAIRS, n_stage, LANES), F32)],
        compiler_params=_cparams("parallel"),
        name="band_merge",
    )(q, k, v, st1, st2, bias_tiles)


def _dilated(bq, bk, bv, bias_tiles):
    (_, d0), (_, d1), (_, d2) = B_PATTERNS
    st0 = _band_state(bq[0], bk, bv, bias_tiles, 0, d0)
    st1 = _band_state(bq[1], bk, bv, bias_tiles, 1, d1)
    return _band_merge(bq[2], bk, bv, st0, st1, bias_tiles, 2, d2)


def _block_mean_kernel(k_ref, o_ref):
    n_blk = k_ref.shape[1] // C_BLOCK
    o_ref[...] = jnp.zeros_like(o_ref)
    for j in range(n_blk):
        blk = k_ref[0, j * C_BLOCK:(j + 1) * C_BLOCK, :].astype(F32)
        o_ref[0, j:j + 1, :] = (jnp.sum(blk, axis=0, keepdims=True) * (1.0 / C_BLOCK)).astype(o_ref.dtype)


def _block_means(ck):
    bsz, s_len, w = ck.shape
    assert s_len // C_BLOCK <= BF16_ROWS
    return pl.pallas_call(
        _block_mean_kernel,
        grid=(bsz,),
        in_specs=[pl.BlockSpec((1, s_len, w), lambda b: (b, 0, 0))],
        out_specs=pl.BlockSpec((1, BF16_ROWS, w), lambda b: (b, 0, 0)),
        out_shape=jax.ShapeDtypeStruct((bsz, BF16_ROWS, w), BF),
        compiler_params=_cparams("parallel"),
        name="moba_block_means",
    )(ck)


def _moba_kernel(q_ref, k_ref, vt_ref, kmean_ref, bias_ref, _, o_ref, **static):
    for bb in range(q_ref.shape[0]):
        _moba_tile(*(_one_batch(r, bb) for r in (q_ref, k_ref, vt_ref, kmean_ref)), bias_ref,
                   _one_batch(o_ref, bb), **static)


def _moba_tile(q_ref, k_ref, vt_ref, kmean_ref, bias_ref, o_ref, *, cur, s_len):
    tq = q_ref.shape[1]
    t0 = (cur * TILES_PER_GROUP + pl.program_id(0)) * tq
    q = q_ref[0]
    kmean = kmean_ref[0]
    blk = lax.broadcasted_iota(jnp.int32, (BF16_ROWS, tq), 0)
    outs = []
    for p in range(C_HEADS // 2):
        ps = slice(p * LANES, (p + 1) * LANES)
        rhs = _head_pair_rhs(q[:, ps])
        logits = _dot_nt(k_ref[0, :, ps], rhs).astype(BF)
        gates = _dot_nt(kmean[:, ps], rhs)
        for half in range(2):
            h = 2 * p + half
            hq = slice(half * tq, (half + 1) * tq)
            if cur > C_TOPK:
                gate = jnp.where(blk < cur, gates[:, hq], -jnp.inf)
                off = jnp.where(blk == cur, 0.0, NEG)
                for j in range(cur):
                    gj = gate[j:j + 1, :]
                    beats = (gate > gj) | ((gate == gj) & (blk < j))
                    rank = jnp.sum(jnp.where(beats, 1.0, 0.0), axis=0, keepdims=True)
                    off = jnp.where((blk == j) & (rank < float(C_TOPK)), 0.0, off)
            pieces = []
            for j in range(cur + 1):
                ks = slice(j * C_BLOCK, (j + 1) * C_BLOCK)
                piece = logits[ks, hq] + _causal_bias(bias_ref, h, t0, j * C_BLOCK, C_BLOCK, tq, s_len)
                if cur > C_TOPK:
                    piece = piece + off[j:j + 1, :].astype(BF)
                pieces.append(piece)
            lg = jnp.concatenate(pieces, axis=0)
            outs.append(_softmax_pv(lg, _with_ones_rows(vt_ref[0, h * HEAD_DIM:(h + 1) * HEAD_DIM, :])))
    o_ref[0] = jnp.concatenate(outs, axis=0).T.astype(o_ref.dtype)


def _moba(cq, ck, cv_t, bias_master):
    bsz, s_len, w = cq.shape
    assert TILES_PER_GROUP * TQ == C_BLOCK
    kmean = _block_means(ck)
    out_shape = jax.ShapeDtypeStruct(cq.shape, BF)

    bps = _batch_per_step(bsz)

    def call_group(g, n_keys, out):
        tile0 = g * TILES_PER_GROUP
        return pl.pallas_call(
            functools.partial(_moba_kernel, cur=g, s_len=s_len),
            grid=(TILES_PER_GROUP, bsz // bps),
            in_specs=[pl.BlockSpec((bps, TQ, w), lambda i, b: (b, tile0 + i, 0)),
                      pl.BlockSpec((bps, n_keys, w), lambda i, b: (b, 0, 0)),
                      pl.BlockSpec((bps, w, n_keys), lambda i, b: (b, 0, 0)),
                      pl.BlockSpec((bps, BF16_ROWS, w), lambda i, b: (b, 0, 0)),
                      _const_spec2(bias_master.shape), pl.BlockSpec(memory_space=pl.ANY)],
            out_specs=pl.BlockSpec((bps, TQ, w), lambda i, b: (b, tile0 + i, 0)),
            out_shape=out_shape,
            input_output_aliases={5: 0},
            compiler_params=_cparams("arbitrary", "arbitrary"),
            name=f"moba_mixer_{g}",
        )(cq, ck, cv_t, kmean, bias_master, out)

    return _grouped_tiles(call_group, out_shape)


def _layer_norm(y, g_ref, b_ref):
    mu = jnp.mean(y, axis=-1, keepdims=True)
    yc = y - mu
    var = jnp.mean(yc * yc, axis=-1, keepdims=True)
    return yc * lax.rsqrt(var + LN_EPS) * g_ref[...] + b_ref[...]


def _merge_kernel(x_ref, oa_ref, ob_ref, oc_ref, wg_ref, wa_ref, wb_ref, wc_ref, wo_ref, g_ref, b_ref, y_ref):
    x = x_ref[...]
    xb = x.astype(BF)
    merged = None
    o_b = jnp.concatenate([ob_ref[0, lt] for lt in range(ob_ref.shape[1])], axis=1).astype(BF)
    for n, (o, w_ref) in enumerate(((oa_ref[...], wa_ref), (o_b, wb_ref), (oc_ref[...], wc_ref))):
        gate = jax.nn.sigmoid(_dot(xb, wg_ref[:, n * D_MODEL:(n + 1) * D_MODEL]))
        term = gate * _dot(o, w_ref[...])
        merged = term if merged is None else merged + term
    y = ALPHA * x + _dot(merged.astype(BF), wo_ref[...])
    y_ref[...] = _layer_norm(y, g_ref, b_ref)


def _const_spec(shape):
    return pl.BlockSpec(shape, lambda i: (0,) * len(shape), pipeline_mode=pl.Buffered(1))


def _merge(x2d, oa, ob, oc, wg, wa, wb, wc, wo, ln_g, ln_b, tm):
    n = x2d.shape[0]
    tiles_per_seq = ob.shape[2] // tm
    rows = lambda w: pl.BlockSpec((tm, w), lambda i: (i, 0))
    ob_spec = pl.BlockSpec((1, ob.shape[1], tm, LANES), lambda i: (i // tiles_per_seq, 0, i % tiles_per_seq, 0))
    return pl.pallas_call(
        _merge_kernel,
        grid=(n // tm,),
        in_specs=[rows(D_MODEL), rows(oa.shape[1]), ob_spec, rows(oc.shape[1]),
                  _const_spec(wg.shape), _const_spec(wa.shape), _const_spec(wb.shape), _const_spec(wc.shape),
                  _const_spec(wo.shape), _const_spec(ln_g.shape), _const_spec(ln_b.shape)],
        out_specs=rows(D_MODEL),
        out_shape=jax.ShapeDtypeStruct((n, D_MODEL), F32),
        compiler_params=_cparams("parallel"),
        name="merge_out_ln",
    )(x2d, oa, ob, oc, wg, wa, wb, wc, wo, ln_g, ln_b)


_FF_CHUNK = 1024


def _ffn_kernel(x_ref, p_ref, wu_ref, wd_ref, wpg_ref, wp_ref, g_ref, b_ref, y_ref):
    x = x_ref[...]
    xb = x.astype(BF)
    y = ALPHA * x + jax.nn.sigmoid(_dot(xb, wpg_ref[...])) * _dot(p_ref[...].astype(BF), wp_ref[...])
    for c in range(D_FF // _FF_CHUNK):
        cs = slice(c * _FF_CHUNK, (c + 1) * _FF_CHUNK)
        u = jnp.maximum(_dot(xb, wu_ref[:, cs]), 0.0)
        y = y + _dot((u * u).astype(BF), wd_ref[cs, :])
    y_ref[...] = _layer_norm(y, g_ref, b_ref)


def _ffn(x2d, p2d, wu, wd, wpg, wp, ln_g, ln_b, tm):
    n = x2d.shape[0]
    rows = lambda w: pl.BlockSpec((tm, w), lambda i: (i, 0))
    return pl.pallas_call(
        _ffn_kernel,
        grid=(n // tm,),
        in_specs=[rows(D_MODEL), rows(PLE_DIM), _const_spec(wu.shape), _const_spec(wd.shape),
                  _const_spec(wpg.shape), _const_spec(wp.shape), _const_spec(ln_g.shape), _const_spec(ln_b.shape)],
        out_specs=rows(D_MODEL),
        out_shape=jax.ShapeDtypeStruct((n, D_MODEL), F32),
        compiler_params=_cparams("parallel"),
        name="ffn_ple_ln",
    )(x2d, p2d, wu, wd, wpg, wp, ln_g, ln_b)


def kernel(x, p, w_in, w_gate, w_br_a, w_br_b, w_br_c, w_out, ln1_g, ln1_b,
           w_up, w_down, w_ple_gate, w_ple, ln2_g, ln2_b, rel_bias):
    bsz, s_len, d_model = x.shape
    assert d_model == D_MODEL and s_len == MAX_DISTANCE, (x.shape,)
    n_tok = bsz * s_len
    tm = 1024

    b_head0 = A_HEADS
    c_head0 = A_HEADS + B_GROUPS * B_SLOTS
    bias_a = _causal_bias_master(rel_bias, 0, A_HEADS, s_len, F32)
    bias_b = _band_bias_tiles(rel_bias, b_head0)
    bias_c = _causal_bias_master(rel_bias, c_head0, C_HEADS, s_len, BF)

    x2d = x.reshape(n_tok, D_MODEL)
    for i in range(DEPTH):
        pr = _project(x2d, *_pack_w_in(w_in[i]), tm, s_len)
        seq = lambda name: pr[name].reshape(bsz, s_len, -1)
        o_a = _dsa(seq("aq"), seq("iq"), pr["iwT"], seq("akk"), seq("aii"), pr["avT"], bias_a)
        o_b = _dilated((pr["bq0"], pr["bq1"], pr["bq2"]), pr["bk"], pr["bv"], bias_b)
        o_c = _moba(seq("cq"), seq("ck"), pr["cvT"], bias_c)
        flat = lambda a: a.reshape(n_tok, -1)
        row = lambda a: a.reshape(1, D_MODEL)
        x2d = _merge(x2d, flat(o_a), o_b, flat(o_c), w_gate[i].astype(BF), w_br_a[i].astype(BF),
                     w_br_b[i].astype(BF), w_br_c[i].astype(BF), w_out[i].astype(BF),
                     row(ln1_g[i]), row(ln1_b[i]), tm)
        x2d = _ffn(x2d, p[i].reshape(n_tok, PLE_DIM), w_up[i].astype(BF), w_down[i].astype(BF),
                   w_ple_gate[i].astype(BF), w_ple[i].astype(BF), row(ln2_g[i]), row(ln2_b[i]), tm)
    return x2d.reshape(bsz, s_len, D_MODEL)
```

```python
import functools
import math

import numpy as np
import jax
import jax.numpy as jnp
from jax import lax
from jax.experimental import pallas as pl
from jax.experimental.pallas import tpu as pltpu

D_MODEL = 1024
HEAD_DIM = 64
A_HEADS = 6
IDX_HEADS = 8
A_TOPK_MAX = 256
B_SLOTS = 4
B_PATTERNS = ((128, 1), (512, 4), (2048, 16))
B_GROUPS = 3
C_HEADS = 6
C_BLOCK = 256
C_TOPK = 3
N_BUCKETS = 32
MAX_DISTANCE = 2048
D_FF = 4 * D_MODEL
PLE_DIM = 256
DEPTH = 2
ALPHA = (2 * DEPTH) ** 0.25
LN_EPS = 1e-5
NEG = -1e30
QK_SCALE = HEAD_DIM ** -0.5

LANES = 128
SUBLANES = 8
BF16_ROWS = 16
VMEM_LIMIT = 56 * 1024 * 1024

TQ = 256
BIAS_TQ = LANES
TILES_PER_GROUP = C_BLOCK // TQ
BAND = 128
BF = jnp.bfloat16
F32 = jnp.float32
HALF_BITS = 16
HALF_MIN = -2 ** (HALF_BITS - 1)
FEW_TIES = 8.0

_NT = (((1,), (1,)), ((), ()))


def _dot(a, b):
    return jnp.dot(a, b, preferred_element_type=F32)


def _dot_nt(a, b):
    return lax.dot_general(a, b, _NT, preferred_element_type=F32)


def _cparams(*sem):
    return pltpu.CompilerParams(dimension_semantics=sem, vmem_limit_bytes=VMEM_LIMIT)


def _bucket_starts():
    d = np.arange(0, MAX_DISTANCE + 1)
    max_exact = N_BUCKETS // 2
    nf = np.maximum(d, 1).astype(np.float32)
    large = max_exact + (np.log(nf / np.float32(max_exact)) / np.float32(math.log(MAX_DISTANCE / max_exact))
                         * np.float32(N_BUCKETS - max_exact)).astype(np.int32)
    bucket = np.where(d < max_exact, d, np.minimum(large, N_BUCKETS - 1))
    return [int(np.argmax(bucket >= b)) if np.any(bucket >= b) else None for b in range(N_BUCKETS)]


_BUCKET_START = _bucket_starts()


def _bias_from_distance(dist, tab_ref, col):
    val = jnp.full(dist.shape, tab_ref[0, col], F32)
    for b in range(1, N_BUCKETS):
        if _BUCKET_START[b] is not None:
            val = jnp.where(dist >= _BUCKET_START[b], tab_ref[b, col], val)
    return jnp.where(dist < 0, NEG, val)


def _causal_bias_kernel(tab_ref, o_ref, *, head0):
    h = pl.program_id(0)
    n_rows, tq = o_ref.shape[1], o_ref.shape[2]
    u = lax.broadcasted_iota(jnp.int32, (n_rows, tq), 0)
    t = lax.broadcasted_iota(jnp.int32, (n_rows, tq), 1)
    o_ref[0] = _bias_from_distance(t + (n_rows - tq) // 2 - u, tab_ref, head0 + h).astype(o_ref.dtype)


def _causal_bias_master(rel_bias, head0, n_heads, s_len, dtype):
    n_rows = 2 * s_len - BIAS_TQ
    return pl.pallas_call(
        functools.partial(_causal_bias_kernel, head0=head0),
        grid=(n_heads,),
        in_specs=[pl.BlockSpec(memory_space=pltpu.SMEM)],
        out_specs=pl.BlockSpec((1, n_rows, BIAS_TQ), lambda h: (h, 0, 0)),
        out_shape=jax.ShapeDtypeStruct((n_heads, n_rows, BIAS_TQ), dtype),
        compiler_params=_cparams("arbitrary"),
        name="causal_bias_master",
    )(rel_bias)


def _bias_row0(t0, s_len):
    return pl.multiple_of(s_len - BIAS_TQ - t0, BIAS_TQ)


def _causal_bias(bias_ref, h, t0, first_key, n_keys, tq, s_len):
    parts = [bias_ref[h, pl.ds(_bias_row0(t0 + c * BIAS_TQ, s_len) + first_key, n_keys), :]
             for c in range(tq // BIAS_TQ)]
    return parts[0] if len(parts) == 1 else jnp.concatenate(parts, axis=1)


def _band_bias_kernel(tab_ref, o_ref, *, head0):
    g = pl.program_id(0)
    variant = pl.program_id(1)
    pair = pl.program_id(2)
    row = lax.broadcasted_iota(jnp.int32, (BAND, 2 * BAND), 0)
    col = lax.broadcasted_iota(jnp.int32, (BAND, 2 * BAND), 1)
    j = row + BAND - col
    no_prev = (variant == 1) & (col < BAND)
    for gi, (_, dil) in enumerate(B_PATTERNS):
        @pl.when(g == gi)
        def _(dil=dil):
            for half in range(2):
                bias = _bias_from_distance(j * dil, tab_ref, head0 + g * B_SLOTS + 2 * pair + half)
                bias = jnp.where((j > BAND) | no_prev, NEG, bias)
                o_ref[0, 0, 0, :, half * 2 * BAND:(half + 1) * 2 * BAND] = bias


def _band_bias_tiles(rel_bias, head0):
    return pl.pallas_call(
        functools.partial(_band_bias_kernel, head0=head0),
        grid=(B_GROUPS, 2, B_SLOTS // 2),
        in_specs=[pl.BlockSpec(memory_space=pltpu.SMEM)],
        out_specs=pl.BlockSpec((1, 1, 1, BAND, 4 * BAND), lambda g, v, p: (g, v, p, 0, 0)),
        out_shape=jax.ShapeDtypeStruct((B_GROUPS, 2, B_SLOTS // 2, BAND, 4 * BAND), F32),
        compiler_params=_cparams("arbitrary", "arbitrary", "arbitrary"),
        name="band_bias_tiles",
    )(rel_bias)


_PROJ_OUTS = (
    ("aq", A_HEADS * HEAD_DIM, QK_SCALE),
    ("akk", 2 * HEAD_DIM, 1.0),
    ("aii", 2 * HEAD_DIM, 1.0),
    ("iq", IDX_HEADS * HEAD_DIM, QK_SCALE),
    ("bq0", B_SLOTS * HEAD_DIM, QK_SCALE),
    ("bq1", B_SLOTS * HEAD_DIM, QK_SCALE),
    ("bq2", B_SLOTS * HEAD_DIM, QK_SCALE),
    ("bk", B_SLOTS * HEAD_DIM, 1.0),
    ("bv", B_SLOTS * HEAD_DIM, 1.0),
    ("cq", C_HEADS * HEAD_DIM, QK_SCALE),
    ("ck", C_HEADS * HEAD_DIM, 1.0),
)
_PROJ_WIDTH = sum(w for _, w, _ in _PROJ_OUTS)
_PROJ_GROUP_W = 1024
_PROJ_LANE_TILED = ("bq0", "bq1", "bq2", "bk", "bv")
_PROJ_OUTS_T = (
    ("avT", HEAD_DIM, BF),
    ("cvT", C_HEADS * HEAD_DIM, BF),
    ("iwT", BF16_ROWS, F32),
)
_PROJ_ROWS_T = sum(r for _, r, _ in _PROJ_OUTS_T)


def _pack_w_in(w):
    widths = (384, 64, 64, 512, 64, 8, 768, 256, 256, 384, 384, 384)
    offs = np.concatenate([[0], np.cumsum(widths)])
    aq, ak, av, iq, ik, iw, bq, bk, bv, cq, ck, cv = (w[:, offs[n]:offs[n + 1]] for n in range(12))
    cols = jnp.concatenate([aq, ak, ak, ik, ik, iq, bq, bk, bv, cq, ck], axis=1)
    iw_pad = jnp.concatenate([iw, jnp.zeros((w.shape[0], BF16_ROWS - IDX_HEADS), w.dtype)], axis=1)
    rows = jnp.concatenate([av, cv, iw_pad], axis=1).T
    return cols.astype(BF), rows.astype(BF)


def _proj_kernel(x_ref, w_ref, wt_ref, *o_refs):
    xb = x_ref[...].astype(BF)
    off = 0
    todo = list(zip(o_refs, _PROJ_OUTS))
    while todo:
        group, group_width = [], 0
        while todo and (not group or group_width + todo[0][1][1] <= _PROJ_GROUP_W):
            group.append(todo.pop(0))
            group_width += group[-1][1][1]
        big = _dot(xb, w_ref[:, off:off + group_width])
        col = 0
        for o_ref, (_, width, scale) in group:
            res = big[:, col:col + width]
            if scale != 1.0:
                res = res * scale
            if len(o_ref.shape) == 4:
                for lt in range(width // LANES):
                    o_ref[0, lt] = res[:, lt * LANES:(lt + 1) * LANES]
            else:
                o_ref[...] = res.astype(o_ref.dtype)
            col += width
        off += group_width
    big_t = _dot_nt(wt_ref[...], xb)
    off = 0
    for o_ref, (_, rows, _) in zip(o_refs[len(_PROJ_OUTS):], _PROJ_OUTS_T):
        o_ref[0] = big_t[off:off + rows].astype(o_ref.dtype)
        off += rows


def _project(x2d, w_cols, w_rows, tm, s_len):
    n = x2d.shape[0]
    tiles_per_seq = s_len // tm
    out_shape, out_specs = [], []
    for name, w, _ in _PROJ_OUTS:
        if name in _PROJ_LANE_TILED:
            out_shape.append(jax.ShapeDtypeStruct((n // s_len, w // LANES, s_len, LANES), F32))
            out_specs.append(pl.BlockSpec((1, w // LANES, tm, LANES),
                                          lambda i: (i // tiles_per_seq, 0, i % tiles_per_seq, 0)))
        else:
            out_shape.append(jax.ShapeDtypeStruct((n, w), BF))
            out_specs.append(pl.BlockSpec((tm, w), lambda i: (i, 0)))
    out_shape += [jax.ShapeDtypeStruct((n // s_len, r, s_len), dt) for _, r, dt in _PROJ_OUTS_T]
    out_specs += [pl.BlockSpec((1, r, tm), lambda i: (i // tiles_per_seq, 0, i % tiles_per_seq))
                  for _, r, _ in _PROJ_OUTS_T]
    outs = pl.pallas_call(
        _proj_kernel,
        grid=(n // tm,),
        in_specs=[pl.BlockSpec((tm, D_MODEL), lambda i: (i, 0)),
                  pl.BlockSpec((D_MODEL, _PROJ_WIDTH), lambda i: (0, 0)),
                  pl.BlockSpec((_PROJ_ROWS_T, D_MODEL), lambda i: (0, 0))],
        out_specs=out_specs,
        out_shape=out_shape,
        compiler_params=_cparams("parallel"),
        name="in_proj",
    )(x2d, w_cols, w_rows)
    names = [name for name, _, _ in _PROJ_OUTS] + [name for name, _, _ in _PROJ_OUTS_T]
    return dict(zip(names, outs))


def _head_pair_rhs(pair):
    lane = lax.broadcasted_iota(jnp.int32, pair.shape, 1)
    zero = jnp.zeros_like(pair)
    return jnp.concatenate([jnp.where(lane < HEAD_DIM, pair, zero), jnp.where(lane < HEAD_DIM, zero, pair)], axis=0)


ROW_CHUNK = 8 * SUBLANES


def _max_rows(x):
    chunk = ROW_CHUNK * (4 // x.dtype.itemsize)
    acc = x[:chunk]
    for c in range(1, x.shape[0] // chunk):
        acc = jnp.maximum(acc, x[c * chunk:(c + 1) * chunk])
    return jnp.max(acc.astype(F32), axis=0, keepdims=True)


def _softmax_pv(logits, v_t_ones):
    m = _max_rows(logits)
    e = jnp.exp(logits - m.astype(logits.dtype)).astype(BF)
    o = _dot(v_t_ones, e)
    return o[:HEAD_DIM] / o[HEAD_DIM:HEAD_DIM + 1]


def _with_ones_rows(v_t):
    return jnp.concatenate([v_t, jnp.ones((BF16_ROWS, v_t.shape[1]), v_t.dtype)], axis=0)


def _const_spec2(shape):
    return pl.BlockSpec(shape, lambda i, b: (0,) * len(shape), pipeline_mode=pl.Buffered(1))


BATCH_PER_STEP = 2


def _batch_per_step(bsz):
    return BATCH_PER_STEP if bsz % BATCH_PER_STEP == 0 else 1


def _one_batch(ref, bb):
    return ref.at[pl.ds(bb, 1)]


def _grouped_tiles(call_group, out_shape):
    s_len = out_shape.shape[1]
    out = jnp.zeros(out_shape.shape, out_shape.dtype)
    for g in range(s_len // (TILES_PER_GROUP * TQ)):
        out = call_group(g, (g + 1) * TILES_PER_GROUP * TQ, out)
    return out


def _dsa_kernel(q_ref, iq_ref, iw_ref, kk_ref, ii_ref, vt_ref, bias_ref, _, o_ref, key_ref, half_ref, mask_ref, **static):
    for bb in range(q_ref.shape[0]):
        _dsa_tile(*(_one_batch(r, bb) for r in (q_ref, iq_ref, iw_ref, kk_ref, ii_ref, vt_ref)), bias_ref,
                  _one_batch(o_ref, bb), key_ref, half_ref, mask_ref, **static)


def _dsa_tile(q_ref, iq_ref, iw_ref, kk_ref, ii_ref, vt_ref, bias_ref, o_ref, key_ref, half_ref, mask_ref, *,
              topk, tile0, s_len):
    i = tile0 + pl.program_id(0)
    n_keys, tq = key_ref.shape
    pos = lax.broadcasted_iota(jnp.int32, (n_keys, tq), 0)
    qry = i * tq + lax.broadcasted_iota(jnp.int32, (n_keys, tq), 1)

    if n_keys > topk:
        iq = iq_ref[0]
        iw = iw_ref[0]
        ii = ii_ref[0]
        index = jnp.zeros((n_keys, tq), F32)
        for p in range(IDX_HEADS // 2):
            sc = _dot_nt(ii, _head_pair_rhs(iq[:, p * LANES:(p + 1) * LANES]))
            index = index + iw[2 * p:2 * p + 1, :] * jnp.maximum(sc[:, :tq], 0.0)
            index = index + iw[2 * p + 1:2 * p + 2, :] * jnp.maximum(sc[:, tq:], 0.0)
        index = jnp.where(pos <= qry, index + 0.0, -jnp.inf)
        bits = pltpu.bitcast(index, jnp.int32)
        key_ref[...] = jnp.where(bits < 0, bits ^ jnp.int32(0x7FFFFFFF), bits)

        chunk_pos = lax.broadcasted_iota(jnp.int32, (ROW_CHUNK, tq), 0)

        def count(pred):
            acc = None
            for c in range(n_keys // ROW_CHUNK):
                hit = pred(key_ref[c * ROW_CHUNK:(c + 1) * ROW_CHUNK, :], chunk_pos + c * ROW_CHUNK)
                ones = jnp.where(hit, 1.0, 0.0)
                acc = ones if acc is None else acc + ones
            return jnp.sum(acc, axis=0, keepdims=True)

        def count_half(pred):
            acc = None
            for c in range(n_keys // (2 * ROW_CHUNK)):
                hit = pred(half_ref[c * 2 * ROW_CHUNK:(c + 1) * 2 * ROW_CHUNK, :])
                ones = jnp.where(hit, jnp.int16(1), jnp.int16(0))
                acc = ones if acc is None else acc + ones
            return jnp.sum(acc.astype(F32), axis=0, keepdims=True)

        def search_half(need, count_at_min):
            c0 = count_half(lambda h: h >= 0)
            t0 = jnp.where(c0 >= need, 0, HALF_MIN).astype(jnp.int32)
            n0 = jnp.where(c0 >= need, c0, count_at_min)

            def step(it, carry):
                t, n = carry
                cand = t | (jnp.int32(1) << (HALF_BITS - 2 - it))
                cand16 = cand.astype(jnp.int16)
                c = count_half(lambda h: h >= cand16)
                ok = c >= need
                return jnp.where(ok, cand, t), jnp.where(ok, c, n)

            return lax.fori_loop(0, HALF_BITS - 1, step, (t0, n0))

        k_f = float(topk)
        half_ref[...] = (key_ref[...] >> HALF_BITS).astype(jnp.int16)
        t_hi, cnt_ge_hi = search_half(k_f, float(n_keys))
        t_hi16 = t_hi.astype(jnp.int16)
        cnt_gt_hi = count_half(lambda h: h > t_hi16)
        key = key_ref[...]
        low = (key & (2 ** HALF_BITS - 1)) + HALF_MIN
        half_ref[...] = jnp.where((key >> HALF_BITS) == t_hi, low, HALF_MIN).astype(jnp.int16)
        t_lo, cnt_lo = search_half(k_f - cnt_gt_hi, cnt_ge_hi - cnt_gt_hi)
        thr = t_hi * 2 ** HALF_BITS + (t_lo - HALF_MIN)
        cnt_ge = cnt_gt_hi + cnt_lo

        surplus = cnt_ge - k_f
        max_surplus = jnp.max(surplus)

        def drop_highest_ties():
            half_ref[...] = jnp.where(key_ref[...] == thr, pos, -1).astype(jnp.int16)

            def drop_round(r, cutoff):
                cutoff16 = cutoff.astype(jnp.int16)
                top = None
                for c in range(n_keys // (2 * ROW_CHUNK)):
                    tie_pos = half_ref[c * 2 * ROW_CHUNK:(c + 1) * 2 * ROW_CHUNK, :]
                    below = jnp.where(tie_pos < cutoff16, tie_pos, jnp.int16(-1))
                    top = below if top is None else jnp.where(below > top, below, top)
                highest = jnp.max(top.astype(jnp.int32), axis=0, keepdims=True)
                return jnp.where(surplus > r.astype(F32), highest, cutoff)

            return lax.fori_loop(0, max_surplus.astype(jnp.int32), drop_round, jnp.full((1, tq), n_keys, jnp.int32))

        def search_cutoff():
            remaining = k_f - count(lambda k, _: k > thr)
            n_bits = (n_keys - 1).bit_length()

            def tie_step(it, last):
                cand = last | (jnp.int32(1) << (n_bits - 1 - it))
                c = count(lambda k, kpos: (k == thr) & (kpos < cand))
                return jnp.where(c < remaining, cand, last)

            return lax.fori_loop(0, n_bits, tie_step, jnp.zeros((1, tq), jnp.int32)) + 1

        def write_mask(cutoff):
            key = key_ref[...]
            mask_ref[...] = jnp.where((key > thr) | ((key == thr) & (pos < cutoff)), 0.0, NEG)

        @pl.when(max_surplus == 0.0)
        def _():
            mask_ref[...] = jnp.where(key_ref[...] >= thr, 0.0, NEG)

        @pl.when((max_surplus > 0.0) & (max_surplus <= FEW_TIES))
        def _():
            write_mask(drop_highest_ties())

        @pl.when(max_surplus > FEW_TIES)
        def _():
            write_mask(search_cutoff())
    else:
        mask_ref[...] = jnp.zeros_like(mask_ref)

    q = q_ref[0]
    kk = kk_ref[0]
    v_t = _with_ones_rows(vt_ref[0])
    outs = []
    for p in range(A_HEADS // 2):
        logits = _dot_nt(kk, _head_pair_rhs(q[:, p * LANES:(p + 1) * LANES]))
        for half in range(2):
            bias = _causal_bias(bias_ref, 2 * p + half, i * tq, 0, n_keys, tq, s_len)
            lg = logits[:, half * tq:(half + 1) * tq] + bias + mask_ref[...]
            outs.append(_softmax_pv(lg, v_t))
    o_ref[0] = jnp.concatenate(outs, axis=0).T.astype(o_ref.dtype)


def _dsa(aq, iq, iw_t, akk, aii, av_t, bias_master):
    bsz, s_len, _ = aq.shape
    topk = min(A_TOPK_MAX, s_len // 4)
    out_shape = jax.ShapeDtypeStruct(aq.shape, BF)

    bps = _batch_per_step(bsz)

    def call_group(g, n_keys, out):
        tile0 = g * TILES_PER_GROUP
        q_spec = lambda w: pl.BlockSpec((bps, TQ, w), lambda i, b: (b, tile0 + i, 0))
        keys = lambda w: pl.BlockSpec((bps, n_keys, w), lambda i, b: (b, 0, 0))
        return pl.pallas_call(
            functools.partial(_dsa_kernel, topk=topk, tile0=tile0, s_len=s_len),
            grid=(TILES_PER_GROUP, bsz // bps),
            in_specs=[q_spec(A_HEADS * HEAD_DIM), q_spec(IDX_HEADS * HEAD_DIM),
                      pl.BlockSpec((bps, BF16_ROWS, TQ), lambda i, b: (b, 0, tile0 + i)),
                      keys(2 * HEAD_DIM), keys(2 * HEAD_DIM),
                      pl.BlockSpec((bps, HEAD_DIM, n_keys), lambda i, b: (b, 0, 0)),
                      _const_spec2(bias_master.shape), pl.BlockSpec(memory_space=pl.ANY)],
            out_specs=q_spec(A_HEADS * HEAD_DIM),
            out_shape=out_shape,
            input_output_aliases={7: 0},
            scratch_shapes=[pltpu.VMEM((n_keys, TQ), jnp.int32), pltpu.VMEM((n_keys, TQ), jnp.int16),
                            pltpu.VMEM((n_keys, TQ), F32)],
            compiler_params=_cparams("arbitrary", "arbitrary"),
            name=f"dsa_mixer_{g}",
        )(aq, iq, iw_t, akk, aii, av_t, bias_master, out)

    return _grouped_tiles(call_group, out_shape)


_PAIR_W = 2 * HEAD_DIM
_N_PAIRS = B_SLOTS // 2
_N_STATE = 2 * _N_PAIRS


def _class_rows(r, t, dil):
    start = r + t * BAND * dil
    return pl.ds(start, BAND, stride=dil) if dil > 1 else pl.ds(start, BAND)


def _pair_ones(n_keys):
    row = lax.broadcasted_iota(jnp.int32, (2 * n_keys, _PAIR_W), 0)
    lane = lax.broadcasted_iota(jnp.int32, (2 * n_keys, _PAIR_W), 1)
    return jnp.where((row < n_keys) == (lane < HEAD_DIM), 1.0, 0.0).astype(BF)


def _band_pairs(units, ones_bd):
    n_keys = units[0][1].shape[0]
    lane = lax.broadcasted_iota(jnp.int32, (BAND, _PAIR_W), 1)
    logits = [_dot_nt(q, _head_pair_rhs(k)) + bias for q, k, _, bias in units]
    maxes = [(jnp.max(lg[:, :n_keys], axis=1, keepdims=True), jnp.max(lg[:, n_keys:], axis=1, keepdims=True))
             for lg in logits]
    probs = [jnp.concatenate([jnp.exp(lg[:, :n_keys] - ma), jnp.exp(lg[:, n_keys:] - mb)], axis=1).astype(BF)
             for lg, (ma, mb) in zip(logits, maxes)]
    results = []
    for (_, _, v, _), e, (ma, mb) in zip(units, probs, maxes):
        acc = _dot(e, _head_pair_rhs(v))
        den = _dot(e, ones_bd)
        results.append((acc / den, jnp.where(lane < HEAD_DIM, ma, mb) + jnp.log(den)))
    return results


def _band_state_kernel(q_ref, k_ref, v_ref, kp_ref, vp_ref, bias_ref, st_ref, *, dil):
    c = pl.program_id(1)
    n_tiles = q_ref.shape[2] // (BAND * dil)
    first_variant = jnp.where(c == 0, 1, 0)
    load = lambda ref, p, rows: ref[0, p, rows, :].astype(BF)
    units, where = [], []
    for r in range(dil):
        for t in range(n_tiles):
            rows = _class_rows(r, t, dil)
            for p in range(_N_PAIRS):
                if t == 0:
                    prev = _class_rows(r, 0, dil)
                    k = jnp.concatenate([load(kp_ref, p, prev), load(k_ref, p, rows)], axis=0)
                    v = jnp.concatenate([load(vp_ref, p, prev), load(v_ref, p, rows)], axis=0)
                    bias = bias_ref[0, first_variant, p]
                else:
                    prev = _class_rows(r, t - 1, dil)
                    k = jnp.concatenate([load(k_ref, p, prev), load(k_ref, p, rows)], axis=0)
                    v = jnp.concatenate([load(v_ref, p, prev), load(v_ref, p, rows)], axis=0)
                    bias = bias_ref[0, 0, p]
                units.append((load(q_ref, p, rows), k, v, bias))
                where.append((rows, p))
    for (rows, p), (out, lse) in zip(where, _band_pairs(units, _pair_ones(2 * BAND))):
        st_ref[0, p, rows, :] = out
        st_ref[0, _N_PAIRS + p, rows, :] = lse


def _band_state(q, k, v, bias_tiles, g, dil):
    bsz, n_lt, s_len, _ = q.shape
    rows = 16 * BAND
    prev_rows = BAND * dil
    per = rows // prev_rows
    main = pl.BlockSpec((1, n_lt, rows, LANES), lambda b, c: (b, 0, c, 0))
    prev = pl.BlockSpec((1, n_lt, prev_rows, LANES), lambda b, c: (b, 0, jnp.maximum(c * per - 1, 0), 0))
    return pl.pallas_call(
        functools.partial(_band_state_kernel, dil=dil),
        grid=(bsz, s_len // rows),
        in_specs=[main, main, main, prev, prev,
                  pl.BlockSpec((1, 2, _N_PAIRS, BAND, 4 * BAND), lambda b, c: (g, 0, 0, 0, 0))],
        out_specs=pl.BlockSpec((1, _N_STATE, rows, LANES), lambda b, c: (b, 0, c, 0)),
        out_shape=jax.ShapeDtypeStruct((bsz, _N_STATE, s_len, LANES), F32),
        compiler_params=_cparams("parallel", "arbitrary"),
        name=f"band_state_{g}",
    )(q, k, v, k, v, bias_tiles)


_DEINTERLEAVE = 4


def _band_merge_kernel(q_ref, k_ref, v_ref, st1_ref, st2_ref, bias_ref, o_ref, in_scr, out_scr, *, dil):
    inner = dil // _DEINTERLEAVE
    n_stage = q_ref.shape[2] // _DEINTERLEAVE
    sources = [(q_ref, p) for p in range(_N_PAIRS)] + [(k_ref, p) for p in range(_N_PAIRS)] \
        + [(v_ref, p) for p in range(_N_PAIRS)] + [(st1_ref, j) for j in range(_N_STATE)] \
        + [(st2_ref, j) for j in range(_N_STATE)]
    q0, k0, v0, s1, s2 = 0, _N_PAIRS, 2 * _N_PAIRS, 3 * _N_PAIRS, 3 * _N_PAIRS + _N_STATE
    own_bias = [jnp.concatenate([bias_ref[0, 0, p, :, BAND:2 * BAND], bias_ref[0, 0, p, :, 3 * BAND:]], axis=1)
                for p in range(_N_PAIRS)]
    ones_bd = _pair_ones(BAND)
    for r_outer in range(_DEINTERLEAVE):
        for n, (ref, idx) in enumerate(sources):
            in_scr[n] = ref[0, idx, pl.ds(r_outer, n_stage, stride=_DEINTERLEAVE), :]
        rows_of = [pl.ds(r_inner, BAND, stride=inner) for r_inner in range(inner)]
        load = lambda n, rows: in_scr[n, rows, :].astype(BF)
        units, where = [], []
        for rows in rows_of:
            for p in range(_N_PAIRS):
                units.append((load(q0 + p, rows), load(k0 + p, rows), load(v0 + p, rows), own_bias[p]))
                where.append((rows, p))
        for (rows, p), (out3, lse3) in zip(where, _band_pairs(units, ones_bd)):
            lse1, lse2 = in_scr[s1 + _N_PAIRS + p, rows, :], in_scr[s2 + _N_PAIRS + p, rows, :]
            top = jnp.maximum(jnp.maximum(lse1, lse2), lse3)
            w1, w2, w3 = jnp.exp(lse1 - top), jnp.exp(lse2 - top), jnp.exp(lse3 - top)
            num = w1 * in_scr[s1 + p, rows, :] + w2 * in_scr[s2 + p, rows, :] + w3 * out3
            out_scr[p, rows, :] = num / (w1 + w2 + w3)
        for p in range(_N_PAIRS):
            o_ref[0, p, pl.ds(r_outer, n_stage, stride=_DEINTERLEAVE), :] = out_scr[p]


def _band_merge(q, k, v, st1, st2, bias_tiles, g, dil):
    bsz, n_lt, s_len, _ = q.shape
    assert s_len == BAND * dil and dil % _DEINTERLEAVE == 0
    full = lambda n: pl.BlockSpec((1, n, s_len, LANES), lambda b: (b, 0, 0, 0))
    n_stage = s_len // _DEINTERLEAVE
    return pl.pallas_call(
        functools.partial(_band_merge_kernel, dil=dil),
        grid=(bsz,),
        in_specs=[full(n_lt), full(n_lt), full(n_lt), full(_N_STATE), full(_N_STATE),
                  pl.BlockSpec((1, 2, _N_PAIRS, BAND, 4 * BAND), lambda b: (g, 0, 0, 0, 0))],
        out_specs=full(n_lt),
        out_shape=jax.ShapeDtypeStruct((bsz, n_lt, s_len, LANES), F32),
        scratch_shapes=[pltpu.VMEM((3 * _N_PAIRS + 2 * _N_STATE, n_stage, LANES), F32),
                        pltpu.VMEM((_N_PAIRS, n_stage, LANES), F32)],
        compiler_params=_cparams("parallel"),
        name="band_merge",
    )(q, k, v, st1, st2, bias_tiles)


def _dilated(bq, bk, bv, bias_tiles):
    (_, d0), (_, d1), (_, d2) = B_PATTERNS
    st0 = _band_state(bq[0], bk, bv, bias_tiles, 0, d0)
    st1 = _band_state(bq[1], bk, bv, bias_tiles, 1, d1)
    return _band_merge(bq[2], bk, bv, st0, st1, bias_tiles, 2, d2)


def _block_mean_kernel(k_ref, o_ref):
    n_blk = k_ref.shape[1] // C_BLOCK
    o_ref[...] = jnp.zeros_like(o_ref)
    for j in range(n_blk):
        blk = k_ref[0, j * C_BLOCK:(j + 1) * C_BLOCK, :].astype(F32)
        o_ref[0, j:j + 1, :] = (jnp.sum(blk, axis=0, keepdims=True) * (1.0 / C_BLOCK)).astype(o_ref.dtype)


def _block_means(ck):
    bsz, s_len, w = ck.shape
    assert s_len // C_BLOCK <= BF16_ROWS
    return pl.pallas_call(
        _block_mean_kernel,
        grid=(bsz,),
        in_specs=[pl.BlockSpec((1, s_len, w), lambda b: (b, 0, 0))],
        out_specs=pl.BlockSpec((1, BF16_ROWS, w), lambda b: (b, 0, 0)),
        out_shape=jax.ShapeDtypeStruct((bsz, BF16_ROWS, w), BF),
        compiler_params=_cparams("parallel"),
        name="moba_block_means",
    )(ck)


def _moba_kernel(q_ref, k_ref, vt_ref, kmean_ref, bias_ref, _, o_ref, **static):
    for bb in range(q_ref.shape[0]):
        _moba_tile(*(_one_batch(r, bb) for r in (q_ref, k_ref, vt_ref, kmean_ref)), bias_ref,
                   _one_batch(o_ref, bb), **static)


def _moba_tile(q_ref, k_ref, vt_ref, kmean_ref, bias_ref, o_ref, *, cur, s_len):
    tq = q_ref.shape[1]
    t0 = (cur * TILES_PER_GROUP + pl.program_id(0)) * tq
    q = q_ref[0]
    kmean = kmean_ref[0]
    blk = lax.broadcasted_iota(jnp.int32, (BF16_ROWS, tq), 0)
    outs = []
    for p in range(C_HEADS // 2):
        ps = slice(p * LANES, (p + 1) * LANES)
        rhs = _head_pair_rhs(q[:, ps])
        logits = _dot_nt(k_ref[0, :, ps], rhs).astype(BF)
        gates = _dot_nt(kmean[:, ps], rhs)
        for half in range(2):
            h = 2 * p + half
            hq = slice(half * tq, (half + 1) * tq)
            if cur > C_TOPK:
                gate = jnp.where(blk < cur, gates[:, hq], -jnp.inf)
                off = jnp.where(blk == cur, 0.0, NEG)
                for j in range(cur):
                    gj = gate[j:j + 1, :]
                    beats = (gate > gj) | ((gate == gj) & (blk < j))
                    rank = jnp.sum(jnp.where(beats, 1.0, 0.0), axis=0, keepdims=True)
                    off = jnp.where((blk == j) & (rank < float(C_TOPK)), 0.0, off)
            pieces = []
            for j in range(cur + 1):
                ks = slice(j * C_BLOCK, (j + 1) * C_BLOCK)
                piece = logits[ks, hq] + _causal_bias(bias_ref, h, t0, j * C_BLOCK, C_BLOCK, tq, s_len)
                if cur > C_TOPK:
                    piece = piece + off[j:j + 1, :].astype(BF)
                pieces.append(piece)
            lg = jnp.concatenate(pieces, axis=0)
            outs.append(_softmax_pv(lg, _with_ones_rows(vt_ref[0, h * HEAD_DIM:(h + 1) * HEAD_DIM, :])))
    o_ref[0] = jnp.concatenate(outs, axis=0).T.astype(o_ref.dtype)


def _moba(cq, ck, cv_t, bias_master):
    bsz, s_len, w = cq.shape
    assert TILES_PER_GROUP * TQ == C_BLOCK
    kmean = _block_means(ck)
    out_shape = jax.ShapeDtypeStruct(cq.shape, BF)

    bps = _batch_per_step(bsz)

    def call_group(g, n_keys, out):
        tile0 = g * TILES_PER_GROUP
        return pl.pallas_call(
            functools.partial(_moba_kernel, cur=g, s_len=s_len),
            grid=(TILES_PER_GROUP, bsz // bps),
            in_specs=[pl.BlockSpec((bps, TQ, w), lambda i, b: (b, tile0 + i, 0)),
                      pl.BlockSpec((bps, n_keys, w), lambda i, b: (b, 0, 0)),
                      pl.BlockSpec((bps, w, n_keys), lambda i, b: (b, 0, 0)),
                      pl.BlockSpec((bps, BF16_ROWS, w), lambda i, b: (b, 0, 0)),
                      _const_spec2(bias_master.shape), pl.BlockSpec(memory_space=pl.ANY)],
            out_specs=pl.BlockSpec((bps, TQ, w), lambda i, b: (b, tile0 + i, 0)),
            out_shape=out_shape,
            input_output_aliases={5: 0},
            compiler_params=_cparams("arbitrary", "arbitrary"),
            name=f"moba_mixer_{g}",
        )(cq, ck, cv_t, kmean, bias_master, out)

    return _grouped_tiles(call_group, out_shape)


def _layer_norm(y, g_ref, b_ref):
    mu = jnp.mean(y, axis=-1, keepdims=True)
    yc = y - mu
    var = jnp.mean(yc * yc, axis=-1, keepdims=True)
    return yc * lax.rsqrt(var + LN_EPS) * g_ref[...] + b_ref[...]


def _merge_kernel(x_ref, oa_ref, ob_ref, oc_ref, wg_ref, wa_ref, wb_ref, wc_ref, wo_ref, g_ref, b_ref, y_ref):
    x = x_ref[...]
    xb = x.astype(BF)
    merged = None
    o_b = jnp.concatenate([ob_ref[0, lt] for lt in range(ob_ref.shape[1])], axis=1).astype(BF)
    for n, (o, w_ref) in enumerate(((oa_ref[...], wa_ref), (o_b, wb_ref), (oc_ref[...], wc_ref))):
        gate = jax.nn.sigmoid(_dot(xb, wg_ref[:, n * D_MODEL:(n + 1) * D_MODEL]))
        term = gate * _dot(o, w_ref[...])
        merged = term if merged is None else merged + term
    y = ALPHA * x + _dot(merged.astype(BF), wo_ref[...])
    y_ref[...] = _layer_norm(y, g_ref, b_ref)


def _const_spec(shape):
    return pl.BlockSpec(shape, lambda i: (0,) * len(shape), pipeline_mode=pl.Buffered(1))


def _merge(x2d, oa, ob, oc, wg, wa, wb, wc, wo, ln_g, ln_b, tm):
    n = x2d.shape[0]
    tiles_per_seq = ob.shape[2] // tm
    rows = lambda w: pl.BlockSpec((tm, w), lambda i: (i, 0))
    ob_spec = pl.BlockSpec((1, ob.shape[1], tm, LANES), lambda i: (i // tiles_per_seq, 0, i % tiles_per_seq, 0))
    return pl.pallas_call(
        _merge_kernel,
        grid=(n // tm,),
        in_specs=[rows(D_MODEL), rows(oa.shape[1]), ob_spec, rows(oc.shape[1]),
                  _const_spec(wg.shape), _const_spec(wa.shape), _const_spec(wb.shape), _const_spec(wc.shape),
                  _const_spec(wo.shape), _const_spec(ln_g.shape), _const_spec(ln_b.shape)],
        out_specs=rows(D_MODEL),
        out_shape=jax.ShapeDtypeStruct((n, D_MODEL), F32),
        compiler_params=_cparams("parallel"),
        name="merge_out_ln",
    )(x2d, oa, ob, oc, wg, wa, wb, wc, wo, ln_g, ln_b)


_FF_CHUNK = 1024


def _ffn_kernel(x_ref, p_ref, wu_ref, wd_ref, wpg_ref, wp_ref, g_ref, b_ref, y_ref):
    x = x_ref[...]
    xb = x.astype(BF)
    y = ALPHA * x + jax.nn.sigmoid(_dot(xb, wpg_ref[...])) * _dot(p_ref[...].astype(BF), wp_ref[...])
    for c in range(D_FF // _FF_CHUNK):
        cs = slice(c * _FF_CHUNK, (c + 1) * _FF_CHUNK)
        u = jnp.maximum(_dot(xb, wu_ref[:, cs]), 0.0)
        y = y + _dot((u * u).astype(BF), wd_ref[cs, :])
    y_ref[...] = _layer_norm(y, g_ref, b_ref)


def _ffn(x2d, p2d, wu, wd, wpg, wp, ln_g, ln_b, tm):
    n = x2d.shape[0]
    rows = lambda w: pl.BlockSpec((tm, w), lambda i: (i, 0))
    return pl.pallas_call(
        _ffn_kernel,
        grid=(n // tm,),
        in_specs=[rows(D_MODEL), rows(PLE_DIM), _const_spec(wu.shape), _const_spec(wd.shape),
                  _const_spec(wpg.shape), _const_spec(wp.shape), _const_spec(ln_g.shape), _const_spec(ln_b.shape)],
        out_specs=rows(D_MODEL),
        out_shape=jax.ShapeDtypeStruct((n, D_MODEL), F32),
        compiler_params=_cparams("parallel"),
        name="ffn_ple_ln",
    )(x2d, p2d, wu, wd, wpg, wp, ln_g, ln_b)


def kernel(x, p, w_in, w_gate, w_br_a, w_br_b, w_br_c, w_out, ln1_g, ln1_b,
           w_up, w_down, w_ple_gate, w_ple, ln2_g, ln2_b, rel_bias):
    bsz, s_len, d_model = x.shape
    assert d_model == D_MODEL and s_len == MAX_DISTANCE, (x.shape,)
    n_tok = bsz * s_len
    tm = 1024

    b_head0 = A_HEADS
    c_head0 = A_HEADS + B_GROUPS * B_SLOTS
    bias_a = _causal_bias_master(rel_bias, 0, A_HEADS, s_len, F32)
    bias_b = _band_bias_tiles(rel_bias, b_head0)
    bias_c = _causal_bias_master(rel_bias, c_head0, C_HEADS, s_len, BF)

    x2d = x.reshape(n_tok, D_MODEL)
    for i in range(DEPTH):
        pr = _project(x2d, *_pack_w_in(w_in[i]), tm, s_len)
        seq = lambda name: pr[name].reshape(bsz, s_len, -1)
        o_a = _dsa(seq("aq"), seq("iq"), pr["iwT"], seq("akk"), seq("aii"), pr["avT"], bias_a)
        o_b = _dilated((pr["bq0"], pr["bq1"], pr["bq2"]), pr["bk"], pr["bv"], bias_b)
        o_c = _moba(seq("cq"), seq("ck"), pr["cvT"], bias_c)
        flat = lambda a: a.reshape(n_tok, -1)
        row = lambda a: a.reshape(1, D_MODEL)
        x2d = _merge(x2d, flat(o_a), o_b, flat(o_c), w_gate[i].astype(BF), w_br_a[i].astype(BF),
                     w_br_b[i].astype(BF), w_br_c[i].astype(BF), w_out[i].astype(BF),
                     row(ln1_g[i]), row(ln1_b[i]), tm)
        x2d = _ffn(x2d, p[i].reshape(n_tok, PLE_DIM), w_up[i].astype(BF), w_down[i].astype(BF),
                   w_ple_gate[i].astype(BF), w_ple[i].astype(BF), row(ln2_g[i]), row(ln2_b[i]), tm)
    return x2d.reshape(bsz, s_len, D_MODEL)
```
